```python
import math
import jax, jax.numpy as jnp
from jax import lax
import numpy as np

D_MODEL = 1024
BATCH = 16
SEQ = 256
DEPTH = 2
DEC_BATCH = 8
DEC_SEQ = 4096
PAST_LEN = 512

GRID_W = 64
ROPE_BASE = 10000.0
NORM_EPS = 1e-6
Q_BLOCK = 128

N_AB = (DEPTH + 1) // 2
N_C = DEPTH // 2

MLA_HEADS = 8
MLA_NOPE = 64
MLA_ROPE = 32
MLA_V = 64
MLA_Q_LORA = 384
MLA_KV_LORA = 256
GLA_HEADS = 4
GLA_DK = 128
GLA_DV = 128
GLA_GATE_RANK = 16
GLA_GATE_NORM = 16.0
GLA_CHUNK = 64
DIFF_HEADS = 8
DIFF_HEAD_DIM = 64
FFN_HIDDEN = -(-8 * D_MODEL // (3 * 256)) * 256

AB_SIZES = (MLA_Q_LORA, MLA_KV_LORA, MLA_ROPE, GLA_HEADS * GLA_DK, GLA_HEADS * GLA_DK,
            GLA_HEADS * GLA_DV, 2 * GLA_GATE_RANK, GLA_HEADS * GLA_DV)
AB_IN = sum(AB_SIZES)
AB_SPLIT_IDX = tuple(int(i) for i in np.cumsum(AB_SIZES)[:-1])
AB_MIX = MLA_HEADS * MLA_V + GLA_HEADS * GLA_DV
C_MIX = DIFF_HEADS * 2 * DIFF_HEAD_DIM

kernel_name = 'hybrid_mla_gla_diffattn_prefix_context_dit_step'


def rms_norm(x, g):
    xf = x.astype(jnp.float32)
    y = xf * lax.rsqrt(jnp.mean(xf * xf, axis=-1, keepdims=True) + NORM_EPS)
    return (y * g.astype(jnp.float32)).astype(x.dtype)


def axial_rope(n_tokens, rot_dim):
    rows = n_tokens // GRID_W
    row = jnp.repeat(jnp.arange(rows, dtype=jnp.float32), GRID_W)
    col = jnp.tile(jnp.arange(GRID_W, dtype=jnp.float32), rows)
    n_freq = rot_dim // 4
    inv = ROPE_BASE ** (-jnp.arange(n_freq, dtype=jnp.float32) / n_freq)
    ang = jnp.concatenate([row[:, None] * inv, col[:, None] * inv], axis=-1)
    return jnp.cos(ang), jnp.sin(ang)


def apply_rope(x, rope):
    cos, sin = rope
    n_tok, half = x.shape[1], x.shape[-1] // 2
    shape = (1, n_tok) + (1,) * (x.ndim - 3) + (half,)
    c, s = cos.reshape(shape), sin.reshape(shape)
    xf = x.astype(jnp.float32).reshape(x.shape[:-1] + (half, 2))
    xe, xo = xf[..., 0], xf[..., 1]
    out = jnp.stack([xe * c - xo * s, xe * s + xo * c], axis=-1).reshape(x.shape)
    return out.astype(x.dtype)


def sweep_query_blocks(fn, q):
    b, lq = q.shape[:2]
    nb = lq // Q_BLOCK
    qb = jnp.moveaxis(q.reshape((b, nb, Q_BLOCK) + q.shape[2:]), 1, 0)
    out = jnp.moveaxis(lax.map(fn, qb), 0, 1)
    return out.reshape((b, lq) + out.shape[3:])


def gla_chunked(q, k, v, g, s0):
    b, n_tok, h, _ = q.shape
    dv = v.shape[-1]
    n = n_tok // GLA_CHUNK

    def chunks(a):
        return a.astype(jnp.float32).reshape(b, n, GLA_CHUNK, h, a.shape[-1]).transpose(0, 3, 1, 2, 4)

    q, k, v, g = chunks(q), chunks(k), chunks(v), chunks(g)
    cum = jnp.cumsum(g, axis=3)
    cum_last = cum[:, :, :, -1:, :]
    q_in = q * jnp.exp(cum)
    k_in = k * jnp.exp(-cum)
    mask = jnp.tril(jnp.ones((GLA_CHUNK, GLA_CHUNK), dtype=bool))
    a = jnp.where(mask, jnp.einsum('bhncd,bhnsd->bhncs', q_in, k_in), 0.0)
    o_intra = jnp.einsum('bhncs,bhnsv->bhncv', a, v)
    u = jnp.einsum('bhncd,bhncv->bhndv', k * jnp.exp(cum_last - cum), v)
    decay = jnp.exp(cum_last[:, :, :, 0, :])

    def step(s, xs):
        dec, u_n = xs
        return dec[..., None] * s + u_n, s

    s_final, s_prev = lax.scan(step, s0.astype(jnp.float32),
                               (jnp.moveaxis(decay, 2, 0), jnp.moveaxis(u, 2, 0)))
    s_prev = jnp.moveaxis(s_prev, 0, 2)
    o = o_intra + jnp.einsum('bhncd,bhndv->bhncv', q_in, s_prev)
    return o.transpose(0, 2, 3, 1, 4).reshape(b, n_tok, h, dv), s_final


def mla_mix(q_lat, kv_lat, k_rope, ai, P, rope, ctx):
    b, n_tok, _ = q_lat.shape
    q = (rms_norm(q_lat, P['mla_g_q'][ai]) @ P['mla_w_uq'][ai]).reshape(b, n_tok, MLA_HEADS, MLA_NOPE + MLA_ROPE)
    q_nope, q_rope = q[..., :MLA_NOPE], q[..., MLA_NOPE:]
    ckv = rms_norm(kv_lat, P['mla_g_kv'][ai])
    if rope is not None:
        q_rope = apply_rope(q_rope, rope)
        k_rope = apply_rope(k_rope, rope)
    ckv_all, krope_all = ckv, k_rope
    if ctx is not None:
        ckv_all = jnp.concatenate([ctx[0].astype(ckv.dtype), ckv], axis=1)
        krope_all = jnp.concatenate([ctx[1].astype(k_rope.dtype), k_rope], axis=1)
    n_keys = ckv_all.shape[1]
    kv = (ckv_all @ P['mla_w_ukv'][ai]).reshape(b, n_keys, MLA_HEADS, MLA_NOPE + MLA_V)
    k_nope, v = kv[..., :MLA_NOPE], kv[..., MLA_NOPE:]
    scale = (MLA_NOPE + MLA_ROPE) ** -0.5

    def attend(qb):
        s = (jnp.einsum('bqhd,bkhd->bhqk', qb[..., :MLA_NOPE], k_nope)
             + jnp.einsum('bqhr,bkr->bhqk', qb[..., MLA_NOPE:], krope_all))
        p = jax.nn.softmax(s.astype(jnp.float32) * scale, axis=-1)
        return jnp.einsum('bhqk,bkhd->bqhd', p.astype(v.dtype), v)

    o = sweep_query_blocks(attend, jnp.concatenate([q_nope, q_rope], axis=-1))
    return o.reshape(b, n_tok, MLA_HEADS * MLA_V), (ckv, k_rope)


def gla_mix(gq, gk, gv, ggate, gout, ai, P, ctx):
    b, n_tok, _ = gq.shape
    q = gq.reshape(b, n_tok, GLA_HEADS, GLA_DK) * (GLA_DK ** -0.5)
    k = gk.reshape(b, n_tok, GLA_HEADS, GLA_DK)
    v = gv.reshape(b, n_tok, GLA_HEADS, GLA_DV)
    lr = ggate.reshape(b, n_tok, 2, GLA_GATE_RANK)
    pre = jnp.einsum('bldr,drk->bldk', lr, P['gla_w_gate_up'][ai]) + P['gla_b_gate'][ai]
    g = (jax.nn.log_sigmoid(pre.astype(jnp.float32)) / GLA_GATE_NORM).reshape(b, n_tok, 2, GLA_HEADS, GLA_DK)
    if ctx is None:
        s0f = jnp.zeros((b, GLA_HEADS, GLA_DK, GLA_DV), jnp.float32)
        s0b = s0f
    else:
        s0f, s0b = ctx
    o_f, s_f = gla_chunked(q, k, v, g[:, :, 0], s0f)
    o_b, s_b = gla_chunked(jnp.flip(q, 1), jnp.flip(k, 1), jnp.flip(v, 1), jnp.flip(g[:, :, 1], 1), s0b)
    o = (o_f + jnp.flip(o_b, 1)).astype(gq.dtype)
    o = rms_norm(o, P['gla_g_out'][ai]) * jax.nn.silu(gout.reshape(b, n_tok, GLA_HEADS, GLA_DV))
    return o.reshape(b, n_tok, GLA_HEADS * GLA_DV), (s_f, s_b)


def ab_mix(h, ai, P, rope, ctx):
    parts = jnp.split(h @ P['ab_w_in'][ai], AB_SPLIT_IDX, axis=-1)
    q_lat, kv_lat, k_rope, gq, gk, gv, ggate, gout = parts
    mla_out, (ckv, krope) = mla_mix(q_lat, kv_lat, k_rope, ai, P, rope,
                                    None if ctx is None else ctx[:2])
    gla_out, (s_f, s_b) = gla_mix(gq, gk, gv, ggate, gout, ai, P,
                                  None if ctx is None else ctx[2:])
    out = jnp.concatenate([mla_out, gla_out.astype(mla_out.dtype)], axis=-1) @ P['ab_w_out'][ai]
    return out, (ckv, krope, s_f, s_b)


def diff_lambda_init(layer):
    return 0.8 - 0.6 * math.exp(-0.3 * layer)


def diff_mix(h, li, ci, P, rope, ctx):
    b, n_tok, _ = h.shape
    q, k, v = jnp.split(h @ P['c_w_qkv'][ci], 3, axis=-1)
    q = q.reshape(b, n_tok, DIFF_HEADS, 2, DIFF_HEAD_DIM)
    k = k.reshape(b, n_tok, DIFF_HEADS, 2, DIFF_HEAD_DIM)
    v = v.reshape(b, n_tok, DIFF_HEADS, 2 * DIFF_HEAD_DIM)
    if rope is not None:
        q = apply_rope(q, rope)
        k = apply_rope(k, rope)
    k_all, v_all = k, v
    if ctx is not None:
        k_all = jnp.concatenate([ctx[0].astype(k.dtype), k], axis=1)
        v_all = jnp.concatenate([ctx[1].astype(v.dtype), v], axis=1)
    lp = P['diff_lambda'][ci].astype(jnp.float32)
    lam_init = diff_lambda_init(li)
    lam = jnp.exp(jnp.sum(lp[0] * lp[1])) - jnp.exp(jnp.sum(lp[2] * lp[3])) + lam_init
    scale = DIFF_HEAD_DIM ** -0.5

    def attend(qb):
        s = jnp.einsum('bqhjd,bkhjd->bhjqk', qb, k_all).astype(jnp.float32) * scale
        p = jax.nn.softmax(s, axis=-1)
        a = p[:, :, 0] - lam * p[:, :, 1]
        return jnp.einsum('bhqk,bkhe->bqhe', a.astype(v_all.dtype), v_all)

    o = sweep_query_blocks(attend, q)
    o = rms_norm(o, P['diff_g_out'][ci]) * (1.0 - lam_init)
    return o.reshape(b, n_tok, C_MIX) @ P['c_w_out'][ci], (k, v)


def trunk_layer(x, li, cond, P, rope, ctx):
    mod = (jax.nn.silu(cond) @ P['w_mod'][li] + P['b_mod'][li])[:, None, :]
    shift1, scale1, gate1, shift2, scale2, gate2 = jnp.split(mod, 6, axis=-1)
    g = P['g_norm'][li]
    h = rms_norm(x, g[0]) * (1.0 + scale1) + shift1
    if li % 2 == 0:
        out, new = ab_mix(h, li // 2, P, rope, ctx)
    else:
        out, new = diff_mix(h, li, li // 2, P, rope, ctx)
    x = x + gate1 * rms_norm(out, g[1])
    h = rms_norm(x, g[2]) * (1.0 + scale2) + shift2
    gate, up = jnp.split(h @ P['w_ffn_in'][li], 2, axis=-1)
    f = (jax.nn.silu(gate) * up) @ P['w_ffn_out'][li]
    x = x + gate2 * rms_norm(f, g[3])
    return x, new


def _normal(k, shape, scale):
    return jax.random.normal(k, shape, jnp.float32) * scale


def setup_inputs(seed: int = 0) -> dict:
    key = jax.random.key(seed)
    ks = jax.random.split(key, 32)
    d = D_MODEL
    return {
        'x_prompt': _normal(ks[0], (BATCH, SEQ, d), 1.0),
        'x_sample': _normal(ks[1], (DEC_BATCH, DEC_SEQ, d), 1.0),
        'cache_mla_ckv': _normal(ks[2], (DEC_BATCH, N_AB, PAST_LEN, MLA_KV_LORA), 1.0),
        'cache_mla_krope': _normal(ks[3], (DEC_BATCH, N_AB, PAST_LEN, MLA_ROPE), 1.0),
        'state_gla_fwd': _normal(ks[4], (DEC_BATCH, N_AB, GLA_HEADS, GLA_DK, GLA_DV), 1.0),
        'state_gla_bwd': _normal(ks[5], (DEC_BATCH, N_AB, GLA_HEADS, GLA_DK, GLA_DV), 1.0),
        'cache_diff_k': _normal(ks[6], (DEC_BATCH, N_C, PAST_LEN, DIFF_HEADS, 2, DIFF_HEAD_DIM), 1.0),
        'cache_diff_v': _normal(ks[7], (DEC_BATCH, N_C, PAST_LEN, DIFF_HEADS, 2 * DIFF_HEAD_DIM), 1.0),
        'c': _normal(ks[8], (DEC_BATCH, d), 1.0),
        'c_ctx': _normal(ks[9], (d,), 1.0),
        'w_mod': _normal(ks[10], (DEPTH, d, 6 * d), d ** -0.5),
        'b_mod': _normal(ks[11], (DEPTH, 6 * d), 0.02),
        'g_norm': 1.0 + _normal(ks[12], (DEPTH, 4, d), 0.02),
        'ab_w_in': _normal(ks[13], (N_AB, d, AB_IN), d ** -0.5),
        'mla_g_q': 1.0 + _normal(ks[14], (N_AB, MLA_Q_LORA), 0.02),
        'mla_g_kv': 1.0 + _normal(ks[15], (N_AB, MLA_KV_LORA), 0.02),
        'mla_w_uq': _normal(ks[16], (N_AB, MLA_Q_LORA, MLA_HEADS * (MLA_NOPE + MLA_ROPE)), MLA_Q_LORA ** -0.5),
        'mla_w_ukv': _normal(ks[17], (N_AB, MLA_KV_LORA, MLA_HEADS * (MLA_NOPE + MLA_V)), MLA_KV_LORA ** -0.5),
        'gla_w_gate_up': _normal(ks[18], (N_AB, 2, GLA_GATE_RANK, GLA_HEADS * GLA_DK), GLA_GATE_RANK ** -0.5),
        'gla_b_gate': _normal(ks[19], (N_AB, 2, GLA_HEADS * GLA_DK), 0.1),
        'gla_g_out': 1.0 + _normal(ks[20], (N_AB, GLA_DV), 0.02),
        'ab_w_out': _normal(ks[21], (N_AB, AB_MIX, d), AB_MIX ** -0.5),
        'c_w_qkv': _normal(ks[22], (N_C, d, 3 * C_MIX), d ** -0.5),
        'diff_lambda': _normal(ks[23], (N_C, 4, DIFF_HEAD_DIM), 0.1),
        'diff_g_out': 1.0 + _normal(ks[24], (N_C, 2 * DIFF_HEAD_DIM), 0.02),
        'c_w_out': _normal(ks[25], (N_C, C_MIX, d), C_MIX ** -0.5),
        'w_ffn_in': _normal(ks[26], (DEPTH, d, 2 * FFN_HIDDEN), d ** -0.5),
        'w_ffn_out': _normal(ks[27], (DEPTH, FFN_HIDDEN, d), FFN_HIDDEN ** -0.5),
    }


def reference(x_prompt, x_sample, cache_mla_ckv, cache_mla_krope, state_gla_fwd, state_gla_bwd,
              cache_diff_k, cache_diff_v, c, c_ctx, w_mod, b_mod, g_norm, ab_w_in, mla_g_q, mla_g_kv,
              mla_w_uq, mla_w_ukv, gla_w_gate_up, gla_b_gate, gla_g_out, ab_w_out, c_w_qkv, diff_lambda,
              diff_g_out, c_w_out, w_ffn_in, w_ffn_out):
    P = dict(w_mod=w_mod, b_mod=b_mod, g_norm=g_norm, ab_w_in=ab_w_in, mla_g_q=mla_g_q, mla_g_kv=mla_g_kv,
             mla_w_uq=mla_w_uq, mla_w_ukv=mla_w_ukv, gla_w_gate_up=gla_w_gate_up, gla_b_gate=gla_b_gate,
             gla_g_out=gla_g_out, ab_w_out=ab_w_out, c_w_qkv=c_w_qkv, diff_lambda=diff_lambda,
             diff_g_out=diff_g_out, c_w_out=c_w_out, w_ffn_in=w_ffn_in, w_ffn_out=w_ffn_out)

    y = x_prompt
    ab_states, c_states = [], []
    for li in range(DEPTH):
        y, new = trunk_layer(y, li, c_ctx[None, :], P, None, None)
        if li % 2 == 0:
            ab_states.append(new)
        else:
            c_states.append(new)
    y_prompt = y

    n_lat = x_sample.shape[1]
    rope_mla = axial_rope(n_lat, MLA_ROPE)
    rope_diff = axial_rope(n_lat, DIFF_HEAD_DIM)
    z = x_sample
    for li in range(DEPTH):
        i = li // 2
        if li % 2 == 0:
            ctx = (cache_mla_ckv[:, i], cache_mla_krope[:, i], state_gla_fwd[:, i], state_gla_bwd[:, i])
            z, _ = trunk_layer(z, li, c, P, rope_mla, ctx)
        else:
            ctx = (cache_diff_k[:, i], cache_diff_v[:, i])
            z, _ = trunk_layer(z, li, c, P, rope_diff, ctx)
    y_sample = z

    new_mla_ckv = jnp.stack([s[0] for s in ab_states], axis=1)
    new_mla_krope = jnp.stack([s[1] for s in ab_states], axis=1)
    new_gla_fwd = jnp.stack([s[2] for s in ab_states], axis=1)
    new_gla_bwd = jnp.stack([s[3] for s in ab_states], axis=1)
    new_diff_k = jnp.stack([s[0] for s in c_states], axis=1)
    new_diff_v = jnp.stack([s[1] for s in c_states], axis=1)
    return (y_prompt, y_sample, new_mla_ckv, new_mla_krope, new_gla_fwd, new_gla_bwd, new_diff_k, new_diff_v)
```

```python
import functools
import math

import numpy as np
import jax
import jax.numpy as jnp
from jax import lax
from jax.experimental import pallas as pl
from jax.experimental.pallas import tpu as pltpu

F32 = jnp.float32
BF16 = jnp.bfloat16

D_MODEL = 1024
GRID_W = 64
ROPE_BASE = 10000.0
NORM_EPS = 1e-6

MLA_HEADS = 8
MLA_NOPE = 64
MLA_ROPE = 32
MLA_V = 64
MLA_Q_LORA = 384
MLA_KV_LORA = 256
GLA_HEADS = 4
GLA_DK = 128
GLA_DV = 128
GLA_GATE_RANK = 16
GLA_GATE_NORM = 16.0
GLA_CHUNK = 64
DIFF_HEADS = 8
DIFF_HEAD_DIM = 64
FFN_HIDDEN = 2816

LANES = 128
VMEM_LIMIT = 56 * 1024 * 1024

_OFF_QLAT = 0
_OFF_KVLAT = _OFF_QLAT + MLA_Q_LORA
_OFF_GQ = _OFF_KVLAT + MLA_KV_LORA
_OFF_GK = _OFF_GQ + GLA_HEADS * GLA_DK
_OFF_GV = _OFF_GK + GLA_HEADS * GLA_DK
_OFF_GOUT = _OFF_GV + GLA_HEADS * GLA_DV
_OFF_MISC = _OFF_GOUT + GLA_HEADS * GLA_DV
_AB_COLS = _OFF_MISC + LANES
_MISC_GATE = MLA_ROPE

TOKEN_TILE = 512
ATTN_Q_TILE = 256
ATTN_K_TILE = 512
GLA_BLOCK = 256
FFN_SPLIT = 2


def _rms(x, g):
    var = jnp.mean(x * x, axis=-1, keepdims=True)
    return x * lax.rsqrt(var + NORM_EPS) * g


def _silu(x):
    return x * (1.0 / (1.0 + jnp.exp(-x)))


def _log_sigmoid(x):
    return -(jnp.maximum(-x, 0.0) + jnp.log1p(jnp.exp(-jnp.abs(x))))


def _rope(x, c, se, so):
    n = x.shape[-1]
    return x * c + pltpu.roll(x, n - 1, 1) * se + pltpu.roll(x, 1, 1) * so


def _dot(a, b):
    return jnp.dot(a, b, preferred_element_type=F32)


def _dot_nt(a, b):
    return lax.dot_general(a, b, (((1,), (1,)), ((), ())), preferred_element_type=F32)


def _dot_tn(a, b):
    return lax.dot_general(a, b, (((0,), (0,)), ((), ())), preferred_element_type=F32)


def _params(*sem):
    return pltpu.CompilerParams(dimension_semantics=sem, vmem_limit_bytes=VMEM_LIMIT)


def _full(shape):
    zeros = (0,) * len(shape)
    return pl.BlockSpec(shape, lambda *_: zeros)


def _rope_tables(n_tokens, rot_dim, lane_lo, period):
    rows = n_tokens // GRID_W
    row = np.repeat(np.arange(rows, dtype=np.float64), GRID_W)
    col = np.tile(np.arange(GRID_W, dtype=np.float64), rows)
    n_freq = rot_dim // 4
    inv = ROPE_BASE ** (-np.arange(n_freq, dtype=np.float64) / n_freq)
    ang = np.concatenate([row[:, None] * inv, col[:, None] * inv], axis=-1)
    cos, sin = np.cos(ang), np.sin(ang)
    c = np.ones((n_tokens, period))
    se = np.zeros((n_tokens, period))
    so = np.zeros((n_tokens, period))
    c[:, lane_lo:lane_lo + rot_dim] = np.repeat(cos, 2, axis=-1)
    se[:, lane_lo:lane_lo + rot_dim:2] = -sin
    so[:, lane_lo + 1:lane_lo + rot_dim:2] = sin
    reps = LANES // period
    return tuple(jnp.asarray(np.tile(t, (1, reps)), dtype=F32) for t in (c, se, so))


def _mod_kernel(c_ref, w_ref, b_ref, o_ref):
    a = _silu(c_ref[...]).astype(BF16)
    o_ref[0] = _dot(a, w_ref[0].astype(BF16)) + b_ref[0]


def _modulation(conds, w_mod, b_mod):
    depth, d, n = w_mod.shape
    rows = conds.shape[0]
    tn = 1536
    return pl.pallas_call(
        _mod_kernel,
        grid=(depth, n // tn),
        in_specs=[_full((rows, d)),
                  pl.BlockSpec((1, d, tn), lambda l, j: (l, 0, j)),
                  pl.BlockSpec((1, 1, tn), lambda l, j: (l, 0, j))],
        out_specs=pl.BlockSpec((1, rows, tn), lambda l, j: (l, 0, j)),
        out_shape=jax.ShapeDtypeStruct((depth, rows, n), F32),
        compiler_params=_params("arbitrary", "arbitrary"),
        name="modulation",
    )(conds, w_mod, b_mod.reshape(depth, 1, n))


def _ab_in_kernel(use_rope, *refs):
    (x_ref, mod_ref, g0_ref, win_ref, gq_ref, gkv_ref, wq_ref, wk_ref, wv_ref, e_ref) = refs[:10]
    refs = refs[10:]
    if use_rope:
        cq, seq, soq, cm, sem, som = refs[:6]
        refs = refs[6:]
    (q_out, kcat_out, v_out, ckv_out, misc_out, gq_out, gk_out, gv_out, gout_out) = refs

    mod = mod_ref[0]
    shift, scale = mod[:, 0:D_MODEL], mod[:, D_MODEL:2 * D_MODEL]
    h = _rms(x_ref[...], g0_ref[...]) * (1.0 + scale) + shift
    proj = _dot(h.astype(BF16), win_ref[...])

    gq_out[...] = (proj[:, _OFF_GQ:_OFF_GK] * (GLA_DK ** -0.5)).astype(BF16)
    gk_out[...] = proj[:, _OFF_GK:_OFF_GV].astype(BF16)
    gv_out[...] = proj[:, _OFF_GV:_OFF_GOUT].astype(BF16)
    gout_out[...] = proj[:, _OFF_GOUT:_OFF_MISC].astype(BF16)

    misc = proj[:, _OFF_MISC:_AB_COLS]
    if use_rope:
        misc = _rope(misc, cm[...], sem[...], som[...])
    misc_out[...] = misc

    ckv = _rms(proj[:, _OFF_KVLAT:_OFF_GQ], gkv_ref[...])
    ckv_out[...] = ckv
    ckv_b = ckv.astype(BF16)
    kcat_out[...] = (_dot(ckv_b, wk_ref[...]) + _dot(misc.astype(BF16), e_ref[...])).astype(BF16)
    v_out[...] = _dot(ckv_b, wv_ref[...]).astype(BF16)

    qn = _rms(proj[:, _OFF_QLAT:_OFF_KVLAT], gq_ref[...]).astype(BF16)
    q = _dot(qn, wq_ref[...])
    sm_scale = (MLA_NOPE + MLA_ROPE) ** -0.5
    for hd in range(MLA_HEADS):
        sl = slice(hd * LANES, (hd + 1) * LANES)
        qh = q[:, sl]
        if use_rope:
            qh = _rope(qh, cq[...], seq[...], soq[...])
        q_out[:, sl] = (qh * sm_scale).astype(BF16)


def _ab_in(x, mod3, modrow, g0, w, rope, tm, seq_len):
    t = x.shape[0]
    nt = t // tm
    tok = lambda n: pl.BlockSpec((tm, n), lambda i: (i, 0))
    in_specs = [tok(D_MODEL),
                pl.BlockSpec((1, 1, 6 * D_MODEL), lambda i: (modrow(i), 0, 0)),
                _full((1, D_MODEL)), _full((D_MODEL, _AB_COLS)),
                _full((1, MLA_Q_LORA)), _full((1, MLA_KV_LORA)),
                _full((MLA_Q_LORA, MLA_HEADS * LANES)), _full((MLA_KV_LORA, MLA_HEADS * LANES)),
                _full((MLA_KV_LORA, MLA_HEADS * MLA_V)), _full((LANES, MLA_HEADS * LANES))]
    args = [x, mod3, g0, w["win"], w["g_q"], w["g_kv"], w["wq"], w["wk"], w["wv"], w["e"]]
    if rope is not None:
        per = seq_len // tm
        in_specs += [pl.BlockSpec((tm, LANES), lambda i: (i % per, 0))] * 6
        args += list(rope)
    widths = [(MLA_HEADS * LANES, BF16), (MLA_HEADS * LANES, BF16), (MLA_HEADS * MLA_V, BF16),
              (MLA_KV_LORA, F32), (LANES, F32)] + [(GLA_HEADS * GLA_DK, BF16)] * 4
    return pl.pallas_call(
        functools.partial(_ab_in_kernel, rope is not None),
        grid=(nt,),
        in_specs=in_specs,
        out_specs=[tok(n) for n, _ in widths],
        out_shape=[jax.ShapeDtypeStruct((t, n), dt) for n, dt in widths],
        compiler_params=_params("arbitrary"),
        name="ab_in",
    )(*args)


def _ctx_kv_kernel(ckv_ref, kr_ref, wk_ref, wv_ref, e_ref, kcat_out, v_out):
    c = ckv_ref[...].astype(BF16)
    kr = kr_ref[...].astype(BF16)
    kcat_out[...] = (_dot(c, wk_ref[...]) + _dot(kr, e_ref[...])).astype(BF16)
    v_out[...] = _dot(c, wv_ref[...]).astype(BF16)


def _ctx_kv(ckv, krope, w, tm):
    t = ckv.shape[0]
    tok = lambda n: pl.BlockSpec((tm, n), lambda i: (i, 0))
    return pl.pallas_call(
        _ctx_kv_kernel,
        grid=(t // tm,),
        in_specs=[tok(MLA_KV_LORA), tok(MLA_ROPE),
                  _full((MLA_KV_LORA, MLA_HEADS * LANES)), _full((MLA_KV_LORA, MLA_HEADS * MLA_V)),
                  _full((MLA_ROPE, MLA_HEADS * LANES))],
        out_specs=[tok(MLA_HEADS * LANES), tok(MLA_HEADS * MLA_V)],
        out_shape=[jax.ShapeDtypeStruct((t, MLA_HEADS * LANES), BF16),
                   jax.ShapeDtypeStruct((t, MLA_HEADS * MLA_V), BF16)],
        compiler_params=_params("arbitrary"),
        name="mla_ctx_kv",
    )(ckv, krope, w["wk"], w["wv"], w["e"][:MLA_ROPE])


_NEG = -1e30


def _mla_attn_kernel(has_ctx, tk_new, *refs):
    if has_ctx:
        q_ref, kn_ref, vn_ref, kc_ref, vc_ref, o_ref = refs
    else:
        q_ref, kn_ref, vn_ref, o_ref = refs
    tq = q_ref.shape[1]
    lane = lax.broadcasted_iota(jnp.int32, (tq, LANES), 1)

    for hp in range(MLA_HEADS // 2):
        vsl = slice(hp * LANES, (hp + 1) * LANES)
        outs = []
        for e in range(2):
            hd = 2 * hp + e
            ksl = slice(hd * LANES, (hd + 1) * LANES)
            qh = q_ref[0, :, ksl]

            def make_body(kref, vref, tk):
                def body(c, carry):
                    m, l, acc = carry
                    r0 = pl.multiple_of(c * tk, tk)
                    s = _dot_nt(qh, kref[0, pl.ds(r0, tk), ksl])
                    m_new = jnp.maximum(m, jnp.max(s, axis=-1, keepdims=True))
                    alpha = jnp.exp(m - m_new)
                    p = jnp.exp(s - m_new)
                    l = alpha * l + jnp.sum(p, axis=-1, keepdims=True)
                    acc = alpha * acc + _dot(p.astype(BF16), vref[0, pl.ds(r0, tk), vsl])
                    return m_new, l, acc
                return body

            carry = (jnp.full((tq, 1), _NEG, F32), jnp.zeros((tq, 1), F32), jnp.zeros((tq, LANES), F32))
            if has_ctx:
                n_ctx = kc_ref.shape[1]
                tk_ctx = min(ATTN_K_TILE, n_ctx)
                carry = lax.fori_loop(0, n_ctx // tk_ctx, make_body(kc_ref, vc_ref, tk_ctx), carry)
            carry = lax.fori_loop(0, kn_ref.shape[1] // tk_new, make_body(kn_ref, vn_ref, tk_new), carry)
            _, l, acc = carry
            outs.append(acc / l)
        o_ref[0, :, vsl] = jnp.where(lane < MLA_V, outs[0], outs[1]).astype(BF16)


def _mla_attn(q, k_new, v_new, ctx, tq):
    b, n, _ = q.shape
    tk_new = min(ATTN_K_TILE, n)
    kw, vw = MLA_HEADS * LANES, MLA_HEADS * MLA_V
    in_specs = [pl.BlockSpec((1, tq, kw), lambda bi, qi: (bi, qi, 0)),
                pl.BlockSpec((1, n, kw), lambda bi, qi: (bi, 0, 0)),
                pl.BlockSpec((1, n, vw), lambda bi, qi: (bi, 0, 0))]
    args = [q, k_new, v_new]
    if ctx is not None:
        nc = ctx[0].shape[1]
        in_specs += [pl.BlockSpec((1, nc, kw), lambda bi, qi: (bi, 0, 0)),
                     pl.BlockSpec((1, nc, vw), lambda bi, qi: (bi, 0, 0))]
        args += list(ctx)
    return pl.pallas_call(
        functools.partial(_mla_attn_kernel, ctx is not None, tk_new),
        grid=(b, n // tq),
        in_specs=in_specs,
        out_specs=pl.BlockSpec((1, tq, vw), lambda bi, qi: (bi, qi, 0)),
        out_shape=jax.ShapeDtypeStruct((b, n, vw), BF16),
        compiler_params=_params("arbitrary", "arbitrary"),
        name="mla_attn",
    )(*args)


def _gla_kernel(has_s0, *refs):
    fwd_in, bwd_in = refs[0:4], refs[4:8]
    wg_ref, bg_ref, tf_ref, tb_ref = refs[8:12]
    refs = refs[12:]
    if has_s0:
        s0f_ref, s0b_ref = refs[:2]
        refs = refs[2:]
    of_ref, ob_ref, sf_ref, sb_ref = refs
    tb_rows = of_ref.shape[1]
    nch = tb_rows // GLA_CHUNK

    @pl.when(pl.program_id(1) == 0)
    def _():
        if has_s0:
            sf_ref[...] = s0f_ref[...]
            sb_ref[...] = s0b_ref[...]
        else:
            sf_ref[...] = jnp.zeros(sf_ref.shape, F32)
            sb_ref[...] = jnp.zeros(sb_ref.shape, F32)

    for d, ((q_ref, k_ref, v_ref, m_ref), t_ref, o_ref, s_ref) in enumerate(
            ((fwd_in, tf_ref, of_ref, sf_ref), (bwd_in, tb_ref, ob_ref, sb_ref))):
        t = t_ref[...]
        row = lax.broadcasted_iota(jnp.int32, t.shape, 0)
        col = lax.broadcasted_iota(jnp.int32, t.shape, 1)
        chunk_lo = (row // GLA_CHUNK) * GLA_CHUNK
        if d == 0:
            tmask = (col <= row) & (col >= chunk_lo)
        else:
            tmask = (col >= row) & (col < chunk_lo + GLA_CHUNK)
        pre = _dot(m_ref[0].astype(BF16), wg_ref[d]) + bg_ref[d]
        g = _log_sigmoid(pre) * (1.0 / GLA_GATE_NORM)
        g_hi = g.astype(BF16)
        g_lo = (g - g_hi.astype(F32)).astype(BF16)
        cum = _dot(t, g_hi) + _dot(t, g_lo)
        for hd in range(GLA_HEADS):
            sl = slice(hd * LANES, (hd + 1) * LANES)
            cs = cum[:, sl]
            q = q_ref[0, :, sl].astype(F32)
            k = k_ref[0, :, sl].astype(F32)
            v = v_ref[0, :, sl]
            q_in = (q * jnp.exp(cs)).astype(BF16)
            k_in = (k * jnp.exp(-cs)).astype(BF16)
            a = jnp.where(tmask, _dot_nt(q_in, k_in), 0.0)
            o_intra = _dot(a.astype(BF16), v)
            s_t = s_ref[0, hd]
            order = range(nch) if d == 0 else range(nch - 1, -1, -1)
            for n in order:
                r = slice(n * GLA_CHUNK, (n + 1) * GLA_CHUNK)
                edge = (n + 1) * GLA_CHUNK - 1 if d == 0 else n * GLA_CHUNK
                tot = cs[edge:edge + 1, :]
                o_ref[0, r, sl] = o_intra[r] + _dot_nt(q_in[r], s_t.astype(BF16))
                k_dec = (k[r] * jnp.exp(tot - cs[r])).astype(BF16)
                s_t = s_t * jnp.exp(tot) + _dot_tn(v[r], k_dec)
            s_ref[0, hd] = s_t


def _gla_masks(tb_rows):
    idx = np.arange(tb_rows)
    same = (idx[:, None] // GLA_CHUNK) == (idx[None, :] // GLA_CHUNK)
    fwd = same & (idx[None, :] <= idx[:, None])
    bwd = same & (idx[None, :] >= idx[:, None])
    return jnp.asarray(fwd, dtype=BF16), jnp.asarray(bwd, dtype=BF16)


def _gla(gq, gk, gv, misc, w, s0, tb_rows):
    b, n, width = gq.shape
    nb = n // tb_rows
    fw = lambda cols: pl.BlockSpec((1, tb_rows, cols), lambda bi, i: (bi, i, 0))
    bw = lambda cols: pl.BlockSpec((1, tb_rows, cols), lambda bi, i: (bi, nb - 1 - i, 0))
    st = pl.BlockSpec((1, GLA_HEADS, GLA_DV, GLA_DK), lambda bi, i: (bi, 0, 0, 0))
    tf, tb = _gla_masks(tb_rows)
    in_specs = ([fw(width)] * 3 + [fw(LANES)] + [bw(width)] * 3 + [bw(LANES)]
                + [_full((2, LANES, width)), _full((2, 1, width)),
                   _full((tb_rows, tb_rows)), _full((tb_rows, tb_rows))])
    args = [gq, gk, gv, misc, gq, gk, gv, misc, w["wg"], w["bg"], tf, tb]
    if s0 is not None:
        in_specs += [st, st]
        args += list(s0)
    return pl.pallas_call(
        functools.partial(_gla_kernel, s0 is not None),
        grid=(b, nb),
        in_specs=in_specs,
        out_specs=[fw(width), bw(width), st, st],
        out_shape=[jax.ShapeDtypeStruct((b, n, width), F32)] * 2
        + [jax.ShapeDtypeStruct((b, GLA_HEADS, GLA_DV, GLA_DK), F32)] * 2,
        compiler_params=_params("arbitrary", "arbitrary"),
        name="gla",
    )(*args)


def _out_kernel(has_gla, gate_off, *refs):
    if has_gla:
        mix_ref, of_ref, ob_ref, gout_ref, ggo_ref, w_ref, x_ref, mod_ref, g_ref, o_ref = refs
        o = of_ref[...] + ob_ref[...]
        parts = []
        for hd in range(GLA_HEADS):
            sl = slice(hd * LANES, (hd + 1) * LANES)
            gate = gout_ref[:, sl].astype(F32)
            parts.append((_rms(o[:, sl], ggo_ref[...]) * _silu(gate)).astype(BF16))
        n_mla = mix_ref.shape[1]
        out = _dot(mix_ref[...], w_ref[0:n_mla, :]) + _dot(jnp.concatenate(parts, axis=-1), w_ref[n_mla:, :])
    else:
        mix_ref, w_ref, x_ref, mod_ref, g_ref, o_ref = refs
        out = _dot(mix_ref[...], w_ref[...])
    gate = mod_ref[0][:, gate_off:gate_off + D_MODEL]
    o_ref[...] = x_ref[...] + gate * _rms(out, g_ref[...])


def _out_proj(mix, gla, w_out, x, mod3, modrow, g, tm):
    t = x.shape[0]
    tok = lambda n: pl.BlockSpec((tm, n), lambda i: (i, 0))
    in_specs = [tok(mix.shape[1])]
    args = [mix]
    if gla is not None:
        o_f, o_b, gout, g_gla = gla
        in_specs += [tok(o_f.shape[1])] * 3 + [_full((1, GLA_DV))]
        args += [o_f, o_b, gout, g_gla]
    in_specs += [_full(w_out.shape), tok(D_MODEL),
                 pl.BlockSpec((1, 1, 6 * D_MODEL), lambda i: (modrow(i), 0, 0)), _full((1, D_MODEL))]
    args += [w_out, x, mod3, g]
    return pl.pallas_call(
        functools.partial(_out_kernel, gla is not None, 2 * D_MODEL),
        grid=(t // tm,),
        in_specs=in_specs,
        out_specs=tok(D_MODEL),
        out_shape=jax.ShapeDtypeStruct((t, D_MODEL), F32),
        compiler_params=_params("arbitrary"),
        name="out_proj",
    )(*args)


def _ffn_kernel(x_ref, mod_ref, g2_ref, g3_ref, wg_ref, wu_ref, wo_ref, o_ref, h_scr, acc_scr):
    j = pl.program_id(1)

    @pl.when(j == 0)
    def _():
        mod = mod_ref[0]
        shift, scale = mod[:, 3 * D_MODEL:4 * D_MODEL], mod[:, 4 * D_MODEL:5 * D_MODEL]
        h_scr[...] = (_rms(x_ref[...], g2_ref[...]) * (1.0 + scale) + shift).astype(BF16)
        acc_scr[...] = jnp.zeros(acc_scr.shape, F32)

    hb = h_scr[...]
    act = (_silu(_dot(hb, wg_ref[...])) * _dot(hb, wu_ref[...])).astype(BF16)
    acc_scr[...] += _dot(act, wo_ref[...])

    @pl.when(j == pl.num_programs(1) - 1)
    def _():
        gate = mod_ref[0][:, 5 * D_MODEL:6 * D_MODEL]
        o_ref[...] = x_ref[...] + gate * _rms(acc_scr[...], g3_ref[...])


def _ffn(x, mod3, modrow, g2, g3, w_in, w_out, tm):
    t = x.shape[0]
    th = FFN_HIDDEN // FFN_SPLIT
    tok = pl.BlockSpec((tm, D_MODEL), lambda i, j: (i, 0))
    vec = pl.BlockSpec((1, D_MODEL), lambda i, j: (0, 0))
    return pl.pallas_call(
        _ffn_kernel,
        grid=(t // tm, FFN_SPLIT),
        in_specs=[tok, pl.BlockSpec((1, 1, 6 * D_MODEL), lambda i, j: (modrow(i), 0, 0)), vec, vec,
                  pl.BlockSpec((D_MODEL, th), lambda i, j: (0, j)),
                  pl.BlockSpec((D_MODEL, th), lambda i, j: (0, j + FFN_SPLIT)),
                  pl.BlockSpec((th, D_MODEL), lambda i, j: (j, 0))],
        out_specs=tok,
        out_shape=jax.ShapeDtypeStruct((t, D_MODEL), F32),
        scratch_shapes=[pltpu.VMEM((tm, D_MODEL), BF16), pltpu.VMEM((tm, D_MODEL), F32)],
        compiler_params=_params("arbitrary", "arbitrary"),
        name="ffn",
    )(x, mod3, g2, g3, w_in, w_in, w_out)


def _c_in_kernel(use_rope, want_f32, *refs):
    x_ref, mod_ref, g0_ref, w_ref = refs[:4]
    refs = refs[4:]
    if use_rope:
        c_ref, se_ref, so_ref = refs[:3]
        refs = refs[3:]
    q_out, k_out, v_out = refs[:3]
    mod = mod_ref[0]
    shift, scale = mod[:, 0:D_MODEL], mod[:, D_MODEL:2 * D_MODEL]
    h = _rms(x_ref[...], g0_ref[...]) * (1.0 + scale) + shift
    proj = _dot(h.astype(BF16), w_ref[...])
    width = DIFF_HEADS * LANES
    sm_scale = DIFF_HEAD_DIM ** -0.5
    for hd in range(DIFF_HEADS):
        sl = slice(hd * LANES, (hd + 1) * LANES)
        qh = proj[:, hd * LANES:(hd + 1) * LANES]
        kh = proj[:, width + hd * LANES:width + (hd + 1) * LANES]
        if use_rope:
            qh = _rope(qh, c_ref[...], se_ref[...], so_ref[...])
            kh = _rope(kh, c_ref[...], se_ref[...], so_ref[...])
        q_out[:, sl] = (qh * sm_scale).astype(BF16)
        k_out[:, sl] = kh.astype(BF16)
        if want_f32:
            refs[3][:, sl] = kh
    v = proj[:, 2 * width:3 * width]
    v_out[...] = v.astype(BF16)
    if want_f32:
        refs[4][...] = v


def _c_in(x, mod3, modrow, g0, w_qkv, rope, want_f32, tm, seq_len):
    t = x.shape[0]
    width = DIFF_HEADS * LANES
    tok = lambda n: pl.BlockSpec((tm, n), lambda i: (i, 0))
    in_specs = [tok(D_MODEL), pl.BlockSpec((1, 1, 6 * D_MODEL), lambda i: (modrow(i), 0, 0)),
                _full((1, D_MODEL)), _full((D_MODEL, 3 * width))]
    args = [x, mod3, g0, w_qkv]
    if rope is not None:
        per = seq_len // tm
        in_specs += [pl.BlockSpec((tm, LANES), lambda i: (i % per, 0))] * 3
        args += list(rope)
    dts = [BF16] * 3 + ([F32] * 2 if want_f32 else [])
    return pl.pallas_call(
        functools.partial(_c_in_kernel, rope is not None, want_f32),
        grid=(t // tm,),
        in_specs=in_specs,
        out_specs=[tok(width)] * len(dts),
        out_shape=[jax.ShapeDtypeStruct((t, width), dt) for dt in dts],
        compiler_params=_params("arbitrary"),
        name="c_in",
    )(*args)


def _diff_attn_kernel(has_ctx, tk_new, lam_init, *refs):
    if has_ctx:
        q_ref, kn_ref, vn_ref, kc_ref, vc_ref, lam_ref, g_ref, o_ref = refs
    else:
        q_ref, kn_ref, vn_ref, lam_ref, g_ref, o_ref = refs
    tq = q_ref.shape[1]
    lane = lax.broadcasted_iota(jnp.int32, (tq, LANES), 1)
    lp = lam_ref[...]
    lam = (jnp.exp(jnp.sum(lp[0:1] * lp[1:2], axis=-1, keepdims=True))
           - jnp.exp(jnp.sum(lp[2:3] * lp[3:4], axis=-1, keepdims=True)) + lam_init)

    for hd in range(DIFF_HEADS):
        sl = slice(hd * LANES, (hd + 1) * LANES)
        qh = q_ref[0, :, sl]
        zero = jnp.zeros_like(qh)
        q0 = jnp.where(lane < DIFF_HEAD_DIM, qh, zero)
        q1 = jnp.where(lane < DIFF_HEAD_DIM, zero, qh)

        def make_body(kref, vref, tk):
            def body(c, carry):
                r0 = pl.multiple_of(c * tk, tk)
                k = kref[0, pl.ds(r0, tk), sl]
                v = vref[0, pl.ds(r0, tk), sl]
                new = []
                for qj, (m, l, acc) in zip((q0, q1), (carry[:3], carry[3:])):
                    s = _dot_nt(qj, k)
                    m_new = jnp.maximum(m, jnp.max(s, axis=-1, keepdims=True))
                    alpha = jnp.exp(m - m_new)
                    p = jnp.exp(s - m_new)
                    l = alpha * l + jnp.sum(p, axis=-1, keepdims=True)
                    acc = alpha * acc + _dot(p.astype(BF16), v)
                    new += [m_new, l, acc]
                return tuple(new)
            return body

        init = (jnp.full((tq, 1), _NEG, F32), jnp.zeros((tq, 1), F32), jnp.zeros((tq, LANES), F32))
        carry = init + init
        if has_ctx:
            n_ctx = kc_ref.shape[1]
            tk_ctx = min(ATTN_K_TILE, n_ctx)
            carry = lax.fori_loop(0, n_ctx // tk_ctx, make_body(kc_ref, vc_ref, tk_ctx), carry)
        carry = lax.fori_loop(0, kn_ref.shape[1] // tk_new, make_body(kn_ref, vn_ref, tk_new), carry)
        _, l0, acc0, _, l1, acc1 = carry
        o = acc0 / l0 - lam * (acc1 / l1)
        o_ref[0, :, sl] = (_rms(o, g_ref[...]) * (1.0 - lam_init)).astype(BF16)


def _diff_attn(q, k_new, v_new, ctx, lam_p, g_out, lam_init, tq):
    b, n, width = q.shape
    tk_new = min(ATTN_K_TILE, n)
    qspec = pl.BlockSpec((1, tq, width), lambda bi, qi: (bi, qi, 0))
    kv = lambda rows: pl.BlockSpec((1, rows, width), lambda bi, qi: (bi, 0, 0))
    in_specs = [qspec, kv(n), kv(n)]
    args = [q, k_new, v_new]
    if ctx is not None:
        in_specs += [kv(ctx[0].shape[1])] * 2
        args += list(ctx)
    in_specs += [_full(lam_p.shape), _full(g_out.shape)]
    args += [lam_p, g_out]
    return pl.pallas_call(
        functools.partial(_diff_attn_kernel, ctx is not None, tk_new, lam_init),
        grid=(b, n // tq),
        in_specs=in_specs,
        out_specs=qspec,
        out_shape=jax.ShapeDtypeStruct((b, n, width), BF16),
        compiler_params=_params("arbitrary", "arbitrary"),
        name="diff_attn",
    )(*args)


def _prep_ab_weights(ai, ab_w_in, mla_g_q, mla_g_kv, mla_w_uq, mla_w_ukv, gla_w_gate_up, gla_b_gate):
    w = ab_w_in[ai]
    d = w.shape[0]
    o = np.cumsum([0, MLA_Q_LORA, MLA_KV_LORA, MLA_ROPE, GLA_HEADS * GLA_DK, GLA_HEADS * GLA_DK,
                   GLA_HEADS * GLA_DV, 2 * GLA_GATE_RANK, GLA_HEADS * GLA_DV])
    q_lat, kv_lat, k_rope, gq, gk, gv, ggate, gout = (w[:, o[i]:o[i + 1]] for i in range(8))
    pad = jnp.zeros((d, LANES - MLA_ROPE - 2 * GLA_GATE_RANK), w.dtype)
    win = jnp.concatenate([q_lat, kv_lat, gq, gk, gv, gout, k_rope, ggate, pad], axis=1).astype(BF16)

    head_pad = LANES - MLA_NOPE - MLA_ROPE
    wq = mla_w_uq[ai].reshape(MLA_Q_LORA, MLA_HEADS, MLA_NOPE + MLA_ROPE)
    wq = jnp.pad(wq, ((0, 0), (0, 0), (0, head_pad))).reshape(MLA_Q_LORA, MLA_HEADS * LANES).astype(BF16)
    wkv = mla_w_ukv[ai].reshape(MLA_KV_LORA, MLA_HEADS, MLA_NOPE + MLA_V)
    wk = jnp.pad(wkv[:, :, :MLA_NOPE], ((0, 0), (0, 0), (0, LANES - MLA_NOPE)))
    wk = wk.reshape(MLA_KV_LORA, MLA_HEADS * LANES).astype(BF16)
    wv = wkv[:, :, MLA_NOPE:].reshape(MLA_KV_LORA, MLA_HEADS * MLA_V).astype(BF16)

    e = np.zeros((LANES, MLA_HEADS * LANES), np.float32)
    for hd in range(MLA_HEADS):
        e[np.arange(MLA_ROPE), hd * LANES + MLA_NOPE + np.arange(MLA_ROPE)] = 1.0

    wg = jnp.zeros((2, LANES, GLA_HEADS * GLA_DK), F32)
    for dr in range(2):
        lo = _MISC_GATE + dr * GLA_GATE_RANK
        wg = wg.at[dr, lo:lo + GLA_GATE_RANK, :].set(gla_w_gate_up[ai, dr])
    return dict(win=win, g_q=mla_g_q[ai][None, :], g_kv=mla_g_kv[ai][None, :], wq=wq, wk=wk, wv=wv,
                e=jnp.asarray(e, dtype=BF16), wg=wg.astype(BF16),
                bg=gla_b_gate[ai].reshape(2, 1, GLA_HEADS * GLA_DK))


def _diff_lambda_init(layer):
    return 0.8 - 0.6 * math.exp(-0.3 * layer)


def kernel(x_prompt, x_sample, cache_mla_ckv, cache_mla_krope, state_gla_fwd, state_gla_bwd,
           cache_diff_k, cache_diff_v, c, c_ctx, w_mod, b_mod, g_norm, ab_w_in, mla_g_q, mla_g_kv,
           mla_w_uq, mla_w_ukv, gla_w_gate_up, gla_b_gate, gla_g_out, ab_w_out, c_w_qkv, diff_lambda,
           diff_g_out, c_w_out, w_ffn_in, w_ffn_out):
    depth = w_mod.shape[0]
    bp, lp, d = x_prompt.shape
    bs, ls, _ = x_sample.shape
    n_ctx = cache_mla_ckv.shape[2]

    n_cond = 1 + bs
    rows = -(-n_cond // 8) * 8
    conds = jnp.concatenate([c_ctx[None, :], c, jnp.zeros((rows - n_cond, d), F32)], axis=0)
    mod3 = _modulation(conds, w_mod, b_mod).reshape(depth * rows, 1, 6 * d)

    tm = TOKEN_TILE
    per_s = ls // tm
    rope_q = _rope_tables(ls, MLA_ROPE, MLA_NOPE, LANES)
    rope_misc = _rope_tables(ls, MLA_ROPE, 0, LANES)
    rope_diff = _rope_tables(ls, DIFF_HEAD_DIM, 0, DIFF_HEAD_DIM)

    groups = {
        "prompt": dict(x=x_prompt.reshape(bp * lp, d), b=bp, n=lp, rope=False,
                       modrow=lambda li: (lambda i: li * rows)),
        "sample": dict(x=x_sample.reshape(bs * ls, d), b=bs, n=ls, rope=True,
                       modrow=lambda li: (lambda i: li * rows + 1 + i // per_s)),
    }
    ab_states, c_states = [], []

    ab_w = [_prep_ab_weights(ai, ab_w_in, mla_g_q, mla_g_kv, mla_w_uq, mla_w_ukv, gla_w_gate_up, gla_b_gate)
            for ai in range(ab_w_in.shape[0])]
    ab_w_out_b = ab_w_out.astype(BF16)
    c_w_qkv_b = c_w_qkv.astype(BF16)
    c_w_out_b = c_w_out.astype(BF16)
    w_ffn_in_b = w_ffn_in.astype(BF16)
    w_ffn_out_b = w_ffn_out.astype(BF16)

    for name, grp in groups.items():
        x, b, n = grp["x"], grp["b"], grp["n"]
        is_sample = grp["rope"]
        for li in range(depth):
            modrow = grp["modrow"](li)
            g = g_norm[li]
            if li % 2 == 0:
                ai = li // 2
                w = ab_w[ai]
                rope = rope_q + rope_misc if is_sample else None
                q, kcat, v, ckv, misc, gq, gk, gv, gout = _ab_in(x, mod3, modrow, g[0:1], w, rope, tm, n)
                sh = lambda a: a.reshape(b, n, a.shape[-1])
                if is_sample:
                    kc, vc = _ctx_kv(cache_mla_ckv[:, ai].reshape(bs * n_ctx, MLA_KV_LORA),
                                     cache_mla_krope[:, ai].reshape(bs * n_ctx, MLA_ROPE), w, tm)
                    ctx = (kc.reshape(bs, n_ctx, -1), vc.reshape(bs, n_ctx, -1))
                    s0 = (jnp.swapaxes(state_gla_fwd[:, ai], -1, -2), jnp.swapaxes(state_gla_bwd[:, ai], -1, -2))
                else:
                    ctx, s0 = None, None
                mla_o = _mla_attn(sh(q), sh(kcat), sh(v), ctx, min(ATTN_Q_TILE, n))
                o_f, o_b, s_f, s_b = _gla(sh(gq), sh(gk), sh(gv), sh(misc), w, s0, min(GLA_BLOCK, n))
                if not is_sample:
                    ab_states.append((ckv.reshape(b, n, MLA_KV_LORA), misc[:, :MLA_ROPE].reshape(b, n, MLA_ROPE),
                                      jnp.swapaxes(s_f, -1, -2), jnp.swapaxes(s_b, -1, -2)))
                x = _out_proj(mla_o.reshape(b * n, -1),
                              (o_f.reshape(b * n, -1), o_b.reshape(b * n, -1), gout, gla_g_out[ai][None, :]),
                              ab_w_out_b[ai], x, mod3, modrow, g[1:2], tm)
            else:
                ci = li // 2
                rope = rope_diff if is_sample else None
                outs = _c_in(x, mod3, modrow, g[0:1], c_w_qkv_b[ci], rope, not is_sample, tm, n)
                q, k, v = (a.reshape(b, n, -1) for a in outs[:3])
                if is_sample:
                    ctx = (cache_diff_k[:, ci].reshape(bs, n_ctx, -1).astype(BF16),
                           cache_diff_v[:, ci].reshape(bs, n_ctx, -1).astype(BF16))
                else:
                    ctx = None
                    c_states.append((outs[3].reshape(b, n, DIFF_HEADS, 2, DIFF_HEAD_DIM),
                                     outs[4].reshape(b, n, DIFF_HEADS, 2 * DIFF_HEAD_DIM)))
                o = _diff_attn(q, k, v, ctx, diff_lambda[ci], diff_g_out[ci][None, :],
                               _diff_lambda_init(li), min(ATTN_Q_TILE, n))
                x = _out_proj(o.reshape(b * n, -1), None, c_w_out_b[ci], x, mod3, modrow, g[1:2], tm)
            x = _ffn(x, mod3, modrow, g[2:3], g[3:4], w_ffn_in_b[li], w_ffn_out_b[li], tm)
        grp["y"] = x.reshape(b, n, d)

    stack = lambda states, i: jnp.stack([s[i] for s in states], axis=1)
    return (groups["prompt"]["y"], groups["sample"]["y"],
            stack(ab_states, 0), stack(ab_states, 1), stack(ab_states, 2), stack(ab_states, 3),
            stack(c_states, 0), stack(c_states, 1))
```

```python
import functools
import math

import numpy as np
import jax
import jax.numpy as jnp
from jax import lax
from jax.experimental import pallas as pl
from jax.experimental.pallas import tpu as pltpu

F32 = jnp.float32
BF16 = jnp.bfloat16

D_MODEL = 1024
GRID_W = 64
ROPE_BASE = 10000.0
NORM_EPS = 1e-6

MLA_HEADS = 8
MLA_NOPE = 64
MLA_ROPE = 32
MLA_V = 64
MLA_Q_LORA = 384
MLA_KV_LORA = 256
GLA_HEADS = 4
GLA_DK = 128
GLA_DV = 128
GLA_GATE_RANK = 16
GLA_GATE_NORM = 16.0
GLA_CHUNK = 64
DIFF_HEADS = 8
DIFF_HEAD_DIM = 64
FFN_HIDDEN = 2816

LANES = 128
VMEM_LIMIT = 56 * 1024 * 1024

_OFF_QLAT = 0
_OFF_KVLAT = _OFF_QLAT + MLA_Q_LORA
_OFF_GQ = _OFF_KVLAT + MLA_KV_LORA
_OFF_GK = _OFF_GQ + GLA_HEADS * GLA_DK
_OFF_GV = _OFF_GK + GLA_HEADS * GLA_DK
_OFF_GOUT = _OFF_GV + GLA_HEADS * GLA_DV
_OFF_MISC = _OFF_GOUT + GLA_HEADS * GLA_DV
_AB_COLS = _OFF_MISC + LANES
_MISC_GATE = MLA_ROPE

LOG2E = math.log2(math.e)

TOKEN_TILE = 512
ATTN_Q_TILE = 512
ATTN_K_TILE = 512
GLA_BLOCK = 256
FFN_SPLIT = 2


def _rms(x, g):
    var = jnp.mean(x * x, axis=-1, keepdims=True)
    return x * lax.rsqrt(var + NORM_EPS) * g


def _silu(x):
    return x * (1.0 / (1.0 + jnp.exp(-x)))


def _log_sigmoid(x):
    return -(jnp.maximum(-x, 0.0) + jnp.log1p(jnp.exp(-jnp.abs(x))))


def _rope(x, c, se, so):
    n = x.shape[-1]
    return x * c + pltpu.roll(x, n - 1, 1) * se + pltpu.roll(x, 1, 1) * so


def _dot(a, b):
    return jnp.dot(a, b, preferred_element_type=F32)


def _dot_nt(a, b):
    return lax.dot_general(a, b, (((1,), (1,)), ((), ())), preferred_element_type=F32)


def _dot_tn(a, b):
    return lax.dot_general(a, b, (((0,), (0,)), ((), ())), preferred_element_type=F32)


def _params(*sem):
    return pltpu.CompilerParams(dimension_semantics=sem, vmem_limit_bytes=VMEM_LIMIT)


def _full(shape):
    zeros = (0,) * len(shape)
    return pl.BlockSpec(shape, lambda *_: zeros)


def _rope_tables(n_tokens, rot_dim, lane_lo, period):
    rows = n_tokens // GRID_W
    row = np.repeat(np.arange(rows, dtype=np.float64), GRID_W)
    col = np.tile(np.arange(GRID_W, dtype=np.float64), rows)
    n_freq = rot_dim // 4
    inv = ROPE_BASE ** (-np.arange(n_freq, dtype=np.float64) / n_freq)
    ang = np.concatenate([row[:, None] * inv, col[:, None] * inv], axis=-1)
    cos, sin = np.cos(ang), np.sin(ang)
    c = np.ones((n_tokens, period))
    se = np.zeros((n_tokens, period))
    so = np.zeros((n_tokens, period))
    c[:, lane_lo:lane_lo + rot_dim] = np.repeat(cos, 2, axis=-1)
    se[:, lane_lo:lane_lo + rot_dim:2] = -sin
    so[:, lane_lo + 1:lane_lo + rot_dim:2] = sin
    reps = LANES // period
    return tuple(jnp.asarray(np.tile(t, (1, reps)), dtype=F32) for t in (c, se, so))


def _mod_kernel(c_ref, w_ref, b_ref, o_ref):
    a = _silu(c_ref[...]).astype(BF16)
    o_ref[0] = _dot(a, w_ref[0].astype(BF16)) + b_ref[0]


def _modulation(conds, w_mod, b_mod):
    depth, d, n = w_mod.shape
    rows = conds.shape[0]
    tn = 1536
    return pl.pallas_call(
        _mod_kernel,
        grid=(depth, n // tn),
        in_specs=[_full((rows, d)),
                  pl.BlockSpec((1, d, tn), lambda l, j: (l, 0, j)),
                  pl.BlockSpec((1, 1, tn), lambda l, j: (l, 0, j))],
        out_specs=pl.BlockSpec((1, rows, tn), lambda l, j: (l, 0, j)),
        out_shape=jax.ShapeDtypeStruct((depth, rows, n), F32),
        compiler_params=_params("arbitrary", "arbitrary"),
        name="modulation",
    )(conds, w_mod, b_mod.reshape(depth, 1, n))


def _ab_in_kernel(use_rope, *refs):
    (x_ref, mod_ref, g0_ref, win_ref, gq_ref, gkv_ref, wq_ref, wk_ref, wv_ref, e_ref) = refs[:10]
    refs = refs[10:]
    if use_rope:
        cq, seq, soq, cm, sem, som = refs[:6]
        refs = refs[6:]
    (q_out, kcat_out, v_out, ckv_out, misc_out, gq_out, gk_out, gv_out, gout_out) = refs

    mod = mod_ref[0]
    shift, scale = mod[:, 0:D_MODEL], mod[:, D_MODEL:2 * D_MODEL]
    h = _rms(x_ref[...], g0_ref[...]) * (1.0 + scale) + shift
    proj = _dot(h.astype(BF16), win_ref[...])

    gq_out[...] = (proj[:, _OFF_GQ:_OFF_GK] * (GLA_DK ** -0.5)).astype(BF16)
    gk_out[...] = proj[:, _OFF_GK:_OFF_GV].astype(BF16)
    gv_out[...] = proj[:, _OFF_GV:_OFF_GOUT].astype(BF16)
    gout_out[...] = proj[:, _OFF_GOUT:_OFF_MISC].astype(BF16)

    misc = proj[:, _OFF_MISC:_AB_COLS]
    if use_rope:
        misc = _rope(misc, cm[...], sem[...], som[...])
    misc_out[...] = misc

    ckv = _rms(proj[:, _OFF_KVLAT:_OFF_GQ], gkv_ref[...])
    ckv_out[...] = ckv
    ckv_b = ckv.astype(BF16)
    kcat = (_dot(ckv_b, wk_ref[...]) + _dot(misc.astype(BF16), e_ref[...])).astype(BF16)
    v = _dot(ckv_b, wv_ref[...]).astype(BF16)
    for hp in range(MLA_HEADS // 2):
        v_out[0, hp] = v[:, hp * LANES:(hp + 1) * LANES]

    qn = _rms(proj[:, _OFF_QLAT:_OFF_KVLAT], gq_ref[...]).astype(BF16)
    q = _dot(qn, wq_ref[...])
    sm_scale = (MLA_NOPE + MLA_ROPE) ** -0.5 * LOG2E
    for hd in range(MLA_HEADS):
        sl = slice(hd * LANES, (hd + 1) * LANES)
        kcat_out[0, hd] = kcat[:, sl]
        qh = q[:, sl]
        if use_rope:
            qh = _rope(qh, cq[...], seq[...], soq[...])
        q_out[0, hd] = (qh * sm_scale).astype(BF16)


def _ab_in(x, mod3, modrow, g0, w, rope, tm, seq_len):
    t = x.shape[0]
    nt = t // tm
    tok = lambda n: pl.BlockSpec((tm, n), lambda i: (i, 0))
    in_specs = [tok(D_MODEL),
                pl.BlockSpec((1, 1, 6 * D_MODEL), lambda i: (modrow(i), 0, 0)),
                _full((1, D_MODEL)), _full((D_MODEL, _AB_COLS)),
                _full((1, MLA_Q_LORA)), _full((1, MLA_KV_LORA)),
                _full((MLA_Q_LORA, MLA_HEADS * LANES)), _full((MLA_KV_LORA, MLA_HEADS * LANES)),
                _full((MLA_KV_LORA, MLA_HEADS * MLA_V)), _full((LANES, MLA_HEADS * LANES))]
    args = [x, mod3, g0, w["win"], w["g_q"], w["g_kv"], w["wq"], w["wk"], w["wv"], w["e"]]
    per = seq_len // tm
    if rope is not None:
        in_specs += [pl.BlockSpec((tm, LANES), lambda i: (i % per, 0))] * 6
        args += list(rope)
    nb = t // seq_len
    heads = lambda nh: pl.BlockSpec((1, nh, tm, LANES), lambda i: (i // per, 0, i % per, 0))
    head_shape = lambda nh: jax.ShapeDtypeStruct((nb, nh, seq_len, LANES), BF16)
    widths = [(MLA_KV_LORA, F32), (LANES, F32)] + [(GLA_HEADS * GLA_DK, BF16)] * 4
    return pl.pallas_call(
        functools.partial(_ab_in_kernel, rope is not None),
        grid=(nt,),
        in_specs=in_specs,
        out_specs=[heads(MLA_HEADS), heads(MLA_HEADS), heads(MLA_HEADS // 2)] + [tok(n) for n, _ in widths],
        out_shape=[head_shape(MLA_HEADS), head_shape(MLA_HEADS), head_shape(MLA_HEADS // 2)]
        + [jax.ShapeDtypeStruct((t, n), dt) for n, dt in widths],
        compiler_params=_params("arbitrary"),
        name="ab_in",
    )(*args)


def _ctx_kv_kernel(ckv_ref, kr_ref, wk_ref, wv_ref, e_ref, kcat_out, v_out):
    c = ckv_ref[...].astype(BF16)
    kr = kr_ref[...].astype(BF16)
    kcat = (_dot(c, wk_ref[...]) + _dot(kr, e_ref[...])).astype(BF16)
    v = _dot(c, wv_ref[...]).astype(BF16)
    for hd in range(MLA_HEADS):
        kcat_out[0, hd] = kcat[:, hd * LANES:(hd + 1) * LANES]
    for hp in range(MLA_HEADS // 2):
        v_out[0, hp] = v[:, hp * LANES:(hp + 1) * LANES]


def _ctx_kv(ckv, krope, w):
    b, n_ctx, _ = ckv.shape
    tok = lambda n: pl.BlockSpec((None, n_ctx, n), lambda i: (i, 0, 0))
    heads = lambda nh: pl.BlockSpec((1, nh, n_ctx, LANES), lambda i: (i, 0, 0, 0))
    return pl.pallas_call(
        _ctx_kv_kernel,
        grid=(b,),
        in_specs=[tok(MLA_KV_LORA), tok(MLA_ROPE),
                  _full((MLA_KV_LORA, MLA_HEADS * LANES)), _full((MLA_KV_LORA, MLA_HEADS * MLA_V)),
                  _full((MLA_ROPE, MLA_HEADS * LANES))],
        out_specs=[heads(MLA_HEADS), heads(MLA_HEADS // 2)],
        out_shape=[jax.ShapeDtypeStruct((b, MLA_HEADS, n_ctx, LANES), BF16),
                   jax.ShapeDtypeStruct((b, MLA_HEADS // 2, n_ctx, LANES), BF16)],
        compiler_params=_params("arbitrary"),
        name="mla_ctx_kv",
    )(ckv, krope, w["wk"], w["wv"], w["e"][:MLA_ROPE])


_NEG = -1e30


def _two_map_flash(q0, q1, sources):
    state = [None, None]
    for k0_ref, k1_ref, v_ref in sources:
        rows = v_ref.shape[0]
        tk = min(ATTN_K_TILE, rows)
        for c in range(rows // tk):
            r = slice(c * tk, (c + 1) * tk)
            v = v_ref[r, :]
            for j, (q, k_ref) in enumerate(((q0, k0_ref), (q1, k1_ref))):
                s = _dot_nt(q, k_ref[r, :])
                m_blk = jnp.max(s, axis=-1, keepdims=True)
                if state[j] is None:
                    p = jnp.exp2(s - m_blk)
                    state[j] = (m_blk, jnp.sum(p, axis=-1, keepdims=True), _dot(p.astype(BF16), v))
                else:
                    m, l, acc = state[j]
                    m_new = jnp.maximum(m, m_blk)
                    alpha = jnp.exp2(m - m_new)
                    p = jnp.exp2(s - m_new)
                    state[j] = (m_new, alpha * l + jnp.sum(p, axis=-1, keepdims=True),
                                alpha * acc + _dot(p.astype(BF16), v))
    return tuple((l, acc) for _, l, acc in state)


def _mla_attn_kernel(has_ctx, *refs):
    if has_ctx:
        q0_ref, q1_ref, k0n, k1n, vn, k0c, k1c, vc, o_ref = refs
        sources = [(k0c, k1c, vc), (k0n, k1n, vn)]
    else:
        q0_ref, q1_ref, k0n, k1n, vn, o_ref = refs
        sources = [(k0n, k1n, vn)]
    (l0, acc0), (l1, acc1) = _two_map_flash(q0_ref[...], q1_ref[...], sources)
    lane = lax.broadcasted_iota(jnp.int32, acc0.shape, 1)
    o_ref[...] = jnp.where(lane < MLA_V, acc0 * (1.0 / l0), acc1 * (1.0 / l1)).astype(BF16)


def _mla_attn(q, k_new, v_new, ctx, tq):
    b, nh, n, _ = q.shape
    slab = lambda rows, f: pl.BlockSpec((None, None, rows, LANES), f)
    even = lambda bi, p, qi: (bi, 2 * p, 0, 0)
    odd = lambda bi, p, qi: (bi, 2 * p + 1, 0, 0)
    pair = lambda bi, p, qi: (bi, p, 0, 0)
    in_specs = [slab(tq, lambda bi, p, qi: (bi, 2 * p, qi, 0)), slab(tq, lambda bi, p, qi: (bi, 2 * p + 1, qi, 0)),
                slab(n, even), slab(n, odd), slab(n, pair)]
    args = [q, q, k_new, k_new, v_new]
    if ctx is not None:
        nc = ctx[0].shape[2]
        in_specs += [slab(nc, even), slab(nc, odd), slab(nc, pair)]
        args += [ctx[0], ctx[0], ctx[1]]
    return pl.pallas_call(
        functools.partial(_mla_attn_kernel, ctx is not None),
        grid=(b, nh // 2, n // tq),
        in_specs=in_specs,
        out_specs=pl.BlockSpec((None, tq, LANES), lambda bi, p, qi: (bi, qi, p)),
        out_shape=jax.ShapeDtypeStruct((b, n, (nh // 2) * LANES), BF16),
        compiler_params=_params("arbitrary", "arbitrary", "arbitrary"),
        name="mla_attn",
    )(*args)


def _gla_kernel(has_s0, *refs):
    fwd_in, bwd_in = refs[0:4], refs[4:8]
    wg_ref, bg_ref, tf_ref, tb_ref = refs[8:12]
    refs = refs[12:]
    if has_s0:
        s0f_ref, s0b_ref = refs[:2]
        refs = refs[2:]
    of_ref, ob_ref, sf_ref, sb_ref = refs
    tb_rows = of_ref.shape[1]
    nch = tb_rows // GLA_CHUNK

    @pl.when(pl.program_id(1) == 0)
    def _():
        if has_s0:
            sf_ref[...] = s0f_ref[...]
            sb_ref[...] = s0b_ref[...]
        else:
            sf_ref[...] = jnp.zeros(sf_ref.shape, F32)
            sb_ref[...] = jnp.zeros(sb_ref.shape, F32)

    for d, ((q_ref, k_ref, v_ref, m_ref), t_ref, o_ref, s_ref) in enumerate(
            ((fwd_in, tf_ref, of_ref, sf_ref), (bwd_in, tb_ref, ob_ref, sb_ref))):
        t = t_ref[...]
        row = lax.broadcasted_iota(jnp.int32, t.shape, 0)
        col = lax.broadcasted_iota(jnp.int32, t.shape, 1)
        chunk_lo = (row // GLA_CHUNK) * GLA_CHUNK
        if d == 0:
            tmask = (col <= row) & (col >= chunk_lo)
        else:
            tmask = (col >= row) & (col < chunk_lo + GLA_CHUNK)
        pre = _dot(m_ref[0].astype(BF16), wg_ref[d]) + bg_ref[d]
        g = _log_sigmoid(pre) * (1.0 / GLA_GATE_NORM)
        g_hi = g.astype(BF16)
        g_lo = (g - g_hi.astype(F32)).astype(BF16)
        cum = _dot(t, g_hi) + _dot(t, g_lo)
        for hd in range(GLA_HEADS):
            sl = slice(hd * LANES, (hd + 1) * LANES)
            cs = cum[:, sl]
            q = q_ref[0, :, sl].astype(F32)
            k = k_ref[0, :, sl].astype(F32)
            v = v_ref[0, :, sl]
            q_in = (q * jnp.exp(cs)).astype(BF16)
            k_in = (k * jnp.exp(-cs)).astype(BF16)
            a = jnp.where(tmask, _dot_nt(q_in, k_in), 0.0)
            o_intra = _dot(a.astype(BF16), v)
            s_t = s_ref[0, hd]
            order = range(nch) if d == 0 else range(nch - 1, -1, -1)
            for n in order:
                r = slice(n * GLA_CHUNK, (n + 1) * GLA_CHUNK)
                edge = (n + 1) * GLA_CHUNK - 1 if d == 0 else n * GLA_CHUNK
                tot = cs[edge:edge + 1, :]
                o_ref[0, r, sl] = o_intra[r] + _dot_nt(q_in[r], s_t.astype(BF16))
                k_dec = (k[r] * jnp.exp(tot - cs[r])).astype(BF16)
                s_t = s_t * jnp.exp(tot) + _dot_tn(v[r], k_dec)
            s_ref[0, hd] = s_t


def _gla_masks(tb_rows):
    idx = np.arange(tb_rows)
    same = (idx[:, None] // GLA_CHUNK) == (idx[None, :] // GLA_CHUNK)
    fwd = same & (idx[None, :] <= idx[:, None])
    bwd = same & (idx[None, :] >= idx[:, None])
    return jnp.asarray(fwd, dtype=BF16), jnp.asarray(bwd, dtype=BF16)


def _gla(gq, gk, gv, misc, w, s0, tb_rows):
    b, n, width = gq.shape
    nb = n // tb_rows
    fw = lambda cols: pl.BlockSpec((1, tb_rows, cols), lambda bi, i: (bi, i, 0))
    bw = lambda cols: pl.BlockSpec((1, tb_rows, cols), lambda bi, i: (bi, nb - 1 - i, 0))
    st = pl.BlockSpec((1, GLA_HEADS, GLA_DV, GLA_DK), lambda bi, i: (bi, 0, 0, 0))
    tf, tb = _gla_masks(tb_rows)
    in_specs = ([fw(width)] * 3 + [fw(LANES)] + [bw(width)] * 3 + [bw(LANES)]
                + [_full((2, LANES, width)), _full((2, 1, width)),
                   _full((tb_rows, tb_rows)), _full((tb_rows, tb_rows))])
    args = [gq, gk, gv, misc, gq, gk, gv, misc, w["wg"], w["bg"], tf, tb]
    if s0 is not None:
        in_specs += [st, st]
        args += list(s0)
    return pl.pallas_call(
        functools.partial(_gla_kernel, s0 is not None),
        grid=(b, nb),
        in_specs=in_specs,
        out_specs=[fw(width), bw(width), st, st],
        out_shape=[jax.ShapeDtypeStruct((b, n, width), F32)] * 2
        + [jax.ShapeDtypeStruct((b, GLA_HEADS, GLA_DV, GLA_DK), F32)] * 2,
        compiler_params=_params("arbitrary", "arbitrary"),
        name="gla",
    )(*args)


def _out_kernel(has_gla, gate_off, *refs):
    if has_gla:
        mix_ref, of_ref, ob_ref, gout_ref, ggo_ref, w_ref, x_ref, mod_ref, g_ref, o_ref = refs
        o = of_ref[...] + ob_ref[...]
        parts = []
        for hd in range(GLA_HEADS):
            sl = slice(hd * LANES, (hd + 1) * LANES)
            gate = gout_ref[:, sl].astype(F32)
            parts.append((_rms(o[:, sl], ggo_ref[...]) * _silu(gate)).astype(BF16))
        n_mla = mix_ref.shape[1]
        out = _dot(mix_ref[...], w_ref[0:n_mla, :]) + _dot(jnp.concatenate(parts, axis=-1), w_ref[n_mla:, :])
    else:
        mix_ref, w_ref, x_ref, mod_ref, g_ref, o_ref = refs
        out = _dot(mix_ref[...], w_ref[...])
    gate = mod_ref[0][:, gate_off:gate_off + D_MODEL]
    o_ref[...] = x_ref[...] + gate * _rms(out, g_ref[...])


def _out_proj(mix, gla, w_out, x, mod3, modrow, g, tm):
    t = x.shape[0]
    tok = lambda n: pl.BlockSpec((tm, n), lambda i: (i, 0))
    in_specs = [tok(mix.shape[1])]
    args = [mix]
    if gla is not None:
        o_f, o_b, gout, g_gla = gla
        in_specs += [tok(o_f.shape[1])] * 3 + [_full((1, GLA_DV))]
        args += [o_f, o_b, gout, g_gla]
    in_specs += [_full(w_out.shape), tok(D_MODEL),
                 pl.BlockSpec((1, 1, 6 * D_MODEL), lambda i: (modrow(i), 0, 0)), _full((1, D_MODEL))]
    args += [w_out, x, mod3, g]
    return pl.pallas_call(
        functools.partial(_out_kernel, gla is not None, 2 * D_MODEL),
        grid=(t // tm,),
        in_specs=in_specs,
        out_specs=tok(D_MODEL),
        out_shape=jax.ShapeDtypeStruct((t, D_MODEL), F32),
        compiler_params=_params("arbitrary"),
        name="out_proj",
    )(*args)


def _ffn_kernel(x_ref, mod_ref, g2_ref, g3_ref, wg_ref, wu_ref, wo_ref, o_ref, h_scr, acc_scr):
    j = pl.program_id(1)

    @pl.when(j == 0)
    def _():
        mod = mod_ref[0]
        shift, scale = mod[:, 3 * D_MODEL:4 * D_MODEL], mod[:, 4 * D_MODEL:5 * D_MODEL]
        h_scr[...] = (_rms(x_ref[...], g2_ref[...]) * (1.0 + scale) + shift).astype(BF16)
        acc_scr[...] = jnp.zeros(acc_scr.shape, F32)

    hb = h_scr[...]
    act = (_silu(_dot(hb, wg_ref[...])) * _dot(hb, wu_ref[...])).astype(BF16)
    acc_scr[...] += _dot(act, wo_ref[...])

    @pl.when(j == pl.num_programs(1) - 1)
    def _():
        gate = mod_ref[0][:, 5 * D_MODEL:6 * D_MODEL]
        o_ref[...] = x_ref[...] + gate * _rms(acc_scr[...], g3_ref[...])


def _ffn(x, mod3, modrow, g2, g3, w_in, w_out, tm):
    t = x.shape[0]
    th = FFN_HIDDEN // FFN_SPLIT
    tok = pl.BlockSpec((tm, D_MODEL), lambda i, j: (i, 0))
    vec = pl.BlockSpec((1, D_MODEL), lambda i, j: (0, 0))
    return pl.pallas_call(
        _ffn_kernel,
        grid=(t // tm, FFN_SPLIT),
        in_specs=[tok, pl.BlockSpec((1, 1, 6 * D_MODEL), lambda i, j: (modrow(i), 0, 0)), vec, vec,
                  pl.BlockSpec((D_MODEL, th), lambda i, j: (0, j)),
                  pl.BlockSpec((D_MODEL, th), lambda i, j: (0, j + FFN_SPLIT)),
                  pl.BlockSpec((th, D_MODEL), lambda i, j: (j, 0))],
        out_specs=tok,
        out_shape=jax.ShapeDtypeStruct((t, D_MODEL), F32),
        scratch_shapes=[pltpu.VMEM((tm, D_MODEL), BF16), pltpu.VMEM((tm, D_MODEL), F32)],
        compiler_params=_params("arbitrary", "arbitrary"),
        name="ffn",
    )(x, mod3, g2, g3, w_in, w_in, w_out)


def _c_in_kernel(use_rope, want_f32, *refs):
    x_ref, mod_ref, g0_ref, w_ref = refs[:4]
    refs = refs[4:]
    if use_rope:
        c_ref, se_ref, so_ref = refs[:3]
        refs = refs[3:]
    q_out, k_out, v_out = refs[:3]
    mod = mod_ref[0]
    shift, scale = mod[:, 0:D_MODEL], mod[:, D_MODEL:2 * D_MODEL]
    h = _rms(x_ref[...], g0_ref[...]) * (1.0 + scale) + shift
    proj = _dot(h.astype(BF16), w_ref[...])
    width = DIFF_HEADS * LANES
    sm_scale = DIFF_HEAD_DIM ** -0.5 * LOG2E
    for hd in range(DIFF_HEADS):
        sl = slice(hd * LANES, (hd + 1) * LANES)
        qh = proj[:, hd * LANES:(hd + 1) * LANES]
        kh = proj[:, width + hd * LANES:width + (hd + 1) * LANES]
        vh = proj[:, 2 * width + hd * LANES:2 * width + (hd + 1) * LANES]
        if use_rope:
            qh = _rope(qh, c_ref[...], se_ref[...], so_ref[...])
            kh = _rope(kh, c_ref[...], se_ref[...], so_ref[...])
        q_out[0, hd] = (qh * sm_scale).astype(BF16)
        k_out[0, hd] = kh.astype(BF16)
        v_out[0, hd] = vh.astype(BF16)
        if want_f32:
            refs[3][:, sl] = kh
            refs[4][:, sl] = vh


def _c_in(x, mod3, modrow, g0, w_qkv, rope, want_f32, tm, seq_len):
    t = x.shape[0]
    width = DIFF_HEADS * LANES
    tok = lambda n: pl.BlockSpec((tm, n), lambda i: (i, 0))
    in_specs = [tok(D_MODEL), pl.BlockSpec((1, 1, 6 * D_MODEL), lambda i: (modrow(i), 0, 0)),
                _full((1, D_MODEL)), _full((D_MODEL, 3 * width))]
    args = [x, mod3, g0, w_qkv]
    per = seq_len // tm
    nb = t // seq_len
    if rope is not None:
        in_specs += [pl.BlockSpec((tm, LANES), lambda i: (i % per, 0))] * 3
        args += list(rope)
    heads = pl.BlockSpec((1, DIFF_HEADS, tm, LANES), lambda i: (i // per, 0, i % per, 0))
    head_shape = jax.ShapeDtypeStruct((nb, DIFF_HEADS, seq_len, LANES), BF16)
    n_f32 = 2 if want_f32 else 0
    return pl.pallas_call(
        functools.partial(_c_in_kernel, rope is not None, want_f32),
        grid=(t // tm,),
        in_specs=in_specs,
        out_specs=[heads] * 3 + [tok(width)] * n_f32,
        out_shape=[head_shape] * 3 + [jax.ShapeDtypeStruct((t, width), F32)] * n_f32,
        compiler_params=_params("arbitrary"),
        name="c_in",
    )(*args)


def _diff_attn_kernel(has_ctx, lam_init, *refs):
    if has_ctx:
        q_ref, kn, vn, kc, vc, lam_ref, g_ref, o_ref = refs
        sources = [(kc, kc, vc), (kn, kn, vn)]
    else:
        q_ref, kn, vn, lam_ref, g_ref, o_ref = refs
        sources = [(kn, kn, vn)]
    lp = lam_ref[...]
    lam = (jnp.exp(jnp.sum(lp[0:1] * lp[1:2], axis=-1, keepdims=True))
           - jnp.exp(jnp.sum(lp[2:3] * lp[3:4], axis=-1, keepdims=True)) + lam_init)
    qh = q_ref[...]
    lane = lax.broadcasted_iota(jnp.int32, qh.shape, 1)
    zero = jnp.zeros_like(qh)
    q0 = jnp.where(lane < DIFF_HEAD_DIM, qh, zero)
    q1 = jnp.where(lane < DIFF_HEAD_DIM, zero, qh)
    (l0, acc0), (l1, acc1) = _two_map_flash(q0, q1, sources)
    o = acc0 * (1.0 / l0) - lam * (acc1 * (1.0 / l1))
    o_ref[...] = (_rms(o, g_ref[...]) * (1.0 - lam_init)).astype(BF16)


def _diff_attn(q, k_new, v_new, ctx, lam_p, g_out, lam_init, tq):
    b, nh, n, _ = q.shape
    slab = lambda rows, f: pl.BlockSpec((None, None, rows, LANES), f)
    whole = lambda bi, h, qi: (bi, h, 0, 0)
    in_specs = [slab(tq, lambda bi, h, qi: (bi, h, qi, 0)), slab(n, whole), slab(n, whole)]
    args = [q, k_new, v_new]
    if ctx is not None:
        in_specs += [slab(ctx[0].shape[2], whole)] * 2
        args += list(ctx)
    in_specs += [_full(lam_p.shape), _full(g_out.shape)]
    args += [lam_p, g_out]
    return pl.pallas_call(
        functools.partial(_diff_attn_kernel, ctx is not None, lam_init),
        grid=(b, nh, n // tq),
        in_specs=in_specs,
        out_specs=pl.BlockSpec((None, tq, LANES), lambda bi, h, qi: (bi, qi, h)),
        out_shape=jax.ShapeDtypeStruct((b, n, nh * LANES), BF16),
        compiler_params=_params("arbitrary", "arbitrary", "arbitrary"),
        name="diff_attn",
    )(*args)


def _prep_ab_weights(ai, ab_w_in, mla_g_q, mla_g_kv, mla_w_uq, mla_w_ukv, gla_w_gate_up, gla_b_gate):
    w = ab_w_in[ai]
    d = w.shape[0]
    o = np.cumsum([0, MLA_Q_LORA, MLA_KV_LORA, MLA_ROPE, GLA_HEADS * GLA_DK, GLA_HEADS * GLA_DK,
                   GLA_HEADS * GLA_DV, 2 * GLA_GATE_RANK, GLA_HEADS * GLA_DV])
    q_lat, kv_lat, k_rope, gq, gk, gv, ggate, gout = (w[:, o[i]:o[i + 1]] for i in range(8))
    pad = jnp.zeros((d, LANES - MLA_ROPE - 2 * GLA_GATE_RANK), w.dtype)
    win = jnp.concatenate([q_lat, kv_lat, gq, gk, gv, gout, k_rope, ggate, pad], axis=1).astype(BF16)

    head_pad = LANES - MLA_NOPE - MLA_ROPE
    wq = mla_w_uq[ai].reshape(MLA_Q_LORA, MLA_HEADS, MLA_NOPE + MLA_ROPE)
    wq = jnp.pad(wq, ((0, 0), (0, 0), (0, head_pad))).reshape(MLA_Q_LORA, MLA_HEADS * LANES).astype(BF16)
    wkv = mla_w_ukv[ai].reshape(MLA_KV_LORA, MLA_HEADS, MLA_NOPE + MLA_V)
    wk = jnp.pad(wkv[:, :, :MLA_NOPE], ((0, 0), (0, 0), (0, LANES - MLA_NOPE)))
    wk = wk.reshape(MLA_KV_LORA, MLA_HEADS * LANES).astype(BF16)
    wv = wkv[:, :, MLA_NOPE:].reshape(MLA_KV_LORA, MLA_HEADS * MLA_V).astype(BF16)

    e = np.zeros((LANES, MLA_HEADS * LANES), np.float32)
    for hd in range(MLA_HEADS):
        e[np.arange(MLA_ROPE), hd * LANES + MLA_NOPE + np.arange(MLA_ROPE)] = 1.0

    wg = jnp.zeros((2, LANES, GLA_HEADS * GLA_DK), F32)
    for dr in range(2):
        lo = _MISC_GATE + dr * GLA_GATE_RANK
        wg = wg.at[dr, lo:lo + GLA_GATE_RANK, :].set(gla_w_gate_up[ai, dr])
    return dict(win=win, g_q=mla_g_q[ai][None, :], g_kv=mla_g_kv[ai][None, :], wq=wq, wk=wk, wv=wv,
                e=jnp.asarray(e, dtype=BF16), wg=wg.astype(BF16),
                bg=gla_b_gate[ai].reshape(2, 1, GLA_HEADS * GLA_DK))


def _diff_lambda_init(layer):
    return 0.8 - 0.6 * math.exp(-0.3 * layer)


def kernel(x_prompt, x_sample, cache_mla_ckv, cache_mla_krope, state_gla_fwd, state_gla_bwd,
           cache_diff_k, cache_diff_v, c, c_ctx, w_mod, b_mod, g_norm, ab_w_in, mla_g_q, mla_g_kv,
           mla_w_uq, mla_w_ukv, gla_w_gate_up, gla_b_gate, gla_g_out, ab_w_out, c_w_qkv, diff_lambda,
           diff_g_out, c_w_out, w_ffn_in, w_ffn_out):
    depth = w_mod.shape[0]
    bp, lp, d = x_prompt.shape
    bs, ls, _ = x_sample.shape
    n_ctx = cache_mla_ckv.shape[2]

    n_cond = 1 + bs
    rows = -(-n_cond // 8) * 8
    conds = jnp.concatenate([c_ctx[None, :], c, jnp.zeros((rows - n_cond, d), F32)], axis=0)
    mod3 = _modulation(conds, w_mod, b_mod).reshape(depth * rows, 1, 6 * d)

    per_s = ls // min(TOKEN_TILE, ls)
    rope_q = _rope_tables(ls, MLA_ROPE, MLA_NOPE, LANES)
    rope_misc = _rope_tables(ls, MLA_ROPE, 0, LANES)
    rope_diff = _rope_tables(ls, DIFF_HEAD_DIM, 0, DIFF_HEAD_DIM)

    groups = {
        "prompt": dict(x=x_prompt.reshape(bp * lp, d), b=bp, n=lp, rope=False,
                       modrow=lambda li: (lambda i: li * rows)),
        "sample": dict(x=x_sample.reshape(bs * ls, d), b=bs, n=ls, rope=True,
                       modrow=lambda li: (lambda i: li * rows + 1 + i // per_s)),
    }
    ab_states, c_states = [], []

    ab_w = [_prep_ab_weights(ai, ab_w_in, mla_g_q, mla_g_kv, mla_w_uq, mla_w_ukv, gla_w_gate_up, gla_b_gate)
            for ai in range(ab_w_in.shape[0])]
    ab_w_out_b = ab_w_out.astype(BF16)
    c_w_qkv_b = c_w_qkv.astype(BF16)
    c_w_out_b = c_w_out.astype(BF16)
    w_ffn_in_b = w_ffn_in.astype(BF16)
    w_ffn_out_b = w_ffn_out.astype(BF16)

    for name, grp in groups.items():
        x, b, n = grp["x"], grp["b"], grp["n"]
        is_sample = grp["rope"]
        tm = min(TOKEN_TILE, n)
        for li in range(depth):
            modrow = grp["modrow"](li)
            g = g_norm[li]
            if li % 2 == 0:
                ai = li // 2
                w = ab_w[ai]
                rope = rope_q + rope_misc if is_sample else None
                q, kcat, v, ckv, misc, gq, gk, gv, gout = _ab_in(x, mod3, modrow, g[0:1], w, rope, tm, n)
                sh = lambda a: a.reshape(b, n, a.shape[-1])
                if is_sample:
                    ctx = _ctx_kv(cache_mla_ckv[:, ai], cache_mla_krope[:, ai], w)
                    s0 = (jnp.swapaxes(state_gla_fwd[:, ai], -1, -2), jnp.swapaxes(state_gla_bwd[:, ai], -1, -2))
                else:
                    ctx, s0 = None, None
                mla_o = _mla_attn(q, kcat, v, ctx, min(ATTN_Q_TILE, n))
                o_f, o_b, s_f, s_b = _gla(sh(gq), sh(gk), sh(gv), sh(misc), w, s0, min(GLA_BLOCK, n))
                if not is_sample:
                    ab_states.append((ckv.reshape(b, n, MLA_KV_LORA), misc[:, :MLA_ROPE].reshape(b, n, MLA_ROPE),
                                      jnp.swapaxes(s_f, -1, -2), jnp.swapaxes(s_b, -1, -2)))
                x = _out_proj(mla_o.reshape(b * n, -1),
                              (o_f.reshape(b * n, -1), o_b.reshape(b * n, -1), gout, gla_g_out[ai][None, :]),
                              ab_w_out_b[ai], x, mod3, modrow, g[1:2], tm)
            else:
                ci = li // 2
                rope = rope_diff if is_sample else None
                outs = _c_in(x, mod3, modrow, g[0:1], c_w_qkv_b[ci], rope, not is_sample, tm, n)
                q, k, v = outs[:3]
                if is_sample:
                    head_major = lambda a: jnp.swapaxes(a.reshape(bs, n_ctx, DIFF_HEADS, LANES), 1, 2).astype(BF16)
                    ctx = (head_major(cache_diff_k[:, ci]), head_major(cache_diff_v[:, ci]))
                else:
                    ctx = None
                    c_states.append((outs[3].reshape(b, n, DIFF_HEADS, 2, DIFF_HEAD_DIM),
                                     outs[4].reshape(b, n, DIFF_HEADS, 2 * DIFF_HEAD_DIM)))
                o = _diff_attn(q, k, v, ctx, diff_lambda[ci], diff_g_out[ci][None, :],
                               _diff_lambda_init(li), min(ATTN_Q_TILE, n))
                x = _out_proj(o.reshape(b * n, -1), None, c_w_out_b[ci], x, mod3, modrow, g[1:2], tm)
            x = _ffn(x, mod3, modrow, g[2:3], g[3:4], w_ffn_in_b[li], w_ffn_out_b[li], tm)
        grp["y"] = x.reshape(b, n, d)

    stack = lambda states, i: jnp.stack([s[i] for s in states], axis=1)
    return (groups["prompt"]["y"], groups["sample"]["y"],
            stack(ab_states, 0), stack(ab_states, 1), stack(ab_states, 2), stack(ab_states, 3),
            stack(c_states, 0), stack(c_states, 1))
```

```python
import functools
import math

import numpy as np
import jax
import jax.numpy as jnp
from jax import lax
from jax.experimental import pallas as pl
from jax.experimental.pallas import tpu as pltpu

F32 = jnp.float32
BF16 = jnp.bfloat16

D_MODEL = 1024
GRID_W = 64
ROPE_BASE = 10000.0
NORM_EPS = 1e-6

MLA_HEADS = 8
MLA_NOPE = 64
MLA_ROPE = 32
MLA_V = 64
MLA_Q_LORA = 384
MLA_KV_LORA = 256
GLA_HEADS = 4
GLA_DK = 128
GLA_DV = 128
GLA_GATE_RANK = 16
GLA_GATE_NORM = 16.0
GLA_CHUNK = 64
DIFF_HEADS = 8
DIFF_HEAD_DIM = 64
FFN_HIDDEN = 2816

LANES = 128
VMEM_LIMIT = 56 * 1024 * 1024

_OFF_QLAT = 0
_OFF_KVLAT = _OFF_QLAT + MLA_Q_LORA
_OFF_GQ = _OFF_KVLAT + MLA_KV_LORA
_OFF_GK = _OFF_GQ + GLA_HEADS * GLA_DK
_OFF_GV = _OFF_GK + GLA_HEADS * GLA_DK
_OFF_GOUT = _OFF_GV + GLA_HEADS * GLA_DV
_OFF_MISC = _OFF_GOUT + GLA_HEADS * GLA_DV
_AB_COLS = _OFF_MISC + LANES
_MISC_GATE = MLA_ROPE

LOG2E = math.log2(math.e)

TOKEN_TILE = 512
ATTN_Q_TILE = 512
ATTN_K_TILE = 512
GLA_BLOCK = 256
FFN_SPLIT = 2


def _rms(x, g):
    var = jnp.mean(x * x, axis=-1, keepdims=True)
    return x * lax.rsqrt(var + NORM_EPS) * g


def _silu(x):
    return x * (1.0 / (1.0 + jnp.exp(-x)))


def _log_sigmoid(x):
    return -(jnp.maximum(-x, 0.0) + jnp.log1p(jnp.exp(-jnp.abs(x))))


def _rope(x, c, se, so):
    n = x.shape[-1]
    return x * c + pltpu.roll(x, n - 1, 1) * se + pltpu.roll(x, 1, 1) * so


def _dot(a, b):
    return jnp.dot(a, b, preferred_element_type=F32)


def _dot_nt(a, b):
    return lax.dot_general(a, b, (((1,), (1,)), ((), ())), preferred_element_type=F32)


def _dot_tn(a, b):
    return lax.dot_general(a, b, (((0,), (0,)), ((), ())), preferred_element_type=F32)


def _params(*sem):
    return pltpu.CompilerParams(dimension_semantics=sem, vmem_limit_bytes=VMEM_LIMIT)


def _full(shape):
    zeros = (0,) * len(shape)
    return pl.BlockSpec(shape, lambda *_: zeros)


def _rope_tables(n_tokens, rot_dim, lane_lo, period):
    rows = n_tokens // GRID_W
    row = np.repeat(np.arange(rows, dtype=np.float64), GRID_W)
    col = np.tile(np.arange(GRID_W, dtype=np.float64), rows)
    n_freq = rot_dim // 4
    inv = ROPE_BASE ** (-np.arange(n_freq, dtype=np.float64) / n_freq)
    ang = np.concatenate([row[:, None] * inv, col[:, None] * inv], axis=-1)
    cos, sin = np.cos(ang), np.sin(ang)
    c = np.ones((n_tokens, period))
    se = np.zeros((n_tokens, period))
    so = np.zeros((n_tokens, period))
    c[:, lane_lo:lane_lo + rot_dim] = np.repeat(cos, 2, axis=-1)
    se[:, lane_lo:lane_lo + rot_dim:2] = -sin
    so[:, lane_lo + 1:lane_lo + rot_dim:2] = sin
    reps = LANES // period
    return tuple(jnp.asarray(np.tile(t, (1, reps)), dtype=F32) for t in (c, se, so))


def _mod_kernel(c_ref, w_ref, b_ref, o_ref):
    a = _silu(c_ref[...]).astype(BF16)
    o_ref[0] = _dot(a, w_ref[0].astype(BF16)) + b_ref[0]


def _modulation(conds, w_mod, b_mod):
    depth, d, n = w_mod.shape
    rows = conds.shape[0]
    tn = 1536
    return pl.pallas_call(
        _mod_kernel,
        grid=(depth, n // tn),
        in_specs=[_full((rows, d)),
                  pl.BlockSpec((1, d, tn), lambda l, j: (l, 0, j)),
                  pl.BlockSpec((1, 1, tn), lambda l, j: (l, 0, j))],
        out_specs=pl.BlockSpec((1, rows, tn), lambda l, j: (l, 0, j)),
        out_shape=jax.ShapeDtypeStruct((depth, rows, n), F32),
        compiler_params=_params("arbitrary", "arbitrary"),
        name="modulation",
    )(conds, w_mod, b_mod.reshape(depth, 1, n))


def _ab_in_kernel(use_rope, *refs):
    (x_ref, mod_ref, g0_ref, win_ref, gq_ref, gkv_ref, wq_ref, wk_ref, wv_ref, e_ref) = refs[:10]
    refs = refs[10:]
    if use_rope:
        cq, seq, soq, cm, sem, som = refs[:6]
        refs = refs[6:]
    (q_out, kcat_out, v_out, ckv_out, misc_out, gq_out, gk_out, gv_out, gout_out) = refs

    mod = mod_ref[0]
    shift, scale = mod[:, 0:D_MODEL], mod[:, D_MODEL:2 * D_MODEL]
    h = _rms(x_ref[...], g0_ref[...]) * (1.0 + scale) + shift
    proj = _dot(h.astype(BF16), win_ref[...])

    gq_out[...] = (proj[:, _OFF_GQ:_OFF_GK] * (GLA_DK ** -0.5)).astype(BF16)
    gk_out[...] = proj[:, _OFF_GK:_OFF_GV].astype(BF16)
    gv_out[...] = proj[:, _OFF_GV:_OFF_GOUT].astype(BF16)
    gout_out[...] = proj[:, _OFF_GOUT:_OFF_MISC].astype(BF16)

    misc = proj[:, _OFF_MISC:_AB_COLS]
    if use_rope:
        misc = _rope(misc, cm[...], sem[...], som[...])
    misc_out[...] = misc

    ckv = _rms(proj[:, _OFF_KVLAT:_OFF_GQ], gkv_ref[...])
    ckv_out[...] = ckv
    ckv_b = ckv.astype(BF16)
    kcat = (_dot(ckv_b, wk_ref[...]) + _dot(misc.astype(BF16), e_ref[...])).astype(BF16)
    v = _dot(ckv_b, wv_ref[...]).astype(BF16)
    for hp in range(MLA_HEADS // 2):
        v_out[0, hp] = v[:, hp * LANES:(hp + 1) * LANES]

    qn = _rms(proj[:, _OFF_QLAT:_OFF_KVLAT], gq_ref[...]).astype(BF16)
    q = _dot(qn, wq_ref[...])
    sm_scale = (MLA_NOPE + MLA_ROPE) ** -0.5 * LOG2E
    for hd in range(MLA_HEADS):
        sl = slice(hd * LANES, (hd + 1) * LANES)
        kcat_out[0, hd] = kcat[:, sl]
        qh = q[:, sl]
        if use_rope:
            qh = _rope(qh, cq[...], seq[...], soq[...])
        q_out[0, hd] = (qh * sm_scale).astype(BF16)


def _ab_in(x, mod3, modrow, g0, w, rope, tm, seq_len):
    t = x.shape[0]
    nt = t // tm
    tok = lambda n: pl.BlockSpec((tm, n), lambda i: (i, 0))
    in_specs = [tok(D_MODEL),
                pl.BlockSpec((1, 1, 6 * D_MODEL), lambda i: (modrow(i), 0, 0)),
                _full((1, D_MODEL)), _full((D_MODEL, _AB_COLS)),
                _full((1, MLA_Q_LORA)), _full((1, MLA_KV_LORA)),
                _full((MLA_Q_LORA, MLA_HEADS * LANES)), _full((MLA_KV_LORA, MLA_HEADS * LANES)),
                _full((MLA_KV_LORA, MLA_HEADS * MLA_V)), _full((LANES, MLA_HEADS * LANES))]
    args = [x, mod3, g0, w["win"], w["g_q"], w["g_kv"], w["wq"], w["wk"], w["wv"], w["e"]]
    per = seq_len // tm
    if rope is not None:
        in_specs += [pl.BlockSpec((tm, LANES), lambda i: (i % per, 0))] * 6
        args += list(rope)
    nb = t // seq_len
    heads = lambda nh: pl.BlockSpec((1, nh, tm, LANES), lambda i: (i // per, 0, i % per, 0))
    head_shape = lambda nh: jax.ShapeDtypeStruct((nb, nh, seq_len, LANES), BF16)
    widths = [(MLA_KV_LORA, F32), (LANES, F32)] + [(GLA_HEADS * GLA_DK, BF16)] * 4
    return pl.pallas_call(
        functools.partial(_ab_in_kernel, rope is not None),
        grid=(nt,),
        in_specs=in_specs,
        out_specs=[heads(MLA_HEADS), heads(MLA_HEADS), heads(MLA_HEADS // 2)] + [tok(n) for n, _ in widths],
        out_shape=[head_shape(MLA_HEADS), head_shape(MLA_HEADS), head_shape(MLA_HEADS // 2)]
        + [jax.ShapeDtypeStruct((t, n), dt) for n, dt in widths],
        compiler_params=_params("arbitrary"),
        name="ab_in",
    )(*args)


def _ctx_kv_kernel(ckv_ref, kr_ref, wk_ref, wv_ref, e_ref, kcat_out, v_out):
    c = ckv_ref[...].astype(BF16)
    kr = kr_ref[...].astype(BF16)
    kcat = (_dot(c, wk_ref[...]) + _dot(kr, e_ref[...])).astype(BF16)
    v = _dot(c, wv_ref[...]).astype(BF16)
    for hd in range(MLA_HEADS):
        kcat_out[0, hd] = kcat[:, hd * LANES:(hd + 1) * LANES]
    for hp in range(MLA_HEADS // 2):
        v_out[0, hp] = v[:, hp * LANES:(hp + 1) * LANES]


def _ctx_kv(ckv, krope, w):
    b, n_ctx, _ = ckv.shape
    tok = lambda n: pl.BlockSpec((None, n_ctx, n), lambda i: (i, 0, 0))
    heads = lambda nh: pl.BlockSpec((1, nh, n_ctx, LANES), lambda i: (i, 0, 0, 0))
    return pl.pallas_call(
        _ctx_kv_kernel,
        grid=(b,),
        in_specs=[tok(MLA_KV_LORA), tok(MLA_ROPE),
                  _full((MLA_KV_LORA, MLA_HEADS * LANES)), _full((MLA_KV_LORA, MLA_HEADS * MLA_V)),
                  _full((MLA_ROPE, MLA_HEADS * LANES))],
        out_specs=[heads(MLA_HEADS), heads(MLA_HEADS // 2)],
        out_shape=[jax.ShapeDtypeStruct((b, MLA_HEADS, n_ctx, LANES), BF16),
                   jax.ShapeDtypeStruct((b, MLA_HEADS // 2, n_ctx, LANES), BF16)],
        compiler_params=_params("arbitrary"),
        name="mla_ctx_kv",
    )(ckv, krope, w["wk"], w["wv"], w["e"][:MLA_ROPE])


_NEG = -1e30


def _two_map_flash(q0, q1, sources):
    state = [None, None]
    for k0_ref, k1_ref, v_ref in sources:
        rows = v_ref.shape[0]
        tk = min(ATTN_K_TILE, rows)
        ones = jnp.ones((tk, LANES), BF16)
        for c in range(rows // tk):
            r = slice(c * tk, (c + 1) * tk)
            v = jnp.concatenate([v_ref[r, :], ones], axis=1)
            for j, (q, k_ref) in enumerate(((q0, k0_ref), (q1, k1_ref))):
                s = _dot_nt(q, k_ref[r, :])
                m_blk = jnp.max(s, axis=-1, keepdims=True)
                if state[j] is None:
                    p = jnp.exp2(s - m_blk)
                    state[j] = (m_blk, _dot(p.astype(BF16), v))
                else:
                    m, acc = state[j]
                    m_new = jnp.maximum(m, m_blk)
                    p = jnp.exp2(s - m_new)
                    state[j] = (m_new, jnp.exp2(m - m_new) * acc + _dot(p.astype(BF16), v))
    return tuple((acc[:, LANES:], acc[:, :LANES]) for _, acc in state)


def _mla_attn_kernel(has_ctx, *refs):
    if has_ctx:
        q0_ref, q1_ref, k0n, k1n, vn, k0c, k1c, vc, o_ref = refs
        sources = [(k0c, k1c, vc), (k0n, k1n, vn)]
    else:
        q0_ref, q1_ref, k0n, k1n, vn, o_ref = refs
        sources = [(k0n, k1n, vn)]
    (l0, acc0), (l1, acc1) = _two_map_flash(q0_ref[...], q1_ref[...], sources)
    lane = lax.broadcasted_iota(jnp.int32, acc0.shape, 1)
    o_ref[...] = jnp.where(lane < MLA_V, acc0 * (1.0 / l0), acc1 * (1.0 / l1)).astype(BF16)


def _mla_attn(q, k_new, v_new, ctx, tq):
    b, nh, n, _ = q.shape
    slab = lambda rows, f: pl.BlockSpec((None, None, rows, LANES), f)
    even = lambda bi, p, qi: (bi, 2 * p, 0, 0)
    odd = lambda bi, p, qi: (bi, 2 * p + 1, 0, 0)
    pair = lambda bi, p, qi: (bi, p, 0, 0)
    in_specs = [slab(tq, lambda bi, p, qi: (bi, 2 * p, qi, 0)), slab(tq, lambda bi, p, qi: (bi, 2 * p + 1, qi, 0)),
                slab(n, even), slab(n, odd), slab(n, pair)]
    args = [q, q, k_new, k_new, v_new]
    if ctx is not None:
        nc = ctx[0].shape[2]
        in_specs += [slab(nc, even), slab(nc, odd), slab(nc, pair)]
        args += [ctx[0], ctx[0], ctx[1]]
    return pl.pallas_call(
        functools.partial(_mla_attn_kernel, ctx is not None),
        grid=(b, nh // 2, n // tq),
        in_specs=in_specs,
        out_specs=pl.BlockSpec((None, tq, LANES), lambda bi, p, qi: (bi, qi, p)),
        out_shape=jax.ShapeDtypeStruct((b, n, (nh // 2) * LANES), BF16),
        compiler_params=_params("arbitrary", "arbitrary", "arbitrary"),
        name="mla_attn",
    )(*args)


def _gla_kernel(has_s0, *refs):
    fwd_in, bwd_in = refs[0:4], refs[4:8]
    wg_ref, bg_ref, tf_ref, tb_ref = refs[8:12]
    refs = refs[12:]
    if has_s0:
        s0f_ref, s0b_ref = refs[:2]
        refs = refs[2:]
    of_ref, ob_ref, sf_ref, sb_ref = refs
    tb_rows = of_ref.shape[1]
    nch = tb_rows // GLA_CHUNK

    @pl.when(pl.program_id(1) == 0)
    def _():
        if has_s0:
            sf_ref[...] = s0f_ref[...]
            sb_ref[...] = s0b_ref[...]
        else:
            sf_ref[...] = jnp.zeros(sf_ref.shape, F32)
            sb_ref[...] = jnp.zeros(sb_ref.shape, F32)

    for d, ((q_ref, k_ref, v_ref, m_ref), t_ref, o_ref, s_ref) in enumerate(
            ((fwd_in, tf_ref, of_ref, sf_ref), (bwd_in, tb_ref, ob_ref, sb_ref))):
        t = t_ref[...]
        row = lax.broadcasted_iota(jnp.int32, t.shape, 0)
        col = lax.broadcasted_iota(jnp.int32, t.shape, 1)
        chunk_lo = (row // GLA_CHUNK) * GLA_CHUNK
        if d == 0:
            tmask = (col <= row) & (col >= chunk_lo)
        else:
            tmask = (col >= row) & (col < chunk_lo + GLA_CHUNK)
        pre = _dot(m_ref[0].astype(BF16), wg_ref[d]) + bg_ref[d]
        g = _log_sigmoid(pre) * (1.0 / GLA_GATE_NORM)
        g_hi = g.astype(BF16)
        g_lo = (g - g_hi.astype(F32)).astype(BF16)
        cum = _dot(t, g_hi) + _dot(t, g_lo)
        for hd in range(GLA_HEADS):
            sl = slice(hd * LANES, (hd + 1) * LANES)
            cs = cum[:, sl]
            q = q_ref[0, :, sl].astype(F32)
            k = k_ref[0, :, sl].astype(F32)
            v = v_ref[0, :, sl]
            q_in = (q * jnp.exp(cs)).astype(BF16)
            k_in = (k * jnp.exp(-cs)).astype(BF16)
            a = jnp.where(tmask, _dot_nt(q_in, k_in), 0.0)
            o_intra = _dot(a.astype(BF16), v)
            s_t = s_ref[0, hd]
            order = range(nch) if d == 0 else range(nch - 1, -1, -1)
            for n in order:
                r = slice(n * GLA_CHUNK, (n + 1) * GLA_CHUNK)
                edge = (n + 1) * GLA_CHUNK - 1 if d == 0 else n * GLA_CHUNK
                tot = cs[edge:edge + 1, :]
                o_ref[0, r, sl] = o_intra[r] + _dot_nt(q_in[r], s_t.astype(BF16))
                k_dec = (k[r] * jnp.exp(tot - cs[r])).astype(BF16)
                s_t = s_t * jnp.exp(tot) + _dot_tn(v[r], k_dec)
            s_ref[0, hd] = s_t


def _gla_masks(tb_rows):
    idx = np.arange(tb_rows)
    same = (idx[:, None] // GLA_CHUNK) == (idx[None, :] // GLA_CHUNK)
    fwd = same & (idx[None, :] <= idx[:, None])
    bwd = same & (idx[None, :] >= idx[:, None])
    return jnp.asarray(fwd, dtype=BF16), jnp.asarray(bwd, dtype=BF16)


def _gla(gq, gk, gv, misc, w, s0, tb_rows):
    b, n, width = gq.shape
    nb = n // tb_rows
    fw = lambda cols: pl.BlockSpec((1, tb_rows, cols), lambda bi, i: (bi, i, 0))
    bw = lambda cols: pl.BlockSpec((1, tb_rows, cols), lambda bi, i: (bi, nb - 1 - i, 0))
    st = pl.BlockSpec((1, GLA_HEADS, GLA_DV, GLA_DK), lambda bi, i: (bi, 0, 0, 0))
    tf, tb = _gla_masks(tb_rows)
    in_specs = ([fw(width)] * 3 + [fw(LANES)] + [bw(width)] * 3 + [bw(LANES)]
                + [_full((2, LANES, width)), _full((2, 1, width)),
                   _full((tb_rows, tb_rows)), _full((tb_rows, tb_rows))])
    args = [gq, gk, gv, misc, gq, gk, gv, misc, w["wg"], w["bg"], tf, tb]
    if s0 is not None:
        in_specs += [st, st]
        args += list(s0)
    return pl.pallas_call(
        functools.partial(_gla_kernel, s0 is not None),
        grid=(b, nb),
        in_specs=in_specs,
        out_specs=[fw(width), bw(width), st, st],
        out_shape=[jax.ShapeDtypeStruct((b, n, width), F32)] * 2
        + [jax.ShapeDtypeStruct((b, GLA_HEADS, GLA_DV, GLA_DK), F32)] * 2,
        compiler_params=_params("arbitrary", "arbitrary"),
        name="gla",
    )(*args)


def _out_kernel(has_gla, gate_off, *refs):
    if has_gla:
        mix_ref, of_ref, ob_ref, gout_ref, ggo_ref, w_ref, x_ref, mod_ref, g_ref, o_ref = refs
        o = of_ref[...] + ob_ref[...]
        parts = []
        for hd in range(GLA_HEADS):
            sl = slice(hd * LANES, (hd + 1) * LANES)
            gate = gout_ref[:, sl].astype(F32)
            parts.append((_rms(o[:, sl], ggo_ref[...]) * _silu(gate)).astype(BF16))
        n_mla = mix_ref.shape[1]
        out = _dot(mix_ref[...], w_ref[0:n_mla, :]) + _dot(jnp.concatenate(parts, axis=-1), w_ref[n_mla:, :])
    else:
        mix_ref, w_ref, x_ref, mod_ref, g_ref, o_ref = refs
        out = _dot(mix_ref[...], w_ref[...])
    gate = mod_ref[0][:, gate_off:gate_off + D_MODEL]
    o_ref[...] = x_ref[...] + gate * _rms(out, g_ref[...])


def _out_proj(mix, gla, w_out, x, mod3, modrow, g, tm):
    t = x.shape[0]
    tok = lambda n: pl.BlockSpec((tm, n), lambda i: (i, 0))
    in_specs = [tok(mix.shape[1])]
    args = [mix]
    if gla is not None:
        o_f, o_b, gout, g_gla = gla
        in_specs += [tok(o_f.shape[1])] * 3 + [_full((1, GLA_DV))]
        args += [o_f, o_b, gout, g_gla]
    in_specs += [_full(w_out.shape), tok(D_MODEL),
                 pl.BlockSpec((1, 1, 6 * D_MODEL), lambda i: (modrow(i), 0, 0)), _full((1, D_MODEL))]
    args += [w_out, x, mod3, g]
    return pl.pallas_call(
        functools.partial(_out_kernel, gla is not None, 2 * D_MODEL),
        grid=(t // tm,),
        in_specs=in_specs,
        out_specs=tok(D_MODEL),
        out_shape=jax.ShapeDtypeStruct((t, D_MODEL), F32),
        compiler_params=_params("arbitrary"),
        name="out_proj",
    )(*args)


def _ffn_kernel(x_ref, mod_ref, g2_ref, g3_ref, wg_ref, wu_ref, wo_ref, o_ref, h_scr, acc_scr):
    j = pl.program_id(1)

    @pl.when(j == 0)
    def _():
        mod = mod_ref[0]
        shift, scale = mod[:, 3 * D_MODEL:4 * D_MODEL], mod[:, 4 * D_MODEL:5 * D_MODEL]
        h_scr[...] = (_rms(x_ref[...], g2_ref[...]) * (1.0 + scale) + shift).astype(BF16)
        acc_scr[...] = jnp.zeros(acc_scr.shape, F32)

    hb = h_scr[...]
    act = (_silu(_dot(hb, wg_ref[...])) * _dot(hb, wu_ref[...])).astype(BF16)
    acc_scr[...] += _dot(act, wo_ref[...])

    @pl.when(j == pl.num_programs(1) - 1)
    def _():
        gate = mod_ref[0][:, 5 * D_MODEL:6 * D_MODEL]
        o_ref[...] = x_ref[...] + gate * _rms(acc_scr[...], g3_ref[...])


def _ffn(x, mod3, modrow, g2, g3, w_in, w_out, tm):
    t = x.shape[0]
    th = FFN_HIDDEN // FFN_SPLIT
    tok = pl.BlockSpec((tm, D_MODEL), lambda i, j: (i, 0))
    vec = pl.BlockSpec((1, D_MODEL), lambda i, j: (0, 0))
    return pl.pallas_call(
        _ffn_kernel,
        grid=(t // tm, FFN_SPLIT),
        in_specs=[tok, pl.BlockSpec((1, 1, 6 * D_MODEL), lambda i, j: (modrow(i), 0, 0)), vec, vec,
                  pl.BlockSpec((D_MODEL, th), lambda i, j: (0, j)),
                  pl.BlockSpec((D_MODEL, th), lambda i, j: (0, j + FFN_SPLIT)),
                  pl.BlockSpec((th, D_MODEL), lambda i, j: (j, 0))],
        out_specs=tok,
        out_shape=jax.ShapeDtypeStruct((t, D_MODEL), F32),
        scratch_shapes=[pltpu.VMEM((tm, D_MODEL), BF16), pltpu.VMEM((tm, D_MODEL), F32)],
        compiler_params=_params("arbitrary", "arbitrary"),
        name="ffn",
    )(x, mod3, g2, g3, w_in, w_in, w_out)


def _c_in_kernel(use_rope, want_f32, *refs):
    x_ref, mod_ref, g0_ref, w_ref = refs[:4]
    refs = refs[4:]
    if use_rope:
        c_ref, se_ref, so_ref = refs[:3]
        refs = refs[3:]
    q_out, k_out, v_out = refs[:3]
    mod = mod_ref[0]
    shift, scale = mod[:, 0:D_MODEL], mod[:, D_MODEL:2 * D_MODEL]
    h = _rms(x_ref[...], g0_ref[...]) * (1.0 + scale) + shift
    proj = _dot(h.astype(BF16), w_ref[...])
    width = DIFF_HEADS * LANES
    sm_scale = DIFF_HEAD_DIM ** -0.5 * LOG2E
    for hd in range(DIFF_HEADS):
        sl = slice(hd * LANES, (hd + 1) * LANES)
        qh = proj[:, hd * LANES:(hd + 1) * LANES]
        kh = proj[:, width + hd * LANES:width + (hd + 1) * LANES]
        vh = proj[:, 2 * width + hd * LANES:2 * width + (hd + 1) * LANES]
        if use_rope:
            qh = _rope(qh, c_ref[...], se_ref[...], so_ref[...])
            kh = _rope(kh, c_ref[...], se_ref[...], so_ref[...])
        q_out[0, hd] = (qh * sm_scale).astype(BF16)
        k_out[0, hd] = kh.astype(BF16)
        v_out[0, hd] = vh.astype(BF16)
        if want_f32:
            refs[3][:, sl] = kh
            refs[4][:, sl] = vh


def _c_in(x, mod3, modrow, g0, w_qkv, rope, want_f32, tm, seq_len):
    t = x.shape[0]
    width = DIFF_HEADS * LANES
    tok = lambda n: pl.BlockSpec((tm, n), lambda i: (i, 0))
    in_specs = [tok(D_MODEL), pl.BlockSpec((1, 1, 6 * D_MODEL), lambda i: (modrow(i), 0, 0)),
                _full((1, D_MODEL)), _full((D_MODEL, 3 * width))]
    args = [x, mod3, g0, w_qkv]
    per = seq_len // tm
    nb = t // seq_len
    if rope is not None:
        in_specs += [pl.BlockSpec((tm, LANES), lambda i: (i % per, 0))] * 3
        args += list(rope)
    heads = pl.BlockSpec((1, DIFF_HEADS, tm, LANES), lambda i: (i // per, 0, i % per, 0))
    head_shape = jax.ShapeDtypeStruct((nb, DIFF_HEADS, seq_len, LANES), BF16)
    n_f32 = 2 if want_f32 else 0
    return pl.pallas_call(
        functools.partial(_c_in_kernel, rope is not None, want_f32),
        grid=(t // tm,),
        in_specs=in_specs,
        out_specs=[heads] * 3 + [tok(width)] * n_f32,
        out_shape=[head_shape] * 3 + [jax.ShapeDtypeStruct((t, width), F32)] * n_f32,
        compiler_params=_params("arbitrary"),
        name="c_in",
    )(*args)


def _diff_attn_kernel(has_ctx, lam_init, *refs):
    if has_ctx:
        q_ref, kn, vn, kc, vc, lam_ref, g_ref, o_ref = refs
        sources = [(kc, kc, vc), (kn, kn, vn)]
    else:
        q_ref, kn, vn, lam_ref, g_ref, o_ref = refs
        sources = [(kn, kn, vn)]
    lp = lam_ref[...]
    lam = (jnp.exp(jnp.sum(lp[0:1] * lp[1:2], axis=-1, keepdims=True))
           - jnp.exp(jnp.sum(lp[2:3] * lp[3:4], axis=-1, keepdims=True)) + lam_init)
    qh = q_ref[...]
    lane = lax.broadcasted_iota(jnp.int32, qh.shape, 1)
    zero = jnp.zeros_like(qh)
    q0 = jnp.where(lane < DIFF_HEAD_DIM, qh, zero)
    q1 = jnp.where(lane < DIFF_HEAD_DIM, zero, qh)
    (l0, acc0), (l1, acc1) = _two_map_flash(q0, q1, sources)
    o = acc0 * (1.0 / l0) - lam * (acc1 * (1.0 / l1))
    o_ref[...] = (_rms(o, g_ref[...]) * (1.0 - lam_init)).astype(BF16)


def _diff_attn(q, k_new, v_new, ctx, lam_p, g_out, lam_init, tq):
    b, nh, n, _ = q.shape
    slab = lambda rows, f: pl.BlockSpec((None, None, rows, LANES), f)
    whole = lambda bi, h, qi: (bi, h, 0, 0)
    in_specs = [slab(tq, lambda bi, h, qi: (bi, h, qi, 0)), slab(n, whole), slab(n, whole)]
    args = [q, k_new, v_new]
    if ctx is not None:
        in_specs += [slab(ctx[0].shape[2], whole)] * 2
        args += list(ctx)
    in_specs += [_full(lam_p.shape), _full(g_out.shape)]
    args += [lam_p, g_out]
    return pl.pallas_call(
        functools.partial(_diff_attn_kernel, ctx is not None, lam_init),
        grid=(b, nh, n // tq),
        in_specs=in_specs,
        out_specs=pl.BlockSpec((None, tq, LANES), lambda bi, h, qi: (bi, qi, h)),
        out_shape=jax.ShapeDtypeStruct((b, n, nh * LANES), BF16),
        compiler_params=_params("arbitrary", "arbitrary", "arbitrary"),
        name="diff_attn",
    )(*args)


def _prep_ab_weights(ai, ab_w_in, mla_g_q, mla_g_kv, mla_w_uq, mla_w_ukv, gla_w_gate_up, gla_b_gate):
    w = ab_w_in[ai]
    d = w.shape[0]
    o = np.cumsum([0, MLA_Q_LORA, MLA_KV_LORA, MLA_ROPE, GLA_HEADS * GLA_DK, GLA_HEADS * GLA_DK,
                   GLA_HEADS * GLA_DV, 2 * GLA_GATE_RANK, GLA_HEADS * GLA_DV])
    q_lat, kv_lat, k_rope, gq, gk, gv, ggate, gout = (w[:, o[i]:o[i + 1]] for i in range(8))
    pad = jnp.zeros((d, LANES - MLA_ROPE - 2 * GLA_GATE_RANK), w.dtype)
    win = jnp.concatenate([q_lat, kv_lat, gq, gk, gv, gout, k_rope, ggate, pad], axis=1).astype(BF16)

    head_pad = LANES - MLA_NOPE - MLA_ROPE
    wq = mla_w_uq[ai].reshape(MLA_Q_LORA, MLA_HEADS, MLA_NOPE + MLA_ROPE)
    wq = jnp.pad(wq, ((0, 0), (0, 0), (0, head_pad))).reshape(MLA_Q_LORA, MLA_HEADS * LANES).astype(BF16)
    wkv = mla_w_ukv[ai].reshape(MLA_KV_LORA, MLA_HEADS, MLA_NOPE + MLA_V)
    wk = jnp.pad(wkv[:, :, :MLA_NOPE], ((0, 0), (0, 0), (0, LANES - MLA_NOPE)))
    wk = wk.reshape(MLA_KV_LORA, MLA_HEADS * LANES).astype(BF16)
    wv = wkv[:, :, MLA_NOPE:].reshape(MLA_KV_LORA, MLA_HEADS * MLA_V).astype(BF16)

    e = np.zeros((LANES, MLA_HEADS * LANES), np.float32)
    for hd in range(MLA_HEADS):
        e[np.arange(MLA_ROPE), hd * LANES + MLA_NOPE + np.arange(MLA_ROPE)] = 1.0

    wg = jnp.zeros((2, LANES, GLA_HEADS * GLA_DK), F32)
    for dr in range(2):
        lo = _MISC_GATE + dr * GLA_GATE_RANK
        wg = wg.at[dr, lo:lo + GLA_GATE_RANK, :].set(gla_w_gate_up[ai, dr])
    return dict(win=win, g_q=mla_g_q[ai][None, :], g_kv=mla_g_kv[ai][None, :], wq=wq, wk=wk, wv=wv,
                e=jnp.asarray(e, dtype=BF16), wg=wg.astype(BF16),
                bg=gla_b_gate[ai].reshape(2, 1, GLA_HEADS * GLA_DK))


def _diff_lambda_init(layer):
    return 0.8 - 0.6 * math.exp(-0.3 * layer)


def kernel(x_prompt, x_sample, cache_mla_ckv, cache_mla_krope, state_gla_fwd, state_gla_bwd,
           cache_diff_k, cache_diff_v, c, c_ctx, w_mod, b_mod, g_norm, ab_w_in, mla_g_q, mla_g_kv,
           mla_w_uq, mla_w_ukv, gla_w_gate_up, gla_b_gate, gla_g_out, ab_w_out, c_w_qkv, diff_lambda,
           diff_g_out, c_w_out, w_ffn_in, w_ffn_out):
    depth = w_mod.shape[0]
    bp, lp, d = x_prompt.shape
    bs, ls, _ = x_sample.shape
    n_ctx = cache_mla_ckv.shape[2]

    n_cond = 1 + bs
    rows = -(-n_cond // 8) * 8
    conds = jnp.concatenate([c_ctx[None, :], c, jnp.zeros((rows - n_cond, d), F32)], axis=0)
    mod3 = _modulation(conds, w_mod, b_mod).reshape(depth * rows, 1, 6 * d)

    per_s = ls // min(TOKEN_TILE, ls)
    rope_q = _rope_tables(ls, MLA_ROPE, MLA_NOPE, LANES)
    rope_misc = _rope_tables(ls, MLA_ROPE, 0, LANES)
    rope_diff = _rope_tables(ls, DIFF_HEAD_DIM, 0, DIFF_HEAD_DIM)

    groups = {
        "prompt": dict(x=x_prompt.reshape(bp * lp, d), b=bp, n=lp, rope=False,
                       modrow=lambda li: (lambda i: li * rows)),
        "sample": dict(x=x_sample.reshape(bs * ls, d), b=bs, n=ls, rope=True,
                       modrow=lambda li: (lambda i: li * rows + 1 + i // per_s)),
    }
    ab_states, c_states = [], []

    ab_w = [_prep_ab_weights(ai, ab_w_in, mla_g_q, mla_g_kv, mla_w_uq, mla_w_ukv, gla_w_gate_up, gla_b_gate)
            for ai in range(ab_w_in.shape[0])]
    ab_w_out_b = ab_w_out.astype(BF16)
    c_w_qkv_b = c_w_qkv.astype(BF16)
    c_w_out_b = c_w_out.astype(BF16)
    w_ffn_in_b = w_ffn_in.astype(BF16)
    w_ffn_out_b = w_ffn_out.astype(BF16)

    for name, grp in groups.items():
        x, b, n = grp["x"], grp["b"], grp["n"]
        is_sample = grp["rope"]
        tm = min(TOKEN_TILE, n)
        for li in range(depth):
            modrow = grp["modrow"](li)
            g = g_norm[li]
            if li % 2 == 0:
                ai = li // 2
                w = ab_w[ai]
                rope = rope_q + rope_misc if is_sample else None
                q, kcat, v, ckv, misc, gq, gk, gv, gout = _ab_in(x, mod3, modrow, g[0:1], w, rope, tm, n)
                sh = lambda a: a.reshape(b, n, a.shape[-1])
                if is_sample:
                    ctx = _ctx_kv(cache_mla_ckv[:, ai], cache_mla_krope[:, ai], w)
                    s0 = (jnp.swapaxes(state_gla_fwd[:, ai], -1, -2), jnp.swapaxes(state_gla_bwd[:, ai], -1, -2))
                else:
                    ctx, s0 = None, None
                mla_o = _mla_attn(q, kcat, v, ctx, min(ATTN_Q_TILE, n))
                o_f, o_b, s_f, s_b = _gla(sh(gq), sh(gk), sh(gv), sh(misc), w, s0, min(GLA_BLOCK, n))
                if not is_sample:
                    ab_states.append((ckv.reshape(b, n, MLA_KV_LORA), misc[:, :MLA_ROPE].reshape(b, n, MLA_ROPE),
                                      jnp.swapaxes(s_f, -1, -2), jnp.swapaxes(s_b, -1, -2)))
                x = _out_proj(mla_o.reshape(b * n, -1),
                              (o_f.reshape(b * n, -1), o_b.reshape(b * n, -1), gout, gla_g_out[ai][None, :]),
                              ab_w_out_b[ai], x, mod3, modrow, g[1:2], tm)
            else:
                ci = li // 2
                rope = rope_diff if is_sample else None
                outs = _c_in(x, mod3, modrow, g[0:1], c_w_qkv_b[ci], rope, not is_sample, tm, n)
                q, k, v = outs[:3]
                if is_sample:
                    head_major = lambda a: jnp.swapaxes(a.reshape(bs, n_ctx, DIFF_HEADS, LANES), 1, 2).astype(BF16)
                    ctx = (head_major(cache_diff_k[:, ci]), head_major(cache_diff_v[:, ci]))
                else:
                    ctx = None
                    c_states.append((outs[3].reshape(b, n, DIFF_HEADS, 2, DIFF_HEAD_DIM),
                                     outs[4].reshape(b, n, DIFF_HEADS, 2 * DIFF_HEAD_DIM)))
                o = _diff_attn(q, k, v, ctx, diff_lambda[ci], diff_g_out[ci][None, :],
                               _diff_lambda_init(li), min(ATTN_Q_TILE, n))
                x = _out_proj(o.reshape(b * n, -1), None, c_w_out_b[ci], x, mod3, modrow, g[1:2], tm)
            x = _ffn(x, mod3, modrow, g[2:3], g[3:4], w_ffn_in_b[li], w_ffn_out_b[li], tm)
        grp["y"] = x.reshape(b, n, d)

    stack = lambda states, i: jnp.stack([s[i] for s in states], axis=1)
    return (groups["prompt"]["y"], groups["sample"]["y"],
            stack(ab_states, 0), stack(ab_states, 1), stack(ab_states, 2), stack(ab_states, 3),
            stack(c_states, 0), stack(c_states, 1))
```

```python
import functools
import math

import numpy as np
import jax
import jax.numpy as jnp
from jax import lax
from jax.experimental import pallas as pl
from jax.experimental.pallas import tpu as pltpu

F32 = jnp.float32
BF16 = jnp.bfloat16

D_MODEL = 1024
GRID_W = 64
ROPE_BASE = 10000.0
NORM_EPS = 1e-6

MLA_HEADS = 8
MLA_NOPE = 64
MLA_ROPE = 32
MLA_V = 64
MLA_Q_LORA = 384
MLA_KV_LORA = 256
GLA_HEADS = 4
GLA_DK = 128
GLA_DV = 128
GLA_GATE_RANK = 16
GLA_GATE_NORM = 16.0
GLA_CHUNK = 64
DIFF_HEADS = 8
DIFF_HEAD_DIM = 64
FFN_HIDDEN = 2816

LANES = 128
VMEM_LIMIT = 56 * 1024 * 1024

_OFF_QLAT = 0
_OFF_KVLAT = _OFF_QLAT + MLA_Q_LORA
_OFF_GQ = _OFF_KVLAT + MLA_KV_LORA
_OFF_GK = _OFF_GQ + GLA_HEADS * GLA_DK
_OFF_GV = _OFF_GK + GLA_HEADS * GLA_DK
_OFF_GOUT = _OFF_GV + GLA_HEADS * GLA_DV
_OFF_MISC = _OFF_GOUT + GLA_HEADS * GLA_DV
_AB_COLS = _OFF_MISC + LANES
_MISC_GATE = MLA_ROPE

LOG2E = math.log2(math.e)

TOKEN_TILE = 512
ATTN_Q_TILE = 1024
ATTN_K_TILE = 512
GLA_BLOCK = 256
FFN_SPLIT = 2


def _rms(x, g):
    var = jnp.mean(x * x, axis=-1, keepdims=True)
    return x * lax.rsqrt(var + NORM_EPS) * g


def _silu(x):
    return x * (1.0 / (1.0 + jnp.exp(-x)))


def _log_sigmoid(x):
    return -(jnp.maximum(-x, 0.0) + jnp.log1p(jnp.exp(-jnp.abs(x))))


def _rope(x, c, se, so):
    n = x.shape[-1]
    return x * c + pltpu.roll(x, n - 1, 1) * se + pltpu.roll(x, 1, 1) * so


def _dot(a, b):
    return jnp.dot(a, b, preferred_element_type=F32)


def _dot_nt(a, b):
    return lax.dot_general(a, b, (((1,), (1,)), ((), ())), preferred_element_type=F32)


def _dot_tn(a, b):
    return lax.dot_general(a, b, (((0,), (0,)), ((), ())), preferred_element_type=F32)


def _params(*sem):
    return pltpu.CompilerParams(dimension_semantics=sem, vmem_limit_bytes=VMEM_LIMIT)


def _full(shape):
    zeros = (0,) * len(shape)
    return pl.BlockSpec(shape, lambda *_: zeros)


def _rope_tables(n_tokens, rot_dim, lane_lo, period):
    rows = n_tokens // GRID_W
    row = np.repeat(np.arange(rows, dtype=np.float64), GRID_W)
    col = np.tile(np.arange(GRID_W, dtype=np.float64), rows)
    n_freq = rot_dim // 4
    inv = ROPE_BASE ** (-np.arange(n_freq, dtype=np.float64) / n_freq)
    ang = np.concatenate([row[:, None] * inv, col[:, None] * inv], axis=-1)
    cos, sin = np.cos(ang), np.sin(ang)
    c = np.ones((n_tokens, period))
    se = np.zeros((n_tokens, period))
    so = np.zeros((n_tokens, period))
    c[:, lane_lo:lane_lo + rot_dim] = np.repeat(cos, 2, axis=-1)
    se[:, lane_lo:lane_lo + rot_dim:2] = -sin
    so[:, lane_lo + 1:lane_lo + rot_dim:2] = sin
    reps = LANES // period
    return tuple(jnp.asarray(np.tile(t, (1, reps)), dtype=F32) for t in (c, se, so))


def _mod_kernel(c_ref, w_ref, b_ref, o_ref):
    a = _silu(c_ref[...]).astype(BF16)
    o_ref[0] = _dot(a, w_ref[0].astype(BF16)) + b_ref[0]


def _modulation(conds, w_mod, b_mod):
    depth, d, n = w_mod.shape
    rows = conds.shape[0]
    tn = 1536
    return pl.pallas_call(
        _mod_kernel,
        grid=(depth, n // tn),
        in_specs=[_full((rows, d)),
                  pl.BlockSpec((1, d, tn), lambda l, j: (l, 0, j)),
                  pl.BlockSpec((1, 1, tn), lambda l, j: (l, 0, j))],
        out_specs=pl.BlockSpec((1, rows, tn), lambda l, j: (l, 0, j)),
        out_shape=jax.ShapeDtypeStruct((depth, rows, n), F32),
        compiler_params=_params("arbitrary", "arbitrary"),
        name="modulation",
    )(conds, w_mod, b_mod.reshape(depth, 1, n))


def _ab_in_kernel(use_rope, *refs):
    (x_ref, mod_ref, g0_ref, win_ref, gq_ref, gkv_ref, wq_ref, wk_ref, wv_ref, e_ref) = refs[:10]
    refs = refs[10:]
    if use_rope:
        cq, seq, soq, cm, sem, som = refs[:6]
        refs = refs[6:]
    (q_out, kcat_out, v_out, ckv_out, misc_out, gq_out, gk_out, gv_out, gout_out) = refs

    mod = mod_ref[0]
    shift, scale = mod[:, 0:D_MODEL], mod[:, D_MODEL:2 * D_MODEL]
    h = _rms(x_ref[...], g0_ref[...]) * (1.0 + scale) + shift
    proj = _dot(h.astype(BF16), win_ref[...])

    gq_out[...] = (proj[:, _OFF_GQ:_OFF_GK] * (GLA_DK ** -0.5)).astype(BF16)
    gk_out[...] = proj[:, _OFF_GK:_OFF_GV].astype(BF16)
    gv_out[...] = proj[:, _OFF_GV:_OFF_GOUT].astype(BF16)
    gout_out[...] = proj[:, _OFF_GOUT:_OFF_MISC].astype(BF16)

    misc = proj[:, _OFF_MISC:_AB_COLS]
    if use_rope:
        misc = _rope(misc, cm[...], sem[...], som[...])
    misc_out[...] = misc

    ckv = _rms(proj[:, _OFF_KVLAT:_OFF_GQ], gkv_ref[...])
    ckv_out[...] = ckv
    ckv_b = ckv.astype(BF16)
    kcat = (_dot(ckv_b, wk_ref[...]) + _dot(misc.astype(BF16), e_ref[...])).astype(BF16)
    v = _dot(ckv_b, wv_ref[...]).astype(BF16)
    for hp in range(MLA_HEADS // 2):
        v_out[0, hp] = v[:, hp * LANES:(hp + 1) * LANES]

    qn = _rms(proj[:, _OFF_QLAT:_OFF_KVLAT], gq_ref[...]).astype(BF16)
    q = _dot(qn, wq_ref[...])
    sm_scale = (MLA_NOPE + MLA_ROPE) ** -0.5 * LOG2E
    for hd in range(MLA_HEADS):
        sl = slice(hd * LANES, (hd + 1) * LANES)
        kcat_out[0, hd] = kcat[:, sl]
        qh = q[:, sl]
        if use_rope:
            qh = _rope(qh, cq[...], seq[...], soq[...])
        q_out[0, hd] = (qh * sm_scale).astype(BF16)


def _ab_in(x, mod3, modrow, g0, w, rope, tm, seq_len):
    t = x.shape[0]
    nt = t // tm
    tok = lambda n: pl.BlockSpec((tm, n), lambda i: (i, 0))
    in_specs = [tok(D_MODEL),
                pl.BlockSpec((1, 1, 6 * D_MODEL), lambda i: (modrow(i), 0, 0)),
                _full((1, D_MODEL)), _full((D_MODEL, _AB_COLS)),
                _full((1, MLA_Q_LORA)), _full((1, MLA_KV_LORA)),
                _full((MLA_Q_LORA, MLA_HEADS * LANES)), _full((MLA_KV_LORA, MLA_HEADS * LANES)),
                _full((MLA_KV_LORA, MLA_HEADS * MLA_V)), _full((LANES, MLA_HEADS * LANES))]
    args = [x, mod3, g0, w["win"], w["g_q"], w["g_kv"], w["wq"], w["wk"], w["wv"], w["e"]]
    per = seq_len // tm
    if rope is not None:
        in_specs += [pl.BlockSpec((tm, LANES), lambda i: (i % per, 0))] * 6
        args += list(rope)
    nb = t // seq_len
    heads = lambda nh: pl.BlockSpec((1, nh, tm, LANES), lambda i: (i // per, 0, i % per, 0))
    head_shape = lambda nh: jax.ShapeDtypeStruct((nb, nh, seq_len, LANES), BF16)
    widths = [(MLA_KV_LORA, F32), (LANES, F32)] + [(GLA_HEADS * GLA_DK, BF16)] * 4
    return pl.pallas_call(
        functools.partial(_ab_in_kernel, rope is not None),
        grid=(nt,),
        in_specs=in_specs,
        out_specs=[heads(MLA_HEADS), heads(MLA_HEADS), heads(MLA_HEADS // 2)] + [tok(n) for n, _ in widths],
        out_shape=[head_shape(MLA_HEADS), head_shape(MLA_HEADS), head_shape(MLA_HEADS // 2)]
        + [jax.ShapeDtypeStruct((t, n), dt) for n, dt in widths],
        compiler_params=_params("arbitrary"),
        name="ab_in",
    )(*args)


def _ctx_kv_kernel(ckv_ref, kr_ref, wk_ref, wv_ref, e_ref, kcat_out, v_out):
    c = ckv_ref[...].astype(BF16)
    kr = kr_ref[...].astype(BF16)
    kcat = (_dot(c, wk_ref[...]) + _dot(kr, e_ref[...])).astype(BF16)
    v = _dot(c, wv_ref[...]).astype(BF16)
    for hd in range(MLA_HEADS):
        kcat_out[0, hd] = kcat[:, hd * LANES:(hd + 1) * LANES]
    for hp in range(MLA_HEADS // 2):
        v_out[0, hp] = v[:, hp * LANES:(hp + 1) * LANES]


def _ctx_kv(ckv, krope, w):
    b, n_ctx, _ = ckv.shape
    tok = lambda n: pl.BlockSpec((None, n_ctx, n), lambda i: (i, 0, 0))
    heads = lambda nh: pl.BlockSpec((1, nh, n_ctx, LANES), lambda i: (i, 0, 0, 0))
    return pl.pallas_call(
        _ctx_kv_kernel,
        grid=(b,),
        in_specs=[tok(MLA_KV_LORA), tok(MLA_ROPE),
                  _full((MLA_KV_LORA, MLA_HEADS * LANES)), _full((MLA_KV_LORA, MLA_HEADS * MLA_V)),
                  _full((MLA_ROPE, MLA_HEADS * LANES))],
        out_specs=[heads(MLA_HEADS), heads(MLA_HEADS // 2)],
        out_shape=[jax.ShapeDtypeStruct((b, MLA_HEADS, n_ctx, LANES), BF16),
                   jax.ShapeDtypeStruct((b, MLA_HEADS // 2, n_ctx, LANES), BF16)],
        compiler_params=_params("arbitrary"),
        name="mla_ctx_kv",
    )(ckv, krope, w["wk"], w["wv"], w["e"][:MLA_ROPE])


_NEG = -1e30


def _two_map_flash(q0, q1, sources):
    state = [None, None]
    for k0_ref, k1_ref, v_ref in sources:
        rows = v_ref.shape[0]
        tk = min(ATTN_K_TILE, rows)
        ones = jnp.ones((tk, LANES), BF16)
        for c in range(rows // tk):
            r = slice(c * tk, (c + 1) * tk)
            v = jnp.concatenate([v_ref[r, :], ones], axis=1)
            for j, (q, k_ref) in enumerate(((q0, k0_ref), (q1, k1_ref))):
                s = _dot_nt(q, k_ref[r, :])
                m_blk = jnp.max(s, axis=-1, keepdims=True)
                if state[j] is None:
                    p = jnp.exp2(s - m_blk)
                    state[j] = (m_blk, _dot(p.astype(BF16), v))
                else:
                    m, acc = state[j]
                    m_new = jnp.maximum(m, m_blk)
                    p = jnp.exp2(s - m_new)
                    state[j] = (m_new, jnp.exp2(m - m_new) * acc + _dot(p.astype(BF16), v))
    return tuple((acc[:, LANES:], acc[:, :LANES]) for _, acc in state)


def _mla_attn_kernel(has_ctx, *refs):
    if has_ctx:
        q0_ref, q1_ref, k0n, k1n, vn, k0c, k1c, vc, o_ref = refs
        sources = [(k0c, k1c, vc), (k0n, k1n, vn)]
    else:
        q0_ref, q1_ref, k0n, k1n, vn, o_ref = refs
        sources = [(k0n, k1n, vn)]
    (l0, acc0), (l1, acc1) = _two_map_flash(q0_ref[...], q1_ref[...], sources)
    lane = lax.broadcasted_iota(jnp.int32, acc0.shape, 1)
    o_ref[...] = jnp.where(lane < MLA_V, acc0 * (1.0 / l0), acc1 * (1.0 / l1)).astype(BF16)


def _mla_attn(q, k_new, v_new, ctx, tq):
    b, nh, n, _ = q.shape
    slab = lambda rows, f: pl.BlockSpec((None, None, rows, LANES), f)
    even = lambda bi, p, qi: (bi, 2 * p, 0, 0)
    odd = lambda bi, p, qi: (bi, 2 * p + 1, 0, 0)
    pair = lambda bi, p, qi: (bi, p, 0, 0)
    in_specs = [slab(tq, lambda bi, p, qi: (bi, 2 * p, qi, 0)), slab(tq, lambda bi, p, qi: (bi, 2 * p + 1, qi, 0)),
                slab(n, even), slab(n, odd), slab(n, pair)]
    args = [q, q, k_new, k_new, v_new]
    if ctx is not None:
        nc = ctx[0].shape[2]
        in_specs += [slab(nc, even), slab(nc, odd), slab(nc, pair)]
        args += [ctx[0], ctx[0], ctx[1]]
    return pl.pallas_call(
        functools.partial(_mla_attn_kernel, ctx is not None),
        grid=(b, nh // 2, n // tq),
        in_specs=in_specs,
        out_specs=pl.BlockSpec((None, tq, LANES), lambda bi, p, qi: (bi, qi, p)),
        out_shape=jax.ShapeDtypeStruct((b, n, (nh // 2) * LANES), BF16),
        compiler_params=_params("arbitrary", "arbitrary", "arbitrary"),
        name="mla_attn",
    )(*args)


def _gla_kernel(has_s0, *refs):
    fwd_in, bwd_in = refs[0:4], refs[4:8]
    wg_ref, bg_ref, tf_ref, tb_ref = refs[8:12]
    refs = refs[12:]
    if has_s0:
        s0f_ref, s0b_ref = refs[:2]
        refs = refs[2:]
    of_ref, ob_ref, sf_ref, sb_ref = refs
    tb_rows = of_ref.shape[1]
    nch = tb_rows // GLA_CHUNK

    @pl.when(pl.program_id(1) == 0)
    def _():
        if has_s0:
            sf_ref[...] = s0f_ref[...]
            sb_ref[...] = s0b_ref[...]
        else:
            sf_ref[...] = jnp.zeros(sf_ref.shape, F32)
            sb_ref[...] = jnp.zeros(sb_ref.shape, F32)

    for d, ((q_ref, k_ref, v_ref, m_ref), t_ref, o_ref, s_ref) in enumerate(
            ((fwd_in, tf_ref, of_ref, sf_ref), (bwd_in, tb_ref, ob_ref, sb_ref))):
        t = t_ref[...]
        row = lax.broadcasted_iota(jnp.int32, t.shape, 0)
        col = lax.broadcasted_iota(jnp.int32, t.shape, 1)
        chunk_lo = (row // GLA_CHUNK) * GLA_CHUNK
        if d == 0:
            tmask = (col <= row) & (col >= chunk_lo)
        else:
            tmask = (col >= row) & (col < chunk_lo + GLA_CHUNK)
        pre = _dot(m_ref[0].astype(BF16), wg_ref[d]) + bg_ref[d]
        g = _log_sigmoid(pre) * (1.0 / GLA_GATE_NORM)
        g_hi = g.astype(BF16)
        g_lo = (g - g_hi.astype(F32)).astype(BF16)
        cum = _dot(t, g_hi) + _dot(t, g_lo)
        for hd in range(GLA_HEADS):
            sl = slice(hd * LANES, (hd + 1) * LANES)
            cs = cum[:, sl]
            q = q_ref[0, :, sl].astype(F32)
            k = k_ref[0, :, sl].astype(F32)
            v = v_ref[0, :, sl]
            q_in = (q * jnp.exp(cs)).astype(BF16)
            k_in = (k * jnp.exp(-cs)).astype(BF16)
            a = jnp.where(tmask, _dot_nt(q_in, k_in), 0.0)
            o_intra = _dot(a.astype(BF16), v)
            s_t = s_ref[0, hd]
            order = range(nch) if d == 0 else range(nch - 1, -1, -1)
            for n in order:
                r = slice(n * GLA_CHUNK, (n + 1) * GLA_CHUNK)
                edge = (n + 1) * GLA_CHUNK - 1 if d == 0 else n * GLA_CHUNK
                tot = cs[edge:edge + 1, :]
                o_ref[0, r, sl] = o_intra[r] + _dot_nt(q_in[r], s_t.astype(BF16))
                k_dec = (k[r] * jnp.exp(tot - cs[r])).astype(BF16)
                s_t = s_t * jnp.exp(tot) + _dot_tn(v[r], k_dec)
            s_ref[0, hd] = s_t


def _gla_masks(tb_rows):
    idx = np.arange(tb_rows)
    same = (idx[:, None] // GLA_CHUNK) == (idx[None, :] // GLA_CHUNK)
    fwd = same & (idx[None, :] <= idx[:, None])
    bwd = same & (idx[None, :] >= idx[:, None])
    return jnp.asarray(fwd, dtype=BF16), jnp.asarray(bwd, dtype=BF16)


def _gla(gq, gk, gv, misc, w, s0, tb_rows):
    b, n, width = gq.shape
    nb = n // tb_rows
    fw = lambda cols: pl.BlockSpec((1, tb_rows, cols), lambda bi, i: (bi, i, 0))
    bw = lambda cols: pl.BlockSpec((1, tb_rows, cols), lambda bi, i: (bi, nb - 1 - i, 0))
    st = pl.BlockSpec((1, GLA_HEADS, GLA_DV, GLA_DK), lambda bi, i: (bi, 0, 0, 0))
    tf, tb = _gla_masks(tb_rows)
    in_specs = ([fw(width)] * 3 + [fw(LANES)] + [bw(width)] * 3 + [bw(LANES)]
                + [_full((2, LANES, width)), _full((2, 1, width)),
                   _full((tb_rows, tb_rows)), _full((tb_rows, tb_rows))])
    args = [gq, gk, gv, misc, gq, gk, gv, misc, w["wg"], w["bg"], tf, tb]
    if s0 is not None:
        in_specs += [st, st]
        args += list(s0)
    return pl.pallas_call(
        functools.partial(_gla_kernel, s0 is not None),
        grid=(b, nb),
        in_specs=in_specs,
        out_specs=[fw(width), bw(width), st, st],
        out_shape=[jax.ShapeDtypeStruct((b, n, width), F32)] * 2
        + [jax.ShapeDtypeStruct((b, GLA_HEADS, GLA_DV, GLA_DK), F32)] * 2,
        compiler_params=_params("arbitrary", "arbitrary"),
        name="gla",
    )(*args)


def _layer_tail_kernel(has_gla, *refs):
    if has_gla:
        mix_ref, of_ref, ob_ref, gout_ref, ggo_ref = refs[:5]
        refs = refs[5:]
    else:
        mix_ref = refs[0]
        refs = refs[1:]
    (wmix_ref, x_ref, mod_ref, g_ref, wg_ref, wu_ref, wo_ref, o_ref, h_scr, x1_scr, acc_scr) = refs
    j = pl.program_id(1)

    @pl.when(j == 0)
    def _():
        mod = mod_ref[0]
        if has_gla:
            o = of_ref[...] + ob_ref[...]
            parts = []
            for hd in range(GLA_HEADS):
                sl = slice(hd * LANES, (hd + 1) * LANES)
                gate = gout_ref[:, sl].astype(F32)
                parts.append((_rms(o[:, sl], ggo_ref[...]) * _silu(gate)).astype(BF16))
            n_mla = mix_ref.shape[1]
            out = (_dot(mix_ref[...], wmix_ref[0:n_mla, :])
                   + _dot(jnp.concatenate(parts, axis=-1), wmix_ref[n_mla:, :]))
        else:
            out = _dot(mix_ref[...], wmix_ref[...])
        gate1 = mod[:, 2 * D_MODEL:3 * D_MODEL]
        x1 = x_ref[...] + gate1 * _rms(out, g_ref[1:2, :])
        x1_scr[...] = x1
        shift, scale = mod[:, 3 * D_MODEL:4 * D_MODEL], mod[:, 4 * D_MODEL:5 * D_MODEL]
        h_scr[...] = (_rms(x1, g_ref[2:3, :]) * (1.0 + scale) + shift).astype(BF16)

    hb = h_scr[...]
    act = (_silu(_dot(hb, wg_ref[...])) * _dot(hb, wu_ref[...])).astype(BF16)
    part = _dot(act, wo_ref[...])

    @pl.when(j == 0)
    def _():
        acc_scr[...] = part

    @pl.when(j > 0)
    def _():
        acc_scr[...] += part

    @pl.when(j == pl.num_programs(1) - 1)
    def _():
        gate2 = mod_ref[0][:, 5 * D_MODEL:6 * D_MODEL]
        o_ref[...] = x1_scr[...] + gate2 * _rms(acc_scr[...], g_ref[3:4, :])


def _layer_tail(mix, gla, w_mix, x, mod3, modrow, g, w_in, w_out, tm):
    t = x.shape[0]
    th = FFN_HIDDEN // FFN_SPLIT
    tok = lambda n: pl.BlockSpec((tm, n), lambda i, j: (i, 0))
    const = lambda shape: pl.BlockSpec(shape, lambda i, j: (0,) * len(shape))
    in_specs = [tok(mix.shape[1])]
    args = [mix]
    if gla is not None:
        o_f, o_b, gout, g_gla = gla
        in_specs += [tok(o_f.shape[1])] * 3 + [const((1, GLA_DV))]
        args += [o_f, o_b, gout, g_gla]
    in_specs += [const(w_mix.shape), tok(D_MODEL),
                 pl.BlockSpec((1, 1, 6 * D_MODEL), lambda i, j: (modrow(i), 0, 0)), const(g.shape),
                 pl.BlockSpec((D_MODEL, th), lambda i, j: (0, j)),
                 pl.BlockSpec((D_MODEL, th), lambda i, j: (0, j + FFN_SPLIT)),
                 pl.BlockSpec((th, D_MODEL), lambda i, j: (j, 0))]
    args += [w_mix, x, mod3, g, w_in, w_in, w_out]
    return pl.pallas_call(
        functools.partial(_layer_tail_kernel, gla is not None),
        grid=(t // tm, FFN_SPLIT),
        in_specs=in_specs,
        out_specs=tok(D_MODEL),
        out_shape=jax.ShapeDtypeStruct((t, D_MODEL), F32),
        scratch_shapes=[pltpu.VMEM((tm, D_MODEL), BF16), pltpu.VMEM((tm, D_MODEL), F32),
                        pltpu.VMEM((tm, D_MODEL), F32)],
        compiler_params=_params("arbitrary", "arbitrary"),
        name="layer_tail",
    )(*args)


def _c_in_kernel(use_rope, want_f32, *refs):
    x_ref, mod_ref, g0_ref, w_ref = refs[:4]
    refs = refs[4:]
    if use_rope:
        c_ref, se_ref, so_ref = refs[:3]
        refs = refs[3:]
    q_out, k_out, v_out = refs[:3]
    mod = mod_ref[0]
    shift, scale = mod[:, 0:D_MODEL], mod[:, D_MODEL:2 * D_MODEL]
    h = _rms(x_ref[...], g0_ref[...]) * (1.0 + scale) + shift
    proj = _dot(h.astype(BF16), w_ref[...])
    width = DIFF_HEADS * LANES
    sm_scale = DIFF_HEAD_DIM ** -0.5 * LOG2E
    for hd in range(DIFF_HEADS):
        sl = slice(hd * LANES, (hd + 1) * LANES)
        qh = proj[:, hd * LANES:(hd + 1) * LANES]
        kh = proj[:, width + hd * LANES:width + (hd + 1) * LANES]
        vh = proj[:, 2 * width + hd * LANES:2 * width + (hd + 1) * LANES]
        if use_rope:
            qh = _rope(qh, c_ref[...], se_ref[...], so_ref[...])
            kh = _rope(kh, c_ref[...], se_ref[...], so_ref[...])
        q_out[0, hd] = (qh * sm_scale).astype(BF16)
        k_out[0, hd] = kh.astype(BF16)
        v_out[0, hd] = vh.astype(BF16)
        if want_f32:
            refs[3][:, sl] = kh
            refs[4][:, sl] = vh


def _c_in(x, mod3, modrow, g0, w_qkv, rope, want_f32, tm, seq_len):
    t = x.shape[0]
    width = DIFF_HEADS * LANES
    tok = lambda n: pl.BlockSpec((tm, n), lambda i: (i, 0))
    in_specs = [tok(D_MODEL), pl.BlockSpec((1, 1, 6 * D_MODEL), lambda i: (modrow(i), 0, 0)),
                _full((1, D_MODEL)), _full((D_MODEL, 3 * width))]
    args = [x, mod3, g0, w_qkv]
    per = seq_len // tm
    nb = t // seq_len
    if rope is not None:
        in_specs += [pl.BlockSpec((tm, LANES), lambda i: (i % per, 0))] * 3
        args += list(rope)
    heads = pl.BlockSpec((1, DIFF_HEADS, tm, LANES), lambda i: (i // per, 0, i % per, 0))
    head_shape = jax.ShapeDtypeStruct((nb, DIFF_HEADS, seq_len, LANES), BF16)
    n_f32 = 2 if want_f32 else 0
    return pl.pallas_call(
        functools.partial(_c_in_kernel, rope is not None, want_f32),
        grid=(t // tm,),
        in_specs=in_specs,
        out_specs=[heads] * 3 + [tok(width)] * n_f32,
        out_shape=[head_shape] * 3 + [jax.ShapeDtypeStruct((t, width), F32)] * n_f32,
        compiler_params=_params("arbitrary"),
        name="c_in",
    )(*args)


def _diff_attn_kernel(has_ctx, lam_init, *refs):
    if has_ctx:
        q_ref, kn, vn, kc, vc, lam_ref, g_ref, o_ref = refs
        sources = [(kc, kc, vc), (kn, kn, vn)]
    else:
        q_ref, kn, vn, lam_ref, g_ref, o_ref = refs
        sources = [(kn, kn, vn)]
    lp = lam_ref[...]
    lam = (jnp.exp(jnp.sum(lp[0:1] * lp[1:2], axis=-1, keepdims=True))
           - jnp.exp(jnp.sum(lp[2:3] * lp[3:4], axis=-1, keepdims=True)) + lam_init)
    qh = q_ref[...]
    lane = lax.broadcasted_iota(jnp.int32, qh.shape, 1)
    zero = jnp.zeros_like(qh)
    q0 = jnp.where(lane < DIFF_HEAD_DIM, qh, zero)
    q1 = jnp.where(lane < DIFF_HEAD_DIM, zero, qh)
    (l0, acc0), (l1, acc1) = _two_map_flash(q0, q1, sources)
    o = acc0 * (1.0 / l0) - lam * (acc1 * (1.0 / l1))
    o_ref[...] = (_rms(o, g_ref[...]) * (1.0 - lam_init)).astype(BF16)


def _diff_attn(q, k_new, v_new, ctx, lam_p, g_out, lam_init, tq):
    b, nh, n, _ = q.shape
    slab = lambda rows, f: pl.BlockSpec((None, None, rows, LANES), f)
    whole = lambda bi, h, qi: (bi, h, 0, 0)
    in_specs = [slab(tq, lambda bi, h, qi: (bi, h, qi, 0)), slab(n, whole), slab(n, whole)]
    args = [q, k_new, v_new]
    if ctx is not None:
        in_specs += [slab(ctx[0].shape[2], whole)] * 2
        args += list(ctx)
    in_specs += [_full(lam_p.shape), _full(g_out.shape)]
    args += [lam_p, g_out]
    return pl.pallas_call(
        functools.partial(_diff_attn_kernel, ctx is not None, lam_init),
        grid=(b, nh, n // tq),
        in_specs=in_specs,
        out_specs=pl.BlockSpec((None, tq, LANES), lambda bi, h, qi: (bi, qi, h)),
        out_shape=jax.ShapeDtypeStruct((b, n, nh * LANES), BF16),
        compiler_params=_params("arbitrary", "arbitrary", "arbitrary"),
        name="diff_attn",
    )(*args)


def _prep_ab_weights(ai, ab_w_in, mla_g_q, mla_g_kv, mla_w_uq, mla_w_ukv, gla_w_gate_up, gla_b_gate):
    w = ab_w_in[ai]
    d = w.shape[0]
    o = np.cumsum([0, MLA_Q_LORA, MLA_KV_LORA, MLA_ROPE, GLA_HEADS * GLA_DK, GLA_HEADS * GLA_DK,
                   GLA_HEADS * GLA_DV, 2 * GLA_GATE_RANK, GLA_HEADS * GLA_DV])
    q_lat, kv_lat, k_rope, gq, gk, gv, ggate, gout = (w[:, o[i]:o[i + 1]] for i in range(8))
    pad = jnp.zeros((d, LANES - MLA_ROPE - 2 * GLA_GATE_RANK), w.dtype)
    win = jnp.concatenate([q_lat, kv_lat, gq, gk, gv, gout, k_rope, ggate, pad], axis=1).astype(BF16)

    head_pad = LANES - MLA_NOPE - MLA_ROPE
    wq = mla_w_uq[ai].reshape(MLA_Q_LORA, MLA_HEADS, MLA_NOPE + MLA_ROPE)
    wq = jnp.pad(wq, ((0, 0), (0, 0), (0, head_pad))).reshape(MLA_Q_LORA, MLA_HEADS * LANES).astype(BF16)
    wkv = mla_w_ukv[ai].reshape(MLA_KV_LORA, MLA_HEADS, MLA_NOPE + MLA_V)
    wk = jnp.pad(wkv[:, :, :MLA_NOPE], ((0, 0), (0, 0), (0, LANES - MLA_NOPE)))
    wk = wk.reshape(MLA_KV_LORA, MLA_HEADS * LANES).astype(BF16)
    wv = wkv[:, :, MLA_NOPE:].reshape(MLA_KV_LORA, MLA_HEADS * MLA_V).astype(BF16)

    e = np.zeros((LANES, MLA_HEADS * LANES), np.float32)
    for hd in range(MLA_HEADS):
        e[np.arange(MLA_ROPE), hd * LANES + MLA_NOPE + np.arange(MLA_ROPE)] = 1.0

    wg = jnp.zeros((2, LANES, GLA_HEADS * GLA_DK), F32)
    for dr in range(2):
        lo = _MISC_GATE + dr * GLA_GATE_RANK
        wg = wg.at[dr, lo:lo + GLA_GATE_RANK, :].set(gla_w_gate_up[ai, dr])
    return dict(win=win, g_q=mla_g_q[ai][None, :], g_kv=mla_g_kv[ai][None, :], wq=wq, wk=wk, wv=wv,
                e=jnp.asarray(e, dtype=BF16), wg=wg.astype(BF16),
                bg=gla_b_gate[ai].reshape(2, 1, GLA_HEADS * GLA_DK))


def _diff_lambda_init(layer):
    return 0.8 - 0.6 * math.exp(-0.3 * layer)


def kernel(x_prompt, x_sample, cache_mla_ckv, cache_mla_krope, state_gla_fwd, state_gla_bwd,
           cache_diff_k, cache_diff_v, c, c_ctx, w_mod, b_mod, g_norm, ab_w_in, mla_g_q, mla_g_kv,
           mla_w_uq, mla_w_ukv, gla_w_gate_up, gla_b_gate, gla_g_out, ab_w_out, c_w_qkv, diff_lambda,
           diff_g_out, c_w_out, w_ffn_in, w_ffn_out):
    depth = w_mod.shape[0]
    bp, lp, d = x_prompt.shape
    bs, ls, _ = x_sample.shape
    n_ctx = cache_mla_ckv.shape[2]

    n_cond = 1 + bs
    rows = -(-n_cond // 8) * 8
    conds = jnp.concatenate([c_ctx[None, :], c, jnp.zeros((rows - n_cond, d), F32)], axis=0)
    mod3 = _modulation(conds, w_mod, b_mod).reshape(depth * rows, 1, 6 * d)

    per_s = ls // min(TOKEN_TILE, ls)
    rope_q = _rope_tables(ls, MLA_ROPE, MLA_NOPE, LANES)
    rope_misc = _rope_tables(ls, MLA_ROPE, 0, LANES)
    rope_diff = _rope_tables(ls, DIFF_HEAD_DIM, 0, DIFF_HEAD_DIM)

    groups = {
        "prompt": dict(x=x_prompt.reshape(bp * lp, d), b=bp, n=lp, rope=False,
                       modrow=lambda li: (lambda i: li * rows)),
        "sample": dict(x=x_sample.reshape(bs * ls, d), b=bs, n=ls, rope=True,
                       modrow=lambda li: (lambda i: li * rows + 1 + i // per_s)),
    }
    ab_states, c_states = [], []

    ab_w = [_prep_ab_weights(ai, ab_w_in, mla_g_q, mla_g_kv, mla_w_uq, mla_w_ukv, gla_w_gate_up, gla_b_gate)
            for ai in range(ab_w_in.shape[0])]
    ab_w_out_b = ab_w_out.astype(BF16)
    c_w_qkv_b = c_w_qkv.astype(BF16)
    c_w_out_b = c_w_out.astype(BF16)
    w_ffn_in_b = w_ffn_in.astype(BF16)
    w_ffn_out_b = w_ffn_out.astype(BF16)

    for name, grp in groups.items():
        x, b, n = grp["x"], grp["b"], grp["n"]
        is_sample = grp["rope"]
        tm = min(TOKEN_TILE, n)
        for li in range(depth):
            modrow = grp["modrow"](li)
            g = g_norm[li]
            if li % 2 == 0:
                ai = li // 2
                w = ab_w[ai]
                rope = rope_q + rope_misc if is_sample else None
                q, kcat, v, ckv, misc, gq, gk, gv, gout = _ab_in(x, mod3, modrow, g[0:1], w, rope, tm, n)
                sh = lambda a: a.reshape(b, n, a.shape[-1])
                if is_sample:
                    ctx = _ctx_kv(cache_mla_ckv[:, ai], cache_mla_krope[:, ai], w)
                    s0 = (jnp.swapaxes(state_gla_fwd[:, ai], -1, -2), jnp.swapaxes(state_gla_bwd[:, ai], -1, -2))
                else:
                    ctx, s0 = None, None
                mla_o = _mla_attn(q, kcat, v, ctx, min(ATTN_Q_TILE, n))
                o_f, o_b, s_f, s_b = _gla(sh(gq), sh(gk), sh(gv), sh(misc), w, s0, min(GLA_BLOCK, n))
                if not is_sample:
                    ab_states.append((ckv.reshape(b, n, MLA_KV_LORA), misc[:, :MLA_ROPE].reshape(b, n, MLA_ROPE),
                                      jnp.swapaxes(s_f, -1, -2), jnp.swapaxes(s_b, -1, -2)))
                mix = mla_o.reshape(b * n, -1)
                gla = (o_f.reshape(b * n, -1), o_b.reshape(b * n, -1), gout, gla_g_out[ai][None, :])
                w_mix = ab_w_out_b[ai]
            else:
                ci = li // 2
                rope = rope_diff if is_sample else None
                outs = _c_in(x, mod3, modrow, g[0:1], c_w_qkv_b[ci], rope, not is_sample, tm, n)
                q, k, v = outs[:3]
                if is_sample:
                    head_major = lambda a: jnp.swapaxes(a.reshape(bs, n_ctx, DIFF_HEADS, LANES), 1, 2).astype(BF16)
                    ctx = (head_major(cache_diff_k[:, ci]), head_major(cache_diff_v[:, ci]))
                else:
                    ctx = None
                    c_states.append((outs[3].reshape(b, n, DIFF_HEADS, 2, DIFF_HEAD_DIM),
                                     outs[4].reshape(b, n, DIFF_HEADS, 2 * DIFF_HEAD_DIM)))
                o = _diff_attn(q, k, v, ctx, diff_lambda[ci], diff_g_out[ci][None, :],
                               _diff_lambda_init(li), min(ATTN_Q_TILE, n))
                mix, gla, w_mix = o.reshape(b * n, -1), None, c_w_out_b[ci]
            x = _layer_tail(mix, gla, w_mix, x, mod3, modrow, g, w_ffn_in_b[li], w_ffn_out_b[li], tm)
        grp["y"] = x.reshape(b, n, d)

    stack = lambda states, i: jnp.stack([s[i] for s in states], axis=1)
    return (groups["prompt"]["y"], groups["sample"]["y"],
            stack(ab_states, 0), stack(ab_states, 1), stack(ab_states, 2), stack(ab_states, 3),
            stack(c_states, 0), stack(c_states, 1))
```

```python
import functools
import math

import numpy as np
import jax
import jax.numpy as jnp
from jax import lax
from jax.experimental import pallas as pl
from jax.experimental.pallas import tpu as pltpu

F32 = jnp.float32
BF16 = jnp.bfloat16

D_MODEL = 1024
GRID_W = 64
ROPE_BASE = 10000.0
NORM_EPS = 1e-6

MLA_HEADS = 8
MLA_NOPE = 64
MLA_ROPE = 32
MLA_V = 64
MLA_Q_LORA = 384
MLA_KV_LORA = 256
GLA_HEADS = 4
GLA_DK = 128
GLA_DV = 128
GLA_GATE_RANK = 16
GLA_GATE_NORM = 16.0
GLA_CHUNK = 64
DIFF_HEADS = 8
DIFF_HEAD_DIM = 64
FFN_HIDDEN = 2816

LANES = 128
VMEM_LIMIT = 56 * 1024 * 1024

_OFF_QLAT = 0
_OFF_KVLAT = _OFF_QLAT + MLA_Q_LORA
_OFF_GQ = _OFF_KVLAT + MLA_KV_LORA
_OFF_GK = _OFF_GQ + GLA_HEADS * GLA_DK
_OFF_GV = _OFF_GK + GLA_HEADS * GLA_DK
_OFF_GOUT = _OFF_GV + GLA_HEADS * GLA_DV
_OFF_MISC = _OFF_GOUT + GLA_HEADS * GLA_DV
_AB_COLS = _OFF_MISC + LANES
_MISC_GATE = MLA_ROPE

LOG2E = math.log2(math.e)

TOKEN_TILE = 512
ATTN_Q_TILE = 1024
ATTN_K_TILE = 512
ATTN_STEP_ROWS = 2048
GLA_BLOCK = 256
GLA_BATCH = 2
FFN_SPLIT = 2


def _rms(x, g):
    var = jnp.mean(x * x, axis=-1, keepdims=True)
    return x * lax.rsqrt(var + NORM_EPS) * g


def _silu(x):
    return x * (1.0 / (1.0 + jnp.exp(-x)))


def _log_sigmoid(x):
    return -(jnp.maximum(-x, 0.0) + jnp.log1p(jnp.exp(-jnp.abs(x))))


def _rope(x, c, se, so):
    n = x.shape[-1]
    return x * c + pltpu.roll(x, n - 1, 1) * se + pltpu.roll(x, 1, 1) * so


def _dot(a, b):
    return jnp.dot(a, b, preferred_element_type=F32)


def _dot_nt(a, b):
    return lax.dot_general(a, b, (((1,), (1,)), ((), ())), preferred_element_type=F32)


def _dot_tn(a, b):
    return lax.dot_general(a, b, (((0,), (0,)), ((), ())), preferred_element_type=F32)


def _params(*sem):
    return pltpu.CompilerParams(dimension_semantics=sem, vmem_limit_bytes=VMEM_LIMIT)


def _full(shape):
    zeros = (0,) * len(shape)
    return pl.BlockSpec(shape, lambda *_: zeros)


def _rope_tables(n_tokens, rot_dim, lane_lo, period):
    rows = n_tokens // GRID_W
    row = np.repeat(np.arange(rows, dtype=np.float64), GRID_W)
    col = np.tile(np.arange(GRID_W, dtype=np.float64), rows)
    n_freq = rot_dim // 4
    inv = ROPE_BASE ** (-np.arange(n_freq, dtype=np.float64) / n_freq)
    ang = np.concatenate([row[:, None] * inv, col[:, None] * inv], axis=-1)
    cos, sin = np.cos(ang), np.sin(ang)
    c = np.ones((n_tokens, period))
    se = np.zeros((n_tokens, period))
    so = np.zeros((n_tokens, period))
    c[:, lane_lo:lane_lo + rot_dim] = np.repeat(cos, 2, axis=-1)
    se[:, lane_lo:lane_lo + rot_dim:2] = -sin
    so[:, lane_lo + 1:lane_lo + rot_dim:2] = sin
    reps = LANES // period
    return tuple(jnp.asarray(np.tile(t, (1, reps)), dtype=F32) for t in (c, se, so))


def _mod_kernel(c_ref, w_ref, b_ref, o_ref):
    a = _silu(c_ref[...]).astype(BF16)
    o_ref[0] = _dot(a, w_ref[0].astype(BF16)) + b_ref[0]


def _modulation(conds, w_mod, b_mod):
    depth, d, n = w_mod.shape
    rows = conds.shape[0]
    tn = 1536
    return pl.pallas_call(
        _mod_kernel,
        grid=(depth, n // tn),
        in_specs=[_full((rows, d)),
                  pl.BlockSpec((1, d, tn), lambda l, j: (l, 0, j)),
                  pl.BlockSpec((1, 1, tn), lambda l, j: (l, 0, j))],
        out_specs=pl.BlockSpec((1, rows, tn), lambda l, j: (l, 0, j)),
        out_shape=jax.ShapeDtypeStruct((depth, rows, n), F32),
        compiler_params=_params("arbitrary", "arbitrary"),
        name="modulation",
    )(conds, w_mod, b_mod.reshape(depth, 1, n))


def _ab_in_kernel(use_rope, *refs):
    (x_ref, mod_ref, g0_ref, win_ref, gq_ref, gkv_ref, wq_ref, wk_ref, wv_ref, e_ref) = refs[:10]
    refs = refs[10:]
    if use_rope:
        cq, seq, soq, cm, sem, som = refs[:6]
        refs = refs[6:]
    (q_out, kcat_out, v_out, ckv_out, misc_out, gq_out, gk_out, gv_out, gout_out) = refs

    mod = mod_ref[0]
    shift, scale = mod[:, 0:D_MODEL], mod[:, D_MODEL:2 * D_MODEL]
    h = _rms(x_ref[...], g0_ref[...]) * (1.0 + scale) + shift
    proj = _dot(h.astype(BF16), win_ref[...])

    gq_out[...] = (proj[:, _OFF_GQ:_OFF_GK] * (GLA_DK ** -0.5)).astype(BF16)
    gk_out[...] = proj[:, _OFF_GK:_OFF_GV].astype(BF16)
    gv_out[...] = proj[:, _OFF_GV:_OFF_GOUT].astype(BF16)
    gout_out[...] = proj[:, _OFF_GOUT:_OFF_MISC].astype(BF16)

    misc = proj[:, _OFF_MISC:_AB_COLS]
    if use_rope:
        misc = _rope(misc, cm[...], sem[...], som[...])
    misc_out[...] = misc

    ckv = _rms(proj[:, _OFF_KVLAT:_OFF_GQ], gkv_ref[...])
    ckv_out[...] = ckv
    ckv_b = ckv.astype(BF16)
    kcat = (_dot(ckv_b, wk_ref[...]) + _dot(misc.astype(BF16), e_ref[...])).astype(BF16)
    v = _dot(ckv_b, wv_ref[...]).astype(BF16)
    for hp in range(MLA_HEADS // 2):
        v_out[0, hp] = v[:, hp * LANES:(hp + 1) * LANES]

    qn = _rms(proj[:, _OFF_QLAT:_OFF_KVLAT], gq_ref[...]).astype(BF16)
    q = _dot(qn, wq_ref[...])
    sm_scale = (MLA_NOPE + MLA_ROPE) ** -0.5 * LOG2E
    for hd in range(MLA_HEADS):
        sl = slice(hd * LANES, (hd + 1) * LANES)
        kcat_out[0, hd] = kcat[:, sl]
        qh = q[:, sl]
        if use_rope:
            qh = _rope(qh, cq[...], seq[...], soq[...])
        q_out[0, hd] = (qh * sm_scale).astype(BF16)


def _ab_in(x, mod3, modrow, g0, w, rope, tm, seq_len):
    t = x.shape[0]
    nt = t // tm
    tok = lambda n: pl.BlockSpec((tm, n), lambda i: (i, 0))
    in_specs = [tok(D_MODEL),
                pl.BlockSpec((1, 1, 6 * D_MODEL), lambda i: (modrow(i), 0, 0)),
                _full((1, D_MODEL)), _full((D_MODEL, _AB_COLS)),
                _full((1, MLA_Q_LORA)), _full((1, MLA_KV_LORA)),
                _full((MLA_Q_LORA, MLA_HEADS * LANES)), _full((MLA_KV_LORA, MLA_HEADS * LANES)),
                _full((MLA_KV_LORA, MLA_HEADS * MLA_V)), _full((LANES, MLA_HEADS * LANES))]
    args = [x, mod3, g0, w["win"], w["g_q"], w["g_kv"], w["wq"], w["wk"], w["wv"], w["e"]]
    per = seq_len // tm
    if rope is not None:
        in_specs += [pl.BlockSpec((tm, LANES), lambda i: (i % per, 0))] * 6
        args += list(rope)
    nb = t // seq_len
    heads = lambda nh: pl.BlockSpec((1, nh, tm, LANES), lambda i: (i // per, 0, i % per, 0))
    head_shape = lambda nh: jax.ShapeDtypeStruct((nb, nh, seq_len, LANES), BF16)
    widths = [(MLA_KV_LORA, F32), (LANES, F32)] + [(GLA_HEADS * GLA_DK, BF16)] * 4
    return pl.pallas_call(
        functools.partial(_ab_in_kernel, rope is not None),
        grid=(nt,),
        in_specs=in_specs,
        out_specs=[heads(MLA_HEADS), heads(MLA_HEADS), heads(MLA_HEADS // 2)] + [tok(n) for n, _ in widths],
        out_shape=[head_shape(MLA_HEADS), head_shape(MLA_HEADS), head_shape(MLA_HEADS // 2)]
        + [jax.ShapeDtypeStruct((t, n), dt) for n, dt in widths],
        compiler_params=_params("arbitrary"),
        name="ab_in",
    )(*args)


def _ctx_kv_kernel(ckv_ref, kr_ref, wk_ref, wv_ref, e_ref, kcat_out, v_out):
    c = ckv_ref[...].astype(BF16)
    kr = kr_ref[...].astype(BF16)
    kcat = (_dot(c, wk_ref[...]) + _dot(kr, e_ref[...])).astype(BF16)
    v = _dot(c, wv_ref[...]).astype(BF16)
    for hd in range(MLA_HEADS):
        kcat_out[0, hd] = kcat[:, hd * LANES:(hd + 1) * LANES]
    for hp in range(MLA_HEADS // 2):
        v_out[0, hp] = v[:, hp * LANES:(hp + 1) * LANES]


def _ctx_kv(ckv, krope, w):
    b, n_ctx, _ = ckv.shape
    tok = lambda n: pl.BlockSpec((None, n_ctx, n), lambda i: (i, 0, 0))
    heads = lambda nh: pl.BlockSpec((1, nh, n_ctx, LANES), lambda i: (i, 0, 0, 0))
    return pl.pallas_call(
        _ctx_kv_kernel,
        grid=(b,),
        in_specs=[tok(MLA_KV_LORA), tok(MLA_ROPE),
                  _full((MLA_KV_LORA, MLA_HEADS * LANES)), _full((MLA_KV_LORA, MLA_HEADS * MLA_V)),
                  _full((MLA_ROPE, MLA_HEADS * LANES))],
        out_specs=[heads(MLA_HEADS), heads(MLA_HEADS // 2)],
        out_shape=[jax.ShapeDtypeStruct((b, MLA_HEADS, n_ctx, LANES), BF16),
                   jax.ShapeDtypeStruct((b, MLA_HEADS // 2, n_ctx, LANES), BF16)],
        compiler_params=_params("arbitrary"),
        name="mla_ctx_kv",
    )(ckv, krope, w["wk"], w["wv"], w["e"][:MLA_ROPE])


_NEG = -1e30


def _two_map_flash(q0, q1, sources):
    state = [None, None]
    for k0_ref, k1_ref, v_ref in sources:
        rows = v_ref.shape[0]
        tk = min(ATTN_K_TILE, rows)
        ones = jnp.ones((tk, LANES), BF16)
        for c in range(rows // tk):
            r = slice(c * tk, (c + 1) * tk)
            v = jnp.concatenate([v_ref[r, :], ones], axis=1)
            for j, (q, k_ref) in enumerate(((q0, k0_ref), (q1, k1_ref))):
                s = _dot_nt(q, k_ref[r, :])
                m_blk = jnp.max(s, axis=-1, keepdims=True)
                if state[j] is None:
                    p = jnp.exp2(s - m_blk)
                    state[j] = (m_blk, _dot(p.astype(BF16), v))
                else:
                    m, acc = state[j]
                    m_new = jnp.maximum(m, m_blk)
                    p = jnp.exp2(s - m_new)
                    state[j] = (m_new, jnp.exp2(m - m_new) * acc + _dot(p.astype(BF16), v))
    return tuple((acc[:, LANES:], acc[:, :LANES]) for _, acc in state)


def _mla_attn_kernel(has_ctx, *refs):
    if has_ctx:
        q_ref, kn, vn, kc, vc, o_ref = refs
    else:
        q_ref, kn, vn, o_ref = refs
    for i in range(vn.shape[0]):
        e, o = 2 * i, 2 * i + 1
        sources = [(kn.at[e], kn.at[o], vn.at[i])]
        if has_ctx:
            sources.insert(0, (kc.at[e], kc.at[o], vc.at[i]))
        (l0, acc0), (l1, acc1) = _two_map_flash(q_ref[e], q_ref[o], sources)
        lane = lax.broadcasted_iota(jnp.int32, acc0.shape, 1)
        o_ref[:, i * LANES:(i + 1) * LANES] = jnp.where(
            lane < MLA_V, acc0 * (1.0 / l0), acc1 * (1.0 / l1)).astype(BF16)


def _mla_attn(q, k_new, v_new, ctx, tq, pairs_per_step):
    b, nh, n, _ = q.shape
    pp = pairs_per_step
    slab = lambda heads, rows, f: pl.BlockSpec((None, heads, rows, LANES), f)
    whole = lambda bi, p, qi: (bi, p, 0, 0)
    in_specs = [slab(2 * pp, tq, lambda bi, p, qi: (bi, p, qi, 0)), slab(2 * pp, n, whole), slab(pp, n, whole)]
    args = [q, k_new, v_new]
    if ctx is not None:
        nc = ctx[0].shape[2]
        in_specs += [slab(2 * pp, nc, whole), slab(pp, nc, whole)]
        args += list(ctx)
    return pl.pallas_call(
        functools.partial(_mla_attn_kernel, ctx is not None),
        grid=(b, nh // (2 * pp), n // tq),
        in_specs=in_specs,
        out_specs=pl.BlockSpec((None, tq, pp * LANES), lambda bi, p, qi: (bi, qi, p)),
        out_shape=jax.ShapeDtypeStruct((b, n, (nh // 2) * LANES), BF16),
        compiler_params=_params("arbitrary", "arbitrary", "arbitrary"),
        name="mla_attn",
    )(*args)


def _gla_kernel(has_s0, lookahead, *refs):
    fwd_in, bwd_in = refs[0:4], refs[4:8]
    refs = refs[8:]
    if lookahead:
        next_misc = refs[:2]
        refs = refs[2:]
    wg_ref, bg_ref, tf_ref, tb_ref = refs[:4]
    refs = refs[4:]
    if has_s0:
        s0f_ref, s0b_ref = refs[:2]
        refs = refs[2:]
    of_ref, ob_ref, sf_ref, sb_ref = refs[:4]
    tb_rows = of_ref.shape[1]
    nch = tb_rows // GLA_CHUNK
    n_bb = of_ref.shape[0]
    t_refs = (tf_ref, tb_ref)

    def log_decay(m_ref, d, bb):
        pre = _dot(m_ref[bb].astype(BF16), wg_ref[d]) + bg_ref[d]
        g = _log_sigmoid(pre) * (1.0 / GLA_GATE_NORM)
        g_hi = g.astype(BF16)
        g_lo = (g - g_hi.astype(F32)).astype(BF16)
        t = t_refs[d][...]
        return _dot(t, g_hi) + _dot(t, g_lo)

    @pl.when(pl.program_id(1) == 0)
    def _():
        if has_s0:
            sf_ref[...] = s0f_ref[...]
            sb_ref[...] = s0b_ref[...]
        else:
            sf_ref[...] = jnp.zeros(sf_ref.shape, F32)
            sb_ref[...] = jnp.zeros(sb_ref.shape, F32)
        if lookahead:
            cum_scr = refs[4]
            for d, m_ref in enumerate((fwd_in[3], bwd_in[3])):
                for bb in range(n_bb):
                    cum_scr[d, bb] = log_decay(m_ref, d, bb)

    scans = []
    for d, ((q_ref, k_ref, v_ref, m_ref), o_ref, s_ref) in enumerate(
            ((fwd_in, of_ref, sf_ref), (bwd_in, ob_ref, sb_ref))):
        row = lax.broadcasted_iota(jnp.int32, (tb_rows, tb_rows), 0)
        col = lax.broadcasted_iota(jnp.int32, (tb_rows, tb_rows), 1)
        chunk_lo = (row // GLA_CHUNK) * GLA_CHUNK
        if d == 0:
            tmask = (col <= row) & (col >= chunk_lo)
        else:
            tmask = (col >= row) & (col < chunk_lo + GLA_CHUNK)
        for bb in range(n_bb):
            cum = refs[4][d, bb] if lookahead else log_decay(m_ref, d, bb)
            for hd in range(GLA_HEADS):
                sl = slice(hd * LANES, (hd + 1) * LANES)
                scans.append(dict(d=d, bb=bb, hd=hd, sl=sl, cs=cum[:, sl], tmask=tmask,
                                  q_ref=q_ref, k_ref=k_ref, v_ref=v_ref, o_ref=o_ref, s_ref=s_ref))

    for sc in scans:
        cs, bb, sl = sc["cs"], sc["bb"], sc["sl"]
        k = sc["k_ref"][bb, :, sl].astype(F32)
        v = sc["v_ref"][bb, :, sl]
        sc["q_in"] = (sc["q_ref"][bb, :, sl].astype(F32) * jnp.exp(cs)).astype(BF16)
        k_in = (k * jnp.exp(-cs)).astype(BF16)
        a = jnp.where(sc["tmask"], _dot_nt(sc["q_in"], k_in), 0.0)
        sc["o_intra"] = _dot(a.astype(BF16), v)
        sc["decay"], sc["u_t"] = [], []
        for n in range(nch):
            r = slice(n * GLA_CHUNK, (n + 1) * GLA_CHUNK)
            edge = (n + 1) * GLA_CHUNK - 1 if sc["d"] == 0 else n * GLA_CHUNK
            tot = cs[edge:edge + 1, :]
            k_dec = (k[r] * jnp.exp(tot - cs[r])).astype(BF16)
            sc["decay"].append(jnp.exp(tot))
            sc["u_t"].append(_dot_tn(v[r], k_dec))
        sc["s_t"] = sc["s_ref"][bb, sc["hd"]]

    for step in range(nch):
        for sc in scans:
            n = step if sc["d"] == 0 else nch - 1 - step
            r = slice(n * GLA_CHUNK, (n + 1) * GLA_CHUNK)
            sc["o_ref"][sc["bb"], r, sc["sl"]] = (
                sc["o_intra"][r] + _dot_nt(sc["q_in"][r], sc["s_t"].astype(BF16)))
            sc["s_t"] = sc["s_t"] * sc["decay"][n] + sc["u_t"][n]

    for sc in scans:
        sc["s_ref"][sc["bb"], sc["hd"]] = sc["s_t"]

    if lookahead:
        for d, m_ref in enumerate(next_misc):
            for bb in range(n_bb):
                refs[4][d, bb] = log_decay(m_ref, d, bb)


def _gla_masks(tb_rows):
    idx = np.arange(tb_rows)
    same = (idx[:, None] // GLA_CHUNK) == (idx[None, :] // GLA_CHUNK)
    fwd = same & (idx[None, :] <= idx[:, None])
    bwd = same & (idx[None, :] >= idx[:, None])
    return jnp.asarray(fwd, dtype=BF16), jnp.asarray(bwd, dtype=BF16)


def _gla(gq, gk, gv, misc, w, s0, tb_rows):
    b, n, width = gq.shape
    nb = n // tb_rows
    gb = GLA_BATCH
    fw = lambda cols: pl.BlockSpec((gb, tb_rows, cols), lambda bi, i: (bi, i, 0))
    bw = lambda cols: pl.BlockSpec((gb, tb_rows, cols), lambda bi, i: (bi, nb - 1 - i, 0))
    st = pl.BlockSpec((gb, GLA_HEADS, GLA_DV, GLA_DK), lambda bi, i: (bi, 0, 0, 0))
    tf, tb = _gla_masks(tb_rows)
    lookahead = nb > 1
    in_specs = [fw(width)] * 3 + [fw(LANES)] + [bw(width)] * 3 + [bw(LANES)]
    args = [gq, gk, gv, misc, gq, gk, gv, misc]
    if lookahead:
        in_specs += [pl.BlockSpec((gb, tb_rows, LANES), lambda bi, i: (bi, jnp.minimum(i + 1, nb - 1), 0)),
                     pl.BlockSpec((gb, tb_rows, LANES), lambda bi, i: (bi, jnp.maximum(nb - 2 - i, 0), 0))]
        args += [misc, misc]
    in_specs += [_full((2, LANES, width)), _full((2, 1, width)),
                 _full((tb_rows, tb_rows)), _full((tb_rows, tb_rows))]
    args += [w["wg"], w["bg"], tf, tb]
    if s0 is not None:
        in_specs += [st, st]
        args += list(s0)
    scratch = [pltpu.VMEM((2, gb, tb_rows, width), F32)] if lookahead else []
    return pl.pallas_call(
        functools.partial(_gla_kernel, s0 is not None, lookahead),
        grid=(b // gb, nb),
        in_specs=in_specs,
        out_specs=[fw(width), bw(width), st, st],
        out_shape=[jax.ShapeDtypeStruct((b, n, width), F32)] * 2
        + [jax.ShapeDtypeStruct((b, GLA_HEADS, GLA_DV, GLA_DK), F32)] * 2,
        scratch_shapes=scratch,
        compiler_params=_params("arbitrary", "arbitrary"),
        name="gla",
    )(*args)


def _layer_tail_kernel(has_gla, *refs):
    if has_gla:
        mix_ref, of_ref, ob_ref, gout_ref, ggo_ref = refs[:5]
        refs = refs[5:]
    else:
        mix_ref = refs[0]
        refs = refs[1:]
    (wmix_ref, x_ref, mod_ref, g_ref, wg_ref, wu_ref, wo_ref, o_ref, h_scr, x1_scr, acc_scr) = refs
    j = pl.program_id(1)

    @pl.when(j == 0)
    def _():
        mod = mod_ref[0]
        if has_gla:
            o = of_ref[...] + ob_ref[...]
            parts = []
            for hd in range(GLA_HEADS):
                sl = slice(hd * LANES, (hd + 1) * LANES)
                gate = gout_ref[:, sl].astype(F32)
                parts.append((_rms(o[:, sl], ggo_ref[...]) * _silu(gate)).astype(BF16))
            n_mla = mix_ref.shape[1]
            out = (_dot(mix_ref[...], wmix_ref[0:n_mla, :])
                   + _dot(jnp.concatenate(parts, axis=-1), wmix_ref[n_mla:, :]))
        else:
            out = _dot(mix_ref[...], wmix_ref[...])
        gate1 = mod[:, 2 * D_MODEL:3 * D_MODEL]
        x1 = x_ref[...] + gate1 * _rms(out, g_ref[1:2, :])
        x1_scr[...] = x1
        shift, scale = mod[:, 3 * D_MODEL:4 * D_MODEL], mod[:, 4 * D_MODEL:5 * D_MODEL]
        h_scr[...] = (_rms(x1, g_ref[2:3, :]) * (1.0 + scale) + shift).astype(BF16)

    hb = h_scr[...]
    act = (_silu(_dot(hb, wg_ref[...])) * _dot(hb, wu_ref[...])).astype(BF16)
    part = _dot(act, wo_ref[...])

    @pl.when(j == 0)
    def _():
        acc_scr[...] = part

    @pl.when(j > 0)
    def _():
        acc_scr[...] += part

    @pl.when(j == pl.num_programs(1) - 1)
    def _():
        gate2 = mod_ref[0][:, 5 * D_MODEL:6 * D_MODEL]
        o_ref[...] = x1_scr[...] + gate2 * _rms(acc_scr[...], g_ref[3:4, :])


def _layer_tail(mix, gla, w_mix, x, mod3, modrow, g, w_in, w_out, tm):
    t = x.shape[0]
    th = FFN_HIDDEN // FFN_SPLIT
    tok = lambda n: pl.BlockSpec((tm, n), lambda i, j: (i, 0))
    const = lambda shape: pl.BlockSpec(shape, lambda i, j: (0,) * len(shape))
    in_specs = [tok(mix.shape[1])]
    args = [mix]
    if gla is not None:
        o_f, o_b, gout, g_gla = gla
        in_specs += [tok(o_f.shape[1])] * 3 + [const((1, GLA_DV))]
        args += [o_f, o_b, gout, g_gla]
    in_specs += [const(w_mix.shape), tok(D_MODEL),
                 pl.BlockSpec((1, 1, 6 * D_MODEL), lambda i, j: (modrow(i), 0, 0)), const(g.shape),
                 pl.BlockSpec((D_MODEL, th), lambda i, j: (0, j)),
                 pl.BlockSpec((D_MODEL, th), lambda i, j: (0, j + FFN_SPLIT)),
                 pl.BlockSpec((th, D_MODEL), lambda i, j: (j, 0))]
    args += [w_mix, x, mod3, g, w_in, w_in, w_out]
    return pl.pallas_call(
        functools.partial(_layer_tail_kernel, gla is not None),
        grid=(t // tm, FFN_SPLIT),
        in_specs=in_specs,
        out_specs=tok(D_MODEL),
        out_shape=jax.ShapeDtypeStruct((t, D_MODEL), F32),
        scratch_shapes=[pltpu.VMEM((tm, D_MODEL), BF16), pltpu.VMEM((tm, D_MODEL), F32),
                        pltpu.VMEM((tm, D_MODEL), F32)],
        compiler_params=_params("arbitrary", "arbitrary"),
        name="layer_tail",
    )(*args)


def _c_in_kernel(use_rope, want_f32, *refs):
    x_ref, mod_ref, g0_ref, w_ref = refs[:4]
    refs = refs[4:]
    if use_rope:
        c_ref, se_ref, so_ref = refs[:3]
        refs = refs[3:]
    q_out, k_out, v_out = refs[:3]
    mod = mod_ref[0]
    shift, scale = mod[:, 0:D_MODEL], mod[:, D_MODEL:2 * D_MODEL]
    h = _rms(x_ref[...], g0_ref[...]) * (1.0 + scale) + shift
    proj = _dot(h.astype(BF16), w_ref[...])
    width = DIFF_HEADS * LANES
    sm_scale = DIFF_HEAD_DIM ** -0.5 * LOG2E
    for hd in range(DIFF_HEADS):
        sl = slice(hd * LANES, (hd + 1) * LANES)
        qh = proj[:, hd * LANES:(hd + 1) * LANES]
        kh = proj[:, width + hd * LANES:width + (hd + 1) * LANES]
        vh = proj[:, 2 * width + hd * LANES:2 * width + (hd + 1) * LANES]
        if use_rope:
            qh = _rope(qh, c_ref[...], se_ref[...], so_ref[...])
            kh = _rope(kh, c_ref[...], se_ref[...], so_ref[...])
        q_out[0, hd] = (qh * sm_scale).astype(BF16)
        k_out[0, hd] = kh.astype(BF16)
        v_out[0, hd] = vh.astype(BF16)
        if want_f32:
            refs[3][:, sl] = kh
            refs[4][:, sl] = vh


def _c_in(x, mod3, modrow, g0, w_qkv, rope, want_f32, tm, seq_len):
    t = x.shape[0]
    width = DIFF_HEADS * LANES
    tok = lambda n: pl.BlockSpec((tm, n), lambda i: (i, 0))
    in_specs = [tok(D_MODEL), pl.BlockSpec((1, 1, 6 * D_MODEL), lambda i: (modrow(i), 0, 0)),
                _full((1, D_MODEL)), _full((D_MODEL, 3 * width))]
    args = [x, mod3, g0, w_qkv]
    per = seq_len // tm
    nb = t // seq_len
    if rope is not None:
        in_specs += [pl.BlockSpec((tm, LANES), lambda i: (i % per, 0))] * 3
        args += list(rope)
    heads = pl.BlockSpec((1, DIFF_HEADS, tm, LANES), lambda i: (i // per, 0, i % per, 0))
    head_shape = jax.ShapeDtypeStruct((nb, DIFF_HEADS, seq_len, LANES), BF16)
    n_f32 = 2 if want_f32 else 0
    return pl.pallas_call(
        functools.partial(_c_in_kernel, rope is not None, want_f32),
        grid=(t // tm,),
        in_specs=in_specs,
        out_specs=[heads] * 3 + [tok(width)] * n_f32,
        out_shape=[head_shape] * 3 + [jax.ShapeDtypeStruct((t, width), F32)] * n_f32,
        compiler_params=_params("arbitrary"),
        name="c_in",
    )(*args)


def _diff_attn_kernel(has_ctx, lam_init, *refs):
    if has_ctx:
        q_ref, kn, vn, kc, vc, lam_ref, g_ref, o_ref = refs
    else:
        q_ref, kn, vn, lam_ref, g_ref, o_ref = refs
    lp = lam_ref[...]
    lam = (jnp.exp(jnp.sum(lp[0:1] * lp[1:2], axis=-1, keepdims=True))
           - jnp.exp(jnp.sum(lp[2:3] * lp[3:4], axis=-1, keepdims=True)) + lam_init)
    for i in range(q_ref.shape[0]):
        sources = [(kn.at[i], kn.at[i], vn.at[i])]
        if has_ctx:
            sources.insert(0, (kc.at[i], kc.at[i], vc.at[i]))
        qh = q_ref[i]
        lane = lax.broadcasted_iota(jnp.int32, qh.shape, 1)
        zero = jnp.zeros_like(qh)
        q0 = jnp.where(lane < DIFF_HEAD_DIM, qh, zero)
        q1 = jnp.where(lane < DIFF_HEAD_DIM, zero, qh)
        (l0, acc0), (l1, acc1) = _two_map_flash(q0, q1, sources)
        o = acc0 * (1.0 / l0) - lam * (acc1 * (1.0 / l1))
        o_ref[:, i * LANES:(i + 1) * LANES] = (_rms(o, g_ref[...]) * (1.0 - lam_init)).astype(BF16)


def _diff_attn(q, k_new, v_new, ctx, lam_p, g_out, lam_init, tq, heads_per_step):
    b, nh, n, _ = q.shape
    hps = heads_per_step
    slab = lambda rows, f: pl.BlockSpec((None, hps, rows, LANES), f)
    whole = lambda bi, h, qi: (bi, h, 0, 0)
    in_specs = [slab(tq, lambda bi, h, qi: (bi, h, qi, 0)), slab(n, whole), slab(n, whole)]
    args = [q, k_new, v_new]
    if ctx is not None:
        in_specs += [slab(ctx[0].shape[2], whole)] * 2
        args += list(ctx)
    in_specs += [_full(lam_p.shape), _full(g_out.shape)]
    args += [lam_p, g_out]
    return pl.pallas_call(
        functools.partial(_diff_attn_kernel, ctx is not None, lam_init),
        grid=(b, nh // hps, n // tq),
        in_specs=in_specs,
        out_specs=pl.BlockSpec((None, tq, hps * LANES), lambda bi, h, qi: (bi, qi, h)),
        out_shape=jax.ShapeDtypeStruct((b, n, nh * LANES), BF16),
        compiler_params=_params("arbitrary", "arbitrary", "arbitrary"),
        name="diff_attn",
    )(*args)


def _prep_ab_weights(ai, ab_w_in, mla_g_q, mla_g_kv, mla_w_uq, mla_w_ukv, gla_w_gate_up, gla_b_gate):
    w = ab_w_in[ai]
    d = w.shape[0]
    o = np.cumsum([0, MLA_Q_LORA, MLA_KV_LORA, MLA_ROPE, GLA_HEADS * GLA_DK, GLA_HEADS * GLA_DK,
                   GLA_HEADS * GLA_DV, 2 * GLA_GATE_RANK, GLA_HEADS * GLA_DV])
    q_lat, kv_lat, k_rope, gq, gk, gv, ggate, gout = (w[:, o[i]:o[i + 1]] for i in range(8))
    pad = jnp.zeros((d, LANES - MLA_ROPE - 2 * GLA_GATE_RANK), w.dtype)
    win = jnp.concatenate([q_lat, kv_lat, gq, gk, gv, gout, k_rope, ggate, pad], axis=1).astype(BF16)

    head_pad = LANES - MLA_NOPE - MLA_ROPE
    wq = mla_w_uq[ai].reshape(MLA_Q_LORA, MLA_HEADS, MLA_NOPE + MLA_ROPE)
    wq = jnp.pad(wq, ((0, 0), (0, 0), (0, head_pad))).reshape(MLA_Q_LORA, MLA_HEADS * LANES).astype(BF16)
    wkv = mla_w_ukv[ai].reshape(MLA_KV_LORA, MLA_HEADS, MLA_NOPE + MLA_V)
    wk = jnp.pad(wkv[:, :, :MLA_NOPE], ((0, 0), (0, 0), (0, LANES - MLA_NOPE)))
    wk = wk.reshape(MLA_KV_LORA, MLA_HEADS * LANES).astype(BF16)
    wv = wkv[:, :, MLA_NOPE:].reshape(MLA_KV_LORA, MLA_HEADS * MLA_V).astype(BF16)

    e = np.zeros((LANES, MLA_HEADS * LANES), np.float32)
    for hd in range(MLA_HEADS):
        e[np.arange(MLA_ROPE), hd * LANES + MLA_NOPE + np.arange(MLA_ROPE)] = 1.0

    wg = jnp.zeros((2, LANES, GLA_HEADS * GLA_DK), F32)
    for dr in range(2):
        lo = _MISC_GATE + dr * GLA_GATE_RANK
        wg = wg.at[dr, lo:lo + GLA_GATE_RANK, :].set(gla_w_gate_up[ai, dr])
    return dict(win=win, g_q=mla_g_q[ai][None, :], g_kv=mla_g_kv[ai][None, :], wq=wq, wk=wk, wv=wv,
                e=jnp.asarray(e, dtype=BF16), wg=wg.astype(BF16),
                bg=gla_b_gate[ai].reshape(2, 1, GLA_HEADS * GLA_DK))


def _diff_lambda_init(layer):
    return 0.8 - 0.6 * math.exp(-0.3 * layer)


def kernel(x_prompt, x_sample, cache_mla_ckv, cache_mla_krope, state_gla_fwd, state_gla_bwd,
           cache_diff_k, cache_diff_v, c, c_ctx, w_mod, b_mod, g_norm, ab_w_in, mla_g_q, mla_g_kv,
           mla_w_uq, mla_w_ukv, gla_w_gate_up, gla_b_gate, gla_g_out, ab_w_out, c_w_qkv, diff_lambda,
           diff_g_out, c_w_out, w_ffn_in, w_ffn_out):
    depth = w_mod.shape[0]
    bp, lp, d = x_prompt.shape
    bs, ls, _ = x_sample.shape
    n_ctx = cache_mla_ckv.shape[2]

    n_cond = 1 + bs
    rows = -(-n_cond // 8) * 8
    conds = jnp.concatenate([c_ctx[None, :], c, jnp.zeros((rows - n_cond, d), F32)], axis=0)
    mod3 = _modulation(conds, w_mod, b_mod).reshape(depth * rows, 1, 6 * d)

    per_s = ls // min(TOKEN_TILE, ls)
    rope_q = _rope_tables(ls, MLA_ROPE, MLA_NOPE, LANES)
    rope_misc = _rope_tables(ls, MLA_ROPE, 0, LANES)
    rope_diff = _rope_tables(ls, DIFF_HEAD_DIM, 0, DIFF_HEAD_DIM)

    groups = {
        "prompt": dict(x=x_prompt.reshape(bp * lp, d), b=bp, n=lp, rope=False,
                       modrow=lambda li: (lambda i: li * rows)),
        "sample": dict(x=x_sample.reshape(bs * ls, d), b=bs, n=ls, rope=True,
                       modrow=lambda li: (lambda i: li * rows + 1 + i // per_s)),
    }
    ab_states, c_states = [], []

    ab_w = [_prep_ab_weights(ai, ab_w_in, mla_g_q, mla_g_kv, mla_w_uq, mla_w_ukv, gla_w_gate_up, gla_b_gate)
            for ai in range(ab_w_in.shape[0])]
    ab_w_out_b = ab_w_out.astype(BF16)
    c_w_qkv_b = c_w_qkv.astype(BF16)
    c_w_out_b = c_w_out.astype(BF16)
    w_ffn_in_b = w_ffn_in.astype(BF16)
    w_ffn_out_b = w_ffn_out.astype(BF16)

    for name, grp in groups.items():
        x, b, n = grp["x"], grp["b"], grp["n"]
        is_sample = grp["rope"]
        tm = min(TOKEN_TILE, n)
        for li in range(depth):
            modrow = grp["modrow"](li)
            g = g_norm[li]
            if li % 2 == 0:
                ai = li // 2
                w = ab_w[ai]
                rope = rope_q + rope_misc if is_sample else None
                q, kcat, v, ckv, misc, gq, gk, gv, gout = _ab_in(x, mod3, modrow, g[0:1], w, rope, tm, n)
                sh = lambda a: a.reshape(b, n, a.shape[-1])
                if is_sample:
                    ctx = _ctx_kv(cache_mla_ckv[:, ai], cache_mla_krope[:, ai], w)
                    s0 = (jnp.swapaxes(state_gla_fwd[:, ai], -1, -2), jnp.swapaxes(state_gla_bwd[:, ai], -1, -2))
                else:
                    ctx, s0 = None, None
                mla_o = _mla_attn(q, kcat, v, ctx, min(ATTN_Q_TILE, n),
                                  max(1, min(MLA_HEADS // 2, ATTN_STEP_ROWS // n)))
                o_f, o_b, s_f, s_b = _gla(sh(gq), sh(gk), sh(gv), sh(misc), w, s0, min(GLA_BLOCK, n))
                if not is_sample:
                    ab_states.append((ckv.reshape(b, n, MLA_KV_LORA), misc[:, :MLA_ROPE].reshape(b, n, MLA_ROPE),
                                      jnp.swapaxes(s_f, -1, -2), jnp.swapaxes(s_b, -1, -2)))
                mix = mla_o.reshape(b * n, -1)
                gla = (o_f.reshape(b * n, -1), o_b.reshape(b * n, -1), gout, gla_g_out[ai][None, :])
                w_mix = ab_w_out_b[ai]
            else:
                ci = li // 2
                rope = rope_diff if is_sample else None
                outs = _c_in(x, mod3, modrow, g[0:1], c_w_qkv_b[ci], rope, not is_sample, tm, n)
                q, k, v = outs[:3]
                if is_sample:
                    head_major = lambda a: jnp.swapaxes(a.reshape(bs, n_ctx, DIFF_HEADS, LANES), 1, 2).astype(BF16)
                    ctx = (head_major(cache_diff_k[:, ci]), head_major(cache_diff_v[:, ci]))
                else:
                    ctx = None
                    c_states.append((outs[3].reshape(b, n, DIFF_HEADS, 2, DIFF_HEAD_DIM),
                                     outs[4].reshape(b, n, DIFF_HEADS, 2 * DIFF_HEAD_DIM)))
                o = _diff_attn(q, k, v, ctx, diff_lambda[ci], diff_g_out[ci][None, :],
                               _diff_lambda_init(li), min(ATTN_Q_TILE, n),
                               max(1, min(DIFF_HEADS, ATTN_STEP_ROWS // n)))
                mix, gla, w_mix = o.reshape(b * n, -1), None, c_w_out_b[ci]
            x = _layer_tail(mix, gla, w_mix, x, mod3, modrow, g, w_ffn_in_b[li], w_ffn_out_b[li], tm)
        grp["y"] = x.reshape(b, n, d)

    stack = lambda states, i: jnp.stack([s[i] for s in states], axis=1)
    return (groups["prompt"]["y"], groups["sample"]["y"],
            stack(ab_states, 0), stack(ab_states, 1), stack(ab_states, 2), stack(ab_states, 3),
            stack(c_states, 0), stack(c_states, 1))
```

```python
import functools
import math

import numpy as np
import jax
import jax.numpy as jnp
from jax import lax
from jax.experimental import pallas as pl
from jax.experimental.pallas import tpu as pltpu

F32 = jnp.float32
BF16 = jnp.bfloat16

D_MODEL = 1024
GRID_W = 64
ROPE_BASE = 10000.0
NORM_EPS = 1e-6

MLA_HEADS = 8
MLA_NOPE = 64
MLA_ROPE = 32
MLA_V = 64
MLA_Q_LORA = 384
MLA_KV_LORA = 256
GLA_HEADS = 4
GLA_DK = 128
GLA_DV = 128
GLA_GATE_RANK = 16
GLA_GATE_NORM = 16.0
GLA_CHUNK = 64
DIFF_HEADS = 8
DIFF_HEAD_DIM = 64
FFN_HIDDEN = 2816

LANES = 128
VMEM_LIMIT = 56 * 1024 * 1024

_OFF_QLAT = 0
_OFF_KVLAT = _OFF_QLAT + MLA_Q_LORA
_OFF_GQ = _OFF_KVLAT + MLA_KV_LORA
_OFF_GK = _OFF_GQ + GLA_HEADS * GLA_DK
_OFF_GV = _OFF_GK + GLA_HEADS * GLA_DK
_OFF_GOUT = _OFF_GV + GLA_HEADS * GLA_DV
_OFF_MISC = _OFF_GOUT + GLA_HEADS * GLA_DV
_AB_COLS = _OFF_MISC + LANES
_MISC_GATE = MLA_ROPE

LOG2E = math.log2(math.e)

TOKEN_TILE = 512
ATTN_Q_TILE = 1024
ATTN_K_TILE = 256
ATTN_STEP_ROWS = 2048
GLA_BLOCK = 256
GLA_BATCH = 2
FFN_SPLIT = 2


def _rms(x, g):
    var = jnp.mean(x * x, axis=-1, keepdims=True)
    return x * lax.rsqrt(var + NORM_EPS) * g


def _silu(x):
    return x * (1.0 / (1.0 + jnp.exp(-x)))


def _log_sigmoid(x):
    return -(jnp.maximum(-x, 0.0) + jnp.log1p(jnp.exp(-jnp.abs(x))))


def _rope(x, c, se, so):
    n = x.shape[-1]
    return x * c + pltpu.roll(x, n - 1, 1) * se + pltpu.roll(x, 1, 1) * so


def _dot(a, b):
    return jnp.dot(a, b, preferred_element_type=F32)


def _dot_nt(a, b):
    return lax.dot_general(a, b, (((1,), (1,)), ((), ())), preferred_element_type=F32)


def _dot_tn(a, b):
    return lax.dot_general(a, b, (((0,), (0,)), ((), ())), preferred_element_type=F32)


def _params(*sem):
    return pltpu.CompilerParams(dimension_semantics=sem, vmem_limit_bytes=VMEM_LIMIT)


def _full(shape):
    zeros = (0,) * len(shape)
    return pl.BlockSpec(shape, lambda *_: zeros)


def _rope_tables(n_tokens, rot_dim, lane_lo, period):
    rows = n_tokens // GRID_W
    row = np.repeat(np.arange(rows, dtype=np.float64), GRID_W)
    col = np.tile(np.arange(GRID_W, dtype=np.float64), rows)
    n_freq = rot_dim // 4
    inv = ROPE_BASE ** (-np.arange(n_freq, dtype=np.float64) / n_freq)
    ang = np.concatenate([row[:, None] * inv, col[:, None] * inv], axis=-1)
    cos, sin = np.cos(ang), np.sin(ang)
    c = np.ones((n_tokens, period))
    se = np.zeros((n_tokens, period))
    so = np.zeros((n_tokens, period))
    c[:, lane_lo:lane_lo + rot_dim] = np.repeat(cos, 2, axis=-1)
    se[:, lane_lo:lane_lo + rot_dim:2] = -sin
    so[:, lane_lo + 1:lane_lo + rot_dim:2] = sin
    reps = LANES // period
    return tuple(jnp.asarray(np.tile(t, (1, reps)), dtype=F32) for t in (c, se, so))


def _mod_kernel(c_ref, w_ref, b_ref, o_ref):
    a = _silu(c_ref[...]).astype(BF16)
    o_ref[0] = _dot(a, w_ref[0].astype(BF16)) + b_ref[0]


def _modulation(conds, w_mod, b_mod):
    depth, d, n = w_mod.shape
    rows = conds.shape[0]
    tn = 1536
    return pl.pallas_call(
        _mod_kernel,
        grid=(depth, n // tn),
        in_specs=[_full((rows, d)),
                  pl.BlockSpec((1, d, tn), lambda l, j: (l, 0, j)),
                  pl.BlockSpec((1, 1, tn), lambda l, j: (l, 0, j))],
        out_specs=pl.BlockSpec((1, rows, tn), lambda l, j: (l, 0, j)),
        out_shape=jax.ShapeDtypeStruct((depth, rows, n), F32),
        compiler_params=_params("arbitrary", "arbitrary"),
        name="modulation",
    )(conds, w_mod, b_mod.reshape(depth, 1, n))


def _ab_in_kernel(use_rope, *refs):
    (x_ref, mod_ref, g0_ref, win_ref, gq_ref, gkv_ref, wq_ref, wk_ref, wv_ref, e_ref) = refs[:10]
    refs = refs[10:]
    if use_rope:
        cq, seq, soq, cm, sem, som = refs[:6]
        refs = refs[6:]
    (q_out, kcat_out, v_out, ckv_out, misc_out, gq_out, gk_out, gv_out, gout_out) = refs

    mod = mod_ref[0]
    shift, scale = mod[:, 0:D_MODEL], mod[:, D_MODEL:2 * D_MODEL]
    h = _rms(x_ref[...], g0_ref[...]) * (1.0 + scale) + shift
    proj = _dot(h.astype(BF16), win_ref[...])

    gq_out[...] = (proj[:, _OFF_GQ:_OFF_GK] * (GLA_DK ** -0.5)).astype(BF16)
    gk_out[...] = proj[:, _OFF_GK:_OFF_GV].astype(BF16)
    gv_out[...] = proj[:, _OFF_GV:_OFF_GOUT].astype(BF16)
    gout_out[...] = proj[:, _OFF_GOUT:_OFF_MISC].astype(BF16)

    misc = proj[:, _OFF_MISC:_AB_COLS]
    if use_rope:
        misc = _rope(misc, cm[...], sem[...], som[...])
    misc_out[...] = misc

    ckv = _rms(proj[:, _OFF_KVLAT:_OFF_GQ], gkv_ref[...])
    ckv_out[...] = ckv
    ckv_b = ckv.astype(BF16)
    kcat = (_dot(ckv_b, wk_ref[...]) + _dot(misc.astype(BF16), e_ref[...])).astype(BF16)
    v = _dot(ckv_b, wv_ref[...]).astype(BF16)
    for hp in range(MLA_HEADS // 2):
        v_out[0, hp] = v[:, hp * LANES:(hp + 1) * LANES]

    qn = _rms(proj[:, _OFF_QLAT:_OFF_KVLAT], gq_ref[...]).astype(BF16)
    q = _dot(qn, wq_ref[...])
    sm_scale = (MLA_NOPE + MLA_ROPE) ** -0.5 * LOG2E
    for hd in range(MLA_HEADS):
        sl = slice(hd * LANES, (hd + 1) * LANES)
        kcat_out[0, hd] = kcat[:, sl]
        qh = q[:, sl]
        if use_rope:
            qh = _rope(qh, cq[...], seq[...], soq[...])
        q_out[0, hd] = (qh * sm_scale).astype(BF16)


def _ab_in(x, mod3, modrow, g0, w, rope, tm, seq_len):
    t = x.shape[0]
    nt = t // tm
    tok = lambda n: pl.BlockSpec((tm, n), lambda i: (i, 0))
    in_specs = [tok(D_MODEL),
                pl.BlockSpec((1, 1, 6 * D_MODEL), lambda i: (modrow(i), 0, 0)),
                _full((1, D_MODEL)), _full((D_MODEL, _AB_COLS)),
                _full((1, MLA_Q_LORA)), _full((1, MLA_KV_LORA)),
                _full((MLA_Q_LORA, MLA_HEADS * LANES)), _full((MLA_KV_LORA, MLA_HEADS * LANES)),
                _full((MLA_KV_LORA, MLA_HEADS * MLA_V)), _full((LANES, MLA_HEADS * LANES))]
    args = [x, mod3, g0, w["win"], w["g_q"], w["g_kv"], w["wq"], w["wk"], w["wv"], w["e"]]
    per = seq_len // tm
    if rope is not None:
        in_specs += [pl.BlockSpec((tm, LANES), lambda i: (i % per, 0))] * 6
        args += list(rope)
    nb = t // seq_len
    heads = lambda nh: pl.BlockSpec((1, nh, tm, LANES), lambda i: (i // per, 0, i % per, 0))
    head_shape = lambda nh: jax.ShapeDtypeStruct((nb, nh, seq_len, LANES), BF16)
    widths = [(MLA_KV_LORA, F32), (LANES, F32)] + [(GLA_HEADS * GLA_DK, BF16)] * 4
    return pl.pallas_call(
        functools.partial(_ab_in_kernel, rope is not None),
        grid=(nt,),
        in_specs=in_specs,
        out_specs=[heads(MLA_HEADS), heads(MLA_HEADS), heads(MLA_HEADS // 2)] + [tok(n) for n, _ in widths],
        out_shape=[head_shape(MLA_HEADS), head_shape(MLA_HEADS), head_shape(MLA_HEADS // 2)]
        + [jax.ShapeDtypeStruct((t, n), dt) for n, dt in widths],
        compiler_params=_params("arbitrary"),
        name="ab_in",
    )(*args)


def _ctx_kv_kernel(ckv_ref, kr_ref, wk_ref, wv_ref, e_ref, kcat_out, v_out):
    c = ckv_ref[...].astype(BF16)
    kr = kr_ref[...].astype(BF16)
    kcat = (_dot(c, wk_ref[...]) + _dot(kr, e_ref[...])).astype(BF16)
    v = _dot(c, wv_ref[...]).astype(BF16)
    for hd in range(MLA_HEADS):
        kcat_out[0, hd] = kcat[:, hd * LANES:(hd + 1) * LANES]
    for hp in range(MLA_HEADS // 2):
        v_out[0, hp] = v[:, hp * LANES:(hp + 1) * LANES]


def _ctx_kv(ckv, krope, w):
    b, n_ctx, _ = ckv.shape
    tok = lambda n: pl.BlockSpec((None, n_ctx, n), lambda i: (i, 0, 0))
    heads = lambda nh: pl.BlockSpec((1, nh, n_ctx, LANES), lambda i: (i, 0, 0, 0))
    return pl.pallas_call(
        _ctx_kv_kernel,
        grid=(b,),
        in_specs=[tok(MLA_KV_LORA), tok(MLA_ROPE),
                  _full((MLA_KV_LORA, MLA_HEADS * LANES)), _full((MLA_KV_LORA, MLA_HEADS * MLA_V)),
                  _full((MLA_ROPE, MLA_HEADS * LANES))],
        out_specs=[heads(MLA_HEADS), heads(MLA_HEADS // 2)],
        out_shape=[jax.ShapeDtypeStruct((b, MLA_HEADS, n_ctx, LANES), BF16),
                   jax.ShapeDtypeStruct((b, MLA_HEADS // 2, n_ctx, LANES), BF16)],
        compiler_params=_params("arbitrary"),
        name="mla_ctx_kv",
    )(ckv, krope, w["wk"], w["wv"], w["e"][:MLA_ROPE])


def _two_map_flash(q0, q1, sources):
    state = [None, None]
    for k0_ref, k1_ref, v_ref in sources:
        rows = v_ref.shape[0]
        tk = min(ATTN_K_TILE, rows)
        ones = jnp.ones((tk, LANES), BF16)
        for c in range(rows // tk):
            r = slice(c * tk, (c + 1) * tk)
            v = jnp.concatenate([v_ref[r, :], ones], axis=1)
            for j, (q, k_ref) in enumerate(((q0, k0_ref), (q1, k1_ref))):
                s = _dot_nt(q, k_ref[r, :])
                m_blk = jnp.max(s, axis=-1, keepdims=True)
                if state[j] is None:
                    p = jnp.exp2(s - m_blk)
                    state[j] = (m_blk, _dot(p.astype(BF16), v))
                else:
                    m, acc = state[j]
                    m_new = jnp.maximum(m, m_blk)
                    p = jnp.exp2(s - m_new)
                    state[j] = (m_new, jnp.exp2(m - m_new) * acc + _dot(p.astype(BF16), v))
    return tuple((acc[:, LANES:], acc[:, :LANES]) for _, acc in state)


def _mla_attn_kernel(has_ctx, *refs):
    if has_ctx:
        q_ref, kn, vn, kc, vc, o_ref = refs
    else:
        q_ref, kn, vn, o_ref = refs
    for i in range(vn.shape[0]):
        e, o = 2 * i, 2 * i + 1
        sources = [(kn.at[e], kn.at[o], vn.at[i])]
        if has_ctx:
            sources.insert(0, (kc.at[e], kc.at[o], vc.at[i]))
        (l0, acc0), (l1, acc1) = _two_map_flash(q_ref[e], q_ref[o], sources)
        lane = lax.broadcasted_iota(jnp.int32, acc0.shape, 1)
        o_ref[:, i * LANES:(i + 1) * LANES] = jnp.where(
            lane < MLA_V, acc0 * (1.0 / l0), acc1 * (1.0 / l1)).astype(BF16)


def _mla_attn(q, k_new, v_new, ctx, tq, pairs_per_step):
    b, nh, n, _ = q.shape
    pp = pairs_per_step
    slab = lambda heads, rows, f: pl.BlockSpec((None, heads, rows, LANES), f)
    whole = lambda bi, p, qi: (bi, p, 0, 0)
    in_specs = [slab(2 * pp, tq, lambda bi, p, qi: (bi, p, qi, 0)), slab(2 * pp, n, whole), slab(pp, n, whole)]
    args = [q, k_new, v_new]
    if ctx is not None:
        nc = ctx[0].shape[2]
        in_specs += [slab(2 * pp, nc, whole), slab(pp, nc, whole)]
        args += list(ctx)
    return pl.pallas_call(
        functools.partial(_mla_attn_kernel, ctx is not None),
        grid=(b, nh // (2 * pp), n // tq),
        in_specs=in_specs,
        out_specs=pl.BlockSpec((None, tq, pp * LANES), lambda bi, p, qi: (bi, qi, p)),
        out_shape=jax.ShapeDtypeStruct((b, n, (nh // 2) * LANES), BF16),
        compiler_params=_params("arbitrary", "arbitrary", "arbitrary"),
        name="mla_attn",
    )(*args)


def _gla_kernel(has_s0, lookahead, *refs):
    fwd_in, bwd_in = refs[0:4], refs[4:8]
    refs = refs[8:]
    if lookahead:
        next_misc = refs[:2]
        refs = refs[2:]
    wg_ref, bg_ref, tf_ref, tb_ref = refs[:4]
    refs = refs[4:]
    if has_s0:
        s0f_ref, s0b_ref = refs[:2]
        refs = refs[2:]
    of_ref, ob_ref, sf_ref, sb_ref = refs[:4]
    tb_rows = of_ref.shape[1]
    nch = tb_rows // GLA_CHUNK
    n_bb = of_ref.shape[0]
    t_refs = (tf_ref, tb_ref)

    def log_decay(m_ref, d, bb):
        pre = _dot(m_ref[bb].astype(BF16), wg_ref[d]) + bg_ref[d]
        g = _log_sigmoid(pre) * (1.0 / GLA_GATE_NORM)
        g_hi = g.astype(BF16)
        g_lo = (g - g_hi.astype(F32)).astype(BF16)
        t = t_refs[d][...]
        return _dot(t, g_hi) + _dot(t, g_lo)

    @pl.when(pl.program_id(1) == 0)
    def _():
        if has_s0:
            sf_ref[...] = s0f_ref[...]
            sb_ref[...] = s0b_ref[...]
        else:
            sf_ref[...] = jnp.zeros(sf_ref.shape, F32)
            sb_ref[...] = jnp.zeros(sb_ref.shape, F32)
        if lookahead:
            cum_scr = refs[4]
            for d, m_ref in enumerate((fwd_in[3], bwd_in[3])):
                for bb in range(n_bb):
                    cum_scr[d, bb] = log_decay(m_ref, d, bb)

    scans = []
    for d, ((q_ref, k_ref, v_ref, m_ref), o_ref, s_ref) in enumerate(
            ((fwd_in, of_ref, sf_ref), (bwd_in, ob_ref, sb_ref))):
        row = lax.broadcasted_iota(jnp.int32, (tb_rows, tb_rows), 0)
        col = lax.broadcasted_iota(jnp.int32, (tb_rows, tb_rows), 1)
        chunk_lo = (row // GLA_CHUNK) * GLA_CHUNK
        if d == 0:
            tmask = (col <= row) & (col >= chunk_lo)
        else:
            tmask = (col >= row) & (col < chunk_lo + GLA_CHUNK)
        for bb in range(n_bb):
            cum = refs[4][d, bb] if lookahead else log_decay(m_ref, d, bb)
            for hd in range(GLA_HEADS):
                sl = slice(hd * LANES, (hd + 1) * LANES)
                scans.append(dict(d=d, bb=bb, hd=hd, sl=sl, cs=cum[:, sl], tmask=tmask,
                                  q_ref=q_ref, k_ref=k_ref, v_ref=v_ref, o_ref=o_ref, s_ref=s_ref))

    for sc in scans:
        cs, bb, sl = sc["cs"], sc["bb"], sc["sl"]
        k = sc["k_ref"][bb, :, sl].astype(F32)
        v = sc["v_ref"][bb, :, sl]
        sc["q_in"] = (sc["q_ref"][bb, :, sl].astype(F32) * jnp.exp(cs)).astype(BF16)
        k_neg = k * jnp.exp(-cs)
        a = jnp.where(sc["tmask"], _dot_nt(sc["q_in"], k_neg.astype(BF16)), 0.0)
        sc["o_intra"] = _dot(a.astype(BF16), v)
        sc["decay"], sc["u_t"] = [], []
        for n in range(nch):
            r = slice(n * GLA_CHUNK, (n + 1) * GLA_CHUNK)
            edge = (n + 1) * GLA_CHUNK - 1 if sc["d"] == 0 else n * GLA_CHUNK
            decay = jnp.exp(cs[edge:edge + 1, :])
            k_dec = (k_neg[r] * decay).astype(BF16)
            sc["decay"].append(decay)
            sc["u_t"].append(_dot_tn(v[r], k_dec))
        sc["s_t"] = sc["s_ref"][bb, sc["hd"]]

    for step in range(nch):
        for sc in scans:
            n = step if sc["d"] == 0 else nch - 1 - step
            r = slice(n * GLA_CHUNK, (n + 1) * GLA_CHUNK)
            sc["o_ref"][sc["bb"], r, sc["sl"]] = (
                sc["o_intra"][r] + _dot_nt(sc["q_in"][r], sc["s_t"].astype(BF16)))
            sc["s_t"] = sc["s_t"] * sc["decay"][n] + sc["u_t"][n]

    for sc in scans:
        sc["s_ref"][sc["bb"], sc["hd"]] = sc["s_t"]

    if lookahead:
        for d, m_ref in enumerate(next_misc):
            for bb in range(n_bb):
                refs[4][d, bb] = log_decay(m_ref, d, bb)


def _gla_masks(tb_rows):
    idx = np.arange(tb_rows)
    same = (idx[:, None] // GLA_CHUNK) == (idx[None, :] // GLA_CHUNK)
    fwd = same & (idx[None, :] <= idx[:, None])
    bwd = same & (idx[None, :] >= idx[:, None])
    return jnp.asarray(fwd, dtype=BF16), jnp.asarray(bwd, dtype=BF16)


def _gla(gq, gk, gv, misc, w, s0, tb_rows):
    b, n, width = gq.shape
    nb = n // tb_rows
    gb = GLA_BATCH
    fw = lambda cols: pl.BlockSpec((gb, tb_rows, cols), lambda bi, i: (bi, i, 0))
    bw = lambda cols: pl.BlockSpec((gb, tb_rows, cols), lambda bi, i: (bi, nb - 1 - i, 0))
    st = pl.BlockSpec((gb, GLA_HEADS, GLA_DV, GLA_DK), lambda bi, i: (bi, 0, 0, 0))
    tf, tb = _gla_masks(tb_rows)
    lookahead = nb > 1
    in_specs = [fw(width)] * 3 + [fw(LANES)] + [bw(width)] * 3 + [bw(LANES)]
    args = [gq, gk, gv, misc, gq, gk, gv, misc]
    if lookahead:
        in_specs += [pl.BlockSpec((gb, tb_rows, LANES), lambda bi, i: (bi, jnp.minimum(i + 1, nb - 1), 0)),
                     pl.BlockSpec((gb, tb_rows, LANES), lambda bi, i: (bi, jnp.maximum(nb - 2 - i, 0), 0))]
        args += [misc, misc]
    in_specs += [_full((2, LANES, width)), _full((2, 1, width)),
                 _full((tb_rows, tb_rows)), _full((tb_rows, tb_rows))]
    args += [w["wg"], w["bg"], tf, tb]
    if s0 is not None:
        in_specs += [st, st]
        args += list(s0)
    scratch = [pltpu.VMEM((2, gb, tb_rows, width), F32)] if lookahead else []
    return pl.pallas_call(
        functools.partial(_gla_kernel, s0 is not None, lookahead),
        grid=(b // gb, nb),
        in_specs=in_specs,
        out_specs=[fw(width), bw(width), st, st],
        out_shape=[jax.ShapeDtypeStruct((b, n, width), F32)] * 2
        + [jax.ShapeDtypeStruct((b, GLA_HEADS, GLA_DV, GLA_DK), F32)] * 2,
        scratch_shapes=scratch,
        compiler_params=_params("arbitrary", "arbitrary"),
        name="gla",
    )(*args)


def _layer_tail_kernel(has_gla, *refs):
    if has_gla:
        mix_ref, of_ref, ob_ref, gout_ref, ggo_ref = refs[:5]
        refs = refs[5:]
    else:
        mix_ref = refs[0]
        refs = refs[1:]
    (wmix_ref, x_ref, mod_ref, g_ref, wg_ref, wu_ref, wo_ref, o_ref, h_scr, x1_scr, acc_scr) = refs
    j = pl.program_id(1)

    @pl.when(j == 0)
    def _():
        mod = mod_ref[0]
        if has_gla:
            o = of_ref[...] + ob_ref[...]
            parts = []
            for hd in range(GLA_HEADS):
                sl = slice(hd * LANES, (hd + 1) * LANES)
                gate = gout_ref[:, sl].astype(F32)
                parts.append((_rms(o[:, sl], ggo_ref[...]) * _silu(gate)).astype(BF16))
            n_mla = mix_ref.shape[1]
            out = (_dot(mix_ref[...], wmix_ref[0:n_mla, :])
                   + _dot(jnp.concatenate(parts, axis=-1), wmix_ref[n_mla:, :]))
        else:
            out = _dot(mix_ref[...], wmix_ref[...])
        gate1 = mod[:, 2 * D_MODEL:3 * D_MODEL]
        x1 = x_ref[...] + gate1 * _rms(out, g_ref[1:2, :])
        x1_scr[...] = x1
        shift, scale = mod[:, 3 * D_MODEL:4 * D_MODEL], mod[:, 4 * D_MODEL:5 * D_MODEL]
        h_scr[...] = (_rms(x1, g_ref[2:3, :]) * (1.0 + scale) + shift).astype(BF16)

    hb = h_scr[...]
    act = (_silu(_dot(hb, wg_ref[...])) * _dot(hb, wu_ref[...])).astype(BF16)
    part = _dot(act, wo_ref[...])

    @pl.when(j == 0)
    def _():
        acc_scr[...] = part

    @pl.when(j > 0)
    def _():
        acc_scr[...] += part

    @pl.when(j == pl.num_programs(1) - 1)
    def _():
        gate2 = mod_ref[0][:, 5 * D_MODEL:6 * D_MODEL]
        o_ref[...] = x1_scr[...] + gate2 * _rms(acc_scr[...], g_ref[3:4, :])


def _layer_tail(mix, gla, w_mix, x, mod3, modrow, g, w_in, w_out, tm):
    t = x.shape[0]
    th = FFN_HIDDEN // FFN_SPLIT
    tok = lambda n: pl.BlockSpec((tm, n), lambda i, j: (i, 0))
    const = lambda shape: pl.BlockSpec(shape, lambda i, j: (0,) * len(shape))
    in_specs = [tok(mix.shape[1])]
    args = [mix]
    if gla is not None:
        o_f, o_b, gout, g_gla = gla
        in_specs += [tok(o_f.shape[1])] * 3 + [const((1, GLA_DV))]
        args += [o_f, o_b, gout, g_gla]
    in_specs += [const(w_mix.shape), tok(D_MODEL),
                 pl.BlockSpec((1, 1, 6 * D_MODEL), lambda i, j: (modrow(i), 0, 0)), const(g.shape),
                 pl.BlockSpec((D_MODEL, th), lambda i, j: (0, j)),
                 pl.BlockSpec((D_MODEL, th), lambda i, j: (0, j + FFN_SPLIT)),
                 pl.BlockSpec((th, D_MODEL), lambda i, j: (j, 0))]
    args += [w_mix, x, mod3, g, w_in, w_in, w_out]
    return pl.pallas_call(
        functools.partial(_layer_tail_kernel, gla is not None),
        grid=(t // tm, FFN_SPLIT),
        in_specs=in_specs,
        out_specs=tok(D_MODEL),
        out_shape=jax.ShapeDtypeStruct((t, D_MODEL), F32),
        scratch_shapes=[pltpu.VMEM((tm, D_MODEL), BF16), pltpu.VMEM((tm, D_MODEL), F32),
                        pltpu.VMEM((tm, D_MODEL), F32)],
        compiler_params=_params("arbitrary", "arbitrary"),
        name="layer_tail",
    )(*args)


def _c_in_kernel(use_rope, want_f32, *refs):
    x_ref, mod_ref, g0_ref, w_ref = refs[:4]
    refs = refs[4:]
    if use_rope:
        c_ref, se_ref, so_ref = refs[:3]
        refs = refs[3:]
    q_out, k_out, v_out = refs[:3]
    mod = mod_ref[0]
    shift, scale = mod[:, 0:D_MODEL], mod[:, D_MODEL:2 * D_MODEL]
    h = _rms(x_ref[...], g0_ref[...]) * (1.0 + scale) + shift
    proj = _dot(h.astype(BF16), w_ref[...])
    width = DIFF_HEADS * LANES
    sm_scale = DIFF_HEAD_DIM ** -0.5 * LOG2E
    for hd in range(DIFF_HEADS):
        sl = slice(hd * LANES, (hd + 1) * LANES)
        qh = proj[:, hd * LANES:(hd + 1) * LANES]
        kh = proj[:, width + hd * LANES:width + (hd + 1) * LANES]
        vh = proj[:, 2 * width + hd * LANES:2 * width + (hd + 1) * LANES]
        if use_rope:
            qh = _rope(qh, c_ref[...], se_ref[...], so_ref[...])
            kh = _rope(kh, c_ref[...], se_ref[...], so_ref[...])
        q_out[0, hd] = (qh * sm_scale).astype(BF16)
        k_out[0, hd] = kh.astype(BF16)
        v_out[0, hd] = vh.astype(BF16)
        if want_f32:
            refs[3][:, sl] = kh
            refs[4][:, sl] = vh


def _c_in(x, mod3, modrow, g0, w_qkv, rope, want_f32, tm, seq_len):
    t = x.shape[0]
    width = DIFF_HEADS * LANES
    tok = lambda n: pl.BlockSpec((tm, n), lambda i: (i, 0))
    in_specs = [tok(D_MODEL), pl.BlockSpec((1, 1, 6 * D_MODEL), lambda i: (modrow(i), 0, 0)),
                _full((1, D_MODEL)), _full((D_MODEL, 3 * width))]
    args = [x, mod3, g0, w_qkv]
    per = seq_len // tm
    nb = t // seq_len
    if rope is not None:
        in_specs += [pl.BlockSpec((tm, LANES), lambda i: (i % per, 0))] * 3
        args += list(rope)
    heads = pl.BlockSpec((1, DIFF_HEADS, tm, LANES), lambda i: (i // per, 0, i % per, 0))
    head_shape = jax.ShapeDtypeStruct((nb, DIFF_HEADS, seq_len, LANES), BF16)
    n_f32 = 2 if want_f32 else 0
    return pl.pallas_call(
        functools.partial(_c_in_kernel, rope is not None, want_f32),
        grid=(t // tm,),
        in_specs=in_specs,
        out_specs=[heads] * 3 + [tok(width)] * n_f32,
        out_shape=[head_shape] * 3 + [jax.ShapeDtypeStruct((t, width), F32)] * n_f32,
        compiler_params=_params("arbitrary"),
        name="c_in",
    )(*args)


def _diff_attn_kernel(has_ctx, lam_init, *refs):
    if has_ctx:
        q_ref, kn, vn, kc, vc, lam_ref, g_ref, o_ref = refs
    else:
        q_ref, kn, vn, lam_ref, g_ref, o_ref = refs
    lp = lam_ref[...]
    lam = (jnp.exp(jnp.sum(lp[0:1] * lp[1:2], axis=-1, keepdims=True))
           - jnp.exp(jnp.sum(lp[2:3] * lp[3:4], axis=-1, keepdims=True)) + lam_init)
    for i in range(q_ref.shape[0]):
        k_new, k_ctx = kn.at[i], (kc.at[i] if has_ctx else None)
        sources = [(k_new, k_new, vn.at[i])]
        if has_ctx:
            sources.insert(0, (k_ctx, k_ctx, vc.at[i]))
        qh = q_ref[i]
        lane = lax.broadcasted_iota(jnp.int32, qh.shape, 1)
        zero = jnp.zeros_like(qh)
        q0 = jnp.where(lane < DIFF_HEAD_DIM, qh, zero)
        q1 = jnp.where(lane < DIFF_HEAD_DIM, zero, qh)
        (l0, acc0), (l1, acc1) = _two_map_flash(q0, q1, sources)
        o = acc0 * (1.0 / l0) - lam * (acc1 * (1.0 / l1))
        o_ref[:, i * LANES:(i + 1) * LANES] = (_rms(o, g_ref[...]) * (1.0 - lam_init)).astype(BF16)


def _diff_attn(q, k_new, v_new, ctx, lam_p, g_out, lam_init, tq, heads_per_step):
    b, nh, n, _ = q.shape
    hps = heads_per_step
    slab = lambda rows, f: pl.BlockSpec((None, hps, rows, LANES), f)
    whole = lambda bi, h, qi: (bi, h, 0, 0)
    in_specs = [slab(tq, lambda bi, h, qi: (bi, h, qi, 0)), slab(n, whole), slab(n, whole)]
    args = [q, k_new, v_new]
    if ctx is not None:
        in_specs += [slab(ctx[0].shape[2], whole)] * 2
        args += list(ctx)
    in_specs += [_full(lam_p.shape), _full(g_out.shape)]
    args += [lam_p, g_out]
    return pl.pallas_call(
        functools.partial(_diff_attn_kernel, ctx is not None, lam_init),
        grid=(b, nh // hps, n // tq),
        in_specs=in_specs,
        out_specs=pl.BlockSpec((None, tq, hps * LANES), lambda bi, h, qi: (bi, qi, h)),
        out_shape=jax.ShapeDtypeStruct((b, n, nh * LANES), BF16),
        compiler_params=_params("arbitrary", "arbitrary", "arbitrary"),
        name="diff_attn",
    )(*args)


def _prep_ab_weights(ai, ab_w_in, mla_g_q, mla_g_kv, mla_w_uq, mla_w_ukv, gla_w_gate_up, gla_b_gate):
    w = ab_w_in[ai]
    d = w.shape[0]
    o = np.cumsum([0, MLA_Q_LORA, MLA_KV_LORA, MLA_ROPE, GLA_HEADS * GLA_DK, GLA_HEADS * GLA_DK,
                   GLA_HEADS * GLA_DV, 2 * GLA_GATE_RANK, GLA_HEADS * GLA_DV])
    q_lat, kv_lat, k_rope, gq, gk, gv, ggate, gout = (w[:, o[i]:o[i + 1]] for i in range(8))
    pad = jnp.zeros((d, LANES - MLA_ROPE - 2 * GLA_GATE_RANK), w.dtype)
    win = jnp.concatenate([q_lat, kv_lat, gq, gk, gv, gout, k_rope, ggate, pad], axis=1).astype(BF16)

    head_pad = LANES - MLA_NOPE - MLA_ROPE
    wq = mla_w_uq[ai].reshape(MLA_Q_LORA, MLA_HEADS, MLA_NOPE + MLA_ROPE)
    wq = jnp.pad(wq, ((0, 0), (0, 0), (0, head_pad))).reshape(MLA_Q_LORA, MLA_HEADS * LANES).astype(BF16)
    wkv = mla_w_ukv[ai].reshape(MLA_KV_LORA, MLA_HEADS, MLA_NOPE + MLA_V)
    wk = jnp.pad(wkv[:, :, :MLA_NOPE], ((0, 0), (0, 0), (0, LANES - MLA_NOPE)))
    wk = wk.reshape(MLA_KV_LORA, MLA_HEADS * LANES).astype(BF16)
    wv = wkv[:, :, MLA_NOPE:].reshape(MLA_KV_LORA, MLA_HEADS * MLA_V).astype(BF16)

    e = np.zeros((LANES, MLA_HEADS * LANES), np.float32)
    for hd in range(MLA_HEADS):
        e[np.arange(MLA_ROPE), hd * LANES + MLA_NOPE + np.arange(MLA_ROPE)] = 1.0

    wg = jnp.zeros((2, LANES, GLA_HEADS * GLA_DK), F32)
    for dr in range(2):
        lo = _MISC_GATE + dr * GLA_GATE_RANK
        wg = wg.at[dr, lo:lo + GLA_GATE_RANK, :].set(gla_w_gate_up[ai, dr])
    return dict(win=win, g_q=mla_g_q[ai][None, :], g_kv=mla_g_kv[ai][None, :], wq=wq, wk=wk, wv=wv,
                e=jnp.asarray(e, dtype=BF16), wg=wg.astype(BF16),
                bg=gla_b_gate[ai].reshape(2, 1, GLA_HEADS * GLA_DK))


def _diff_lambda_init(layer):
    return 0.8 - 0.6 * math.exp(-0.3 * layer)


def kernel(x_prompt, x_sample, cache_mla_ckv, cache_mla_krope, state_gla_fwd, state_gla_bwd,
           cache_diff_k, cache_diff_v, c, c_ctx, w_mod, b_mod, g_norm, ab_w_in, mla_g_q, mla_g_kv,
           mla_w_uq, mla_w_ukv, gla_w_gate_up, gla_b_gate, gla_g_out, ab_w_out, c_w_qkv, diff_lambda,
           diff_g_out, c_w_out, w_ffn_in, w_ffn_out):
    depth = w_mod.shape[0]
    bp, lp, d = x_prompt.shape
    bs, ls, _ = x_sample.shape
    n_ctx = cache_mla_ckv.shape[2]

    n_cond = 1 + bs
    rows = -(-n_cond // 8) * 8
    conds = jnp.concatenate([c_ctx[None, :], c, jnp.zeros((rows - n_cond, d), F32)], axis=0)
    mod3 = _modulation(conds, w_mod, b_mod).reshape(depth * rows, 1, 6 * d)

    per_s = ls // min(TOKEN_TILE, ls)
    rope_q = _rope_tables(ls, MLA_ROPE, MLA_NOPE, LANES)
    rope_misc = _rope_tables(ls, MLA_ROPE, 0, LANES)
    rope_diff = _rope_tables(ls, DIFF_HEAD_DIM, 0, DIFF_HEAD_DIM)

    groups = {
        "prompt": dict(x=x_prompt.reshape(bp * lp, d), b=bp, n=lp, rope=False,
                       modrow=lambda li: (lambda i: li * rows)),
        "sample": dict(x=x_sample.reshape(bs * ls, d), b=bs, n=ls, rope=True,
                       modrow=lambda li: (lambda i: li * rows + 1 + i // per_s)),
    }
    ab_states, c_states = [], []

    ab_w = [_prep_ab_weights(ai, ab_w_in, mla_g_q, mla_g_kv, mla_w_uq, mla_w_ukv, gla_w_gate_up, gla_b_gate)
            for ai in range(ab_w_in.shape[0])]
    ab_w_out_b = ab_w_out.astype(BF16)
    c_w_qkv_b = c_w_qkv.astype(BF16)
    c_w_out_b = c_w_out.astype(BF16)
    w_ffn_in_b = w_ffn_in.astype(BF16)
    w_ffn_out_b = w_ffn_out.astype(BF16)

    for name, grp in groups.items():
        x, b, n = grp["x"], grp["b"], grp["n"]
        is_sample = grp["rope"]
        tm = min(TOKEN_TILE, n)
        for li in range(depth):
            modrow = grp["modrow"](li)
            g = g_norm[li]
            if li % 2 == 0:
                ai = li // 2
                w = ab_w[ai]
                rope = rope_q + rope_misc if is_sample else None
                q, kcat, v, ckv, misc, gq, gk, gv, gout = _ab_in(x, mod3, modrow, g[0:1], w, rope, tm, n)
                sh = lambda a: a.reshape(b, n, a.shape[-1])
                if is_sample:
                    ctx = _ctx_kv(cache_mla_ckv[:, ai], cache_mla_krope[:, ai], w)
                    s0 = (jnp.swapaxes(state_gla_fwd[:, ai], -1, -2), jnp.swapaxes(state_gla_bwd[:, ai], -1, -2))
                else:
                    ctx, s0 = None, None
                mla_o = _mla_attn(q, kcat, v, ctx, min(ATTN_Q_TILE, n),
                                  max(1, min(MLA_HEADS // 2, ATTN_STEP_ROWS // n)))
                o_f, o_b, s_f, s_b = _gla(sh(gq), sh(gk), sh(gv), sh(misc), w, s0, min(GLA_BLOCK, n))
                if not is_sample:
                    ab_states.append((ckv.reshape(b, n, MLA_KV_LORA), misc[:, :MLA_ROPE].reshape(b, n, MLA_ROPE),
                                      jnp.swapaxes(s_f, -1, -2), jnp.swapaxes(s_b, -1, -2)))
                mix = mla_o.reshape(b * n, -1)
                gla = (o_f.reshape(b * n, -1), o_b.reshape(b * n, -1), gout, gla_g_out[ai][None, :])
                w_mix = ab_w_out_b[ai]
            else:
                ci = li // 2
                rope = rope_diff if is_sample else None
                outs = _c_in(x, mod3, modrow, g[0:1], c_w_qkv_b[ci], rope, not is_sample, tm, n)
                q, k, v = outs[:3]
                if is_sample:
                    head_major = lambda a: jnp.swapaxes(a.reshape(bs, n_ctx, DIFF_HEADS, LANES), 1, 2).astype(BF16)
                    ctx = (head_major(cache_diff_k[:, ci]), head_major(cache_diff_v[:, ci]))
                else:
                    ctx = None
                    c_states.append((outs[3].reshape(b, n, DIFF_HEADS, 2, DIFF_HEAD_DIM),
                                     outs[4].reshape(b, n, DIFF_HEADS, 2 * DIFF_HEAD_DIM)))
                o = _diff_attn(q, k, v, ctx, diff_lambda[ci], diff_g_out[ci][None, :],
                               _diff_lambda_init(li), min(ATTN_Q_TILE, n),
                               max(1, min(DIFF_HEADS, ATTN_STEP_ROWS // n)))
                mix, gla, w_mix = o.reshape(b * n, -1), None, c_w_out_b[ci]
            tm_tail = tm if is_sample else min(TOKEN_TILE, b * n)
            x = _layer_tail(mix, gla, w_mix, x, mod3, modrow, g, w_ffn_in_b[li], w_ffn_out_b[li], tm_tail)
        grp["y"] = x.reshape(b, n, d)

    stack = lambda states, i: jnp.stack([s[i] for s in states], axis=1)
    return (groups["prompt"]["y"], groups["sample"]["y"],
            stack(ab_states, 0), stack(ab_states, 1), stack(ab_states, 2), stack(ab_states, 3),
            stack(c_states, 0), stack(c_states, 1))
```

```python
import functools
import math

import numpy as np
import jax
import jax.numpy as jnp
from jax import lax
from jax.experimental import pallas as pl
from jax.experimental.pallas import tpu as pltpu

F32 = jnp.float32
BF16 = jnp.bfloat16

D_MODEL = 1024
GRID_W = 64
ROPE_BASE = 10000.0
NORM_EPS = 1e-6

MLA_HEADS = 8
MLA_NOPE = 64
MLA_ROPE = 32
MLA_V = 64
MLA_Q_LORA = 384
MLA_KV_LORA = 256
GLA_HEADS = 4
GLA_DK = 128
GLA_DV = 128
GLA_GATE_RANK = 16
GLA_GATE_NORM = 16.0
GLA_CHUNK = 64
DIFF_HEADS = 8
DIFF_HEAD_DIM = 64
FFN_HIDDEN = 2816

LANES = 128
VMEM_LIMIT = 56 * 1024 * 1024

_OFF_QLAT = 0
_OFF_KVLAT = _OFF_QLAT + MLA_Q_LORA
_OFF_GQ = _OFF_KVLAT + MLA_KV_LORA
_OFF_GK = _OFF_GQ + GLA_HEADS * GLA_DK
_OFF_GV = _OFF_GK + GLA_HEADS * GLA_DK
_OFF_GOUT = _OFF_GV + GLA_HEADS * GLA_DV
_OFF_MISC = _OFF_GOUT + GLA_HEADS * GLA_DV
_AB_COLS = _OFF_MISC + LANES
_MISC_GATE = MLA_ROPE

LOG2E = math.log2(math.e)

TOKEN_TILE = 512
ATTN_Q_TILE = 1024
ATTN_K_TILE = 256
ATTN_STEP_ROWS = 2048
GLA_BLOCK = 256
GLA_BATCH = 2
FFN_SPLIT = 2


def _rms(x, g):
    var = jnp.mean(x * x, axis=-1, keepdims=True)
    return x * lax.rsqrt(var + NORM_EPS) * g


def _silu(x):
    return x * (1.0 / (1.0 + jnp.exp(-x)))


def _log_sigmoid(x):
    return -(jnp.maximum(-x, 0.0) + jnp.log1p(jnp.exp(-jnp.abs(x))))


def _rope(x, c, se, so):
    n = x.shape[-1]
    return x * c + pltpu.roll(x, n - 1, 1) * se + pltpu.roll(x, 1, 1) * so


def _dot(a, b):
    return jnp.dot(a, b, preferred_element_type=F32)


def _dot_nt(a, b):
    return lax.dot_general(a, b, (((1,), (1,)), ((), ())), preferred_element_type=F32)


def _dot_tn(a, b):
    return lax.dot_general(a, b, (((0,), (0,)), ((), ())), preferred_element_type=F32)


def _params(*sem):
    return pltpu.CompilerParams(dimension_semantics=sem, vmem_limit_bytes=VMEM_LIMIT)


def _full(shape):
    zeros = (0,) * len(shape)
    return pl.BlockSpec(shape, lambda *_: zeros)


def _rope_tables(n_tokens, rot_dim, lane_lo, period):
    rows = n_tokens // GRID_W
    row = np.repeat(np.arange(rows, dtype=np.float64), GRID_W)
    col = np.tile(np.arange(GRID_W, dtype=np.float64), rows)
    n_freq = rot_dim // 4
    inv = ROPE_BASE ** (-np.arange(n_freq, dtype=np.float64) / n_freq)
    ang = np.concatenate([row[:, None] * inv, col[:, None] * inv], axis=-1)
    cos, sin = np.cos(ang), np.sin(ang)
    c = np.ones((n_tokens, period))
    se = np.zeros((n_tokens, period))
    so = np.zeros((n_tokens, period))
    c[:, lane_lo:lane_lo + rot_dim] = np.repeat(cos, 2, axis=-1)
    se[:, lane_lo:lane_lo + rot_dim:2] = -sin
    so[:, lane_lo + 1:lane_lo + rot_dim:2] = sin
    reps = LANES // period
    return tuple(jnp.asarray(np.tile(t, (1, reps)), dtype=F32) for t in (c, se, so))


def _mod_kernel(c_ref, w_ref, b_ref, o_ref):
    a = _silu(c_ref[...]).astype(BF16)
    o_ref[0] = _dot(a, w_ref[0].astype(BF16)) + b_ref[0]


def _modulation(conds, w_mod, b_mod):
    depth, d, n = w_mod.shape
    rows = conds.shape[0]
    tn = 1536
    return pl.pallas_call(
        _mod_kernel,
        grid=(depth, n // tn),
        in_specs=[_full((rows, d)),
                  pl.BlockSpec((1, d, tn), lambda l, j: (l, 0, j)),
                  pl.BlockSpec((1, 1, tn), lambda l, j: (l, 0, j))],
        out_specs=pl.BlockSpec((1, rows, tn), lambda l, j: (l, 0, j)),
        out_shape=jax.ShapeDtypeStruct((depth, rows, n), F32),
        compiler_params=_params("arbitrary", "arbitrary"),
        name="modulation",
    )(conds, w_mod, b_mod.reshape(depth, 1, n))


def _ab_in_kernel(use_rope, *refs):
    (x_ref, mod_ref, g0_ref, win_ref, gq_ref, gkv_ref, wq_ref, wk_ref, wv_ref, e_ref) = refs[:10]
    refs = refs[10:]
    if use_rope:
        cq, seq, soq, cm, sem, som = refs[:6]
        refs = refs[6:]
    (q_out, kcat_out, v_out, ckv_out, misc_out, gq_out, gk_out, gv_out, gout_out) = refs

    mod = mod_ref[0]
    shift, scale = mod[:, 0:D_MODEL], mod[:, D_MODEL:2 * D_MODEL]
    h = _rms(x_ref[...], g0_ref[...]) * (1.0 + scale) + shift
    proj = _dot(h.astype(BF16), win_ref[...])

    gq_out[...] = (proj[:, _OFF_GQ:_OFF_GK] * (GLA_DK ** -0.5)).astype(BF16)
    gk_out[...] = proj[:, _OFF_GK:_OFF_GV].astype(BF16)
    gv_out[...] = proj[:, _OFF_GV:_OFF_GOUT].astype(BF16)
    gout_out[...] = proj[:, _OFF_GOUT:_OFF_MISC].astype(BF16)

    misc = proj[:, _OFF_MISC:_AB_COLS]
    if use_rope:
        misc = _rope(misc, cm[...], sem[...], som[...])
    misc_out[...] = misc

    ckv = _rms(proj[:, _OFF_KVLAT:_OFF_GQ], gkv_ref[...])
    ckv_out[...] = ckv
    ckv_b = ckv.astype(BF16)
    kcat = (_dot(ckv_b, wk_ref[...]) + _dot(misc.astype(BF16), e_ref[...])).astype(BF16)
    v = _dot(ckv_b, wv_ref[...]).astype(BF16)
    for hp in range(MLA_HEADS // 2):
        v_out[0, hp] = v[:, hp * LANES:(hp + 1) * LANES]

    qn = _rms(proj[:, _OFF_QLAT:_OFF_KVLAT], gq_ref[...]).astype(BF16)
    q = _dot(qn, wq_ref[...])
    sm_scale = (MLA_NOPE + MLA_ROPE) ** -0.5 * LOG2E
    for hd in range(MLA_HEADS):
        sl = slice(hd * LANES, (hd + 1) * LANES)
        kcat_out[0, hd] = kcat[:, sl]
        qh = q[:, sl]
        if use_rope:
            qh = _rope(qh, cq[...], seq[...], soq[...])
        q_out[0, hd] = (qh * sm_scale).astype(BF16)


def _ab_in(x, mod3, modrow, g0, w, rope, tm, seq_len):
    t = x.shape[0]
    nt = t // tm
    tok = lambda n: pl.BlockSpec((tm, n), lambda i: (i, 0))
    in_specs = [tok(D_MODEL),
                pl.BlockSpec((1, 1, 6 * D_MODEL), lambda i: (modrow(i), 0, 0)),
                _full((1, D_MODEL)), _full((D_MODEL, _AB_COLS)),
                _full((1, MLA_Q_LORA)), _full((1, MLA_KV_LORA)),
                _full((MLA_Q_LORA, MLA_HEADS * LANES)), _full((MLA_KV_LORA, MLA_HEADS * LANES)),
                _full((MLA_KV_LORA, MLA_HEADS * MLA_V)), _full((LANES, MLA_HEADS * LANES))]
    args = [x, mod3, g0, w["win"], w["g_q"], w["g_kv"], w["wq"], w["wk"], w["wv"], w["e"]]
    per = seq_len // tm
    if rope is not None:
        in_specs += [pl.BlockSpec((tm, LANES), lambda i: (i % per, 0))] * 6
        args += list(rope)
    nb = t // seq_len
    heads = lambda nh: pl.BlockSpec((1, nh, tm, LANES), lambda i: (i // per, 0, i % per, 0))
    head_shape = lambda nh: jax.ShapeDtypeStruct((nb, nh, seq_len, LANES), BF16)
    widths = [(MLA_KV_LORA, F32), (LANES, F32)] + [(GLA_HEADS * GLA_DK, BF16)] * 4
    return pl.pallas_call(
        functools.partial(_ab_in_kernel, rope is not None),
        grid=(nt,),
        in_specs=in_specs,
        out_specs=[heads(MLA_HEADS), heads(MLA_HEADS), heads(MLA_HEADS // 2)] + [tok(n) for n, _ in widths],
        out_shape=[head_shape(MLA_HEADS), head_shape(MLA_HEADS), head_shape(MLA_HEADS // 2)]
        + [jax.ShapeDtypeStruct((t, n), dt) for n, dt in widths],
        compiler_params=_params("arbitrary"),
        name="ab_in",
    )(*args)


def _ctx_kv_kernel(ckv_ref, kr_ref, wk_ref, wv_ref, e_ref, kcat_out, v_out):
    c = ckv_ref[...].astype(BF16)
    kr = kr_ref[...].astype(BF16)
    kcat = (_dot(c, wk_ref[...]) + _dot(kr, e_ref[...])).astype(BF16)
    v = _dot(c, wv_ref[...]).astype(BF16)
    for hd in range(MLA_HEADS):
        kcat_out[0, hd] = kcat[:, hd * LANES:(hd + 1) * LANES]
    for hp in range(MLA_HEADS // 2):
        v_out[0, hp] = v[:, hp * LANES:(hp + 1) * LANES]


def _ctx_kv(ckv, krope, w):
    b, n_ctx, _ = ckv.shape
    tok = lambda n: pl.BlockSpec((None, n_ctx, n), lambda i: (i, 0, 0))
    heads = lambda nh: pl.BlockSpec((1, nh, n_ctx, LANES), lambda i: (i, 0, 0, 0))
    return pl.pallas_call(
        _ctx_kv_kernel,
        grid=(b,),
        in_specs=[tok(MLA_KV_LORA), tok(MLA_ROPE),
                  _full((MLA_KV_LORA, MLA_HEADS * LANES)), _full((MLA_KV_LORA, MLA_HEADS * MLA_V)),
                  _full((MLA_ROPE, MLA_HEADS * LANES))],
        out_specs=[heads(MLA_HEADS), heads(MLA_HEADS // 2)],
        out_shape=[jax.ShapeDtypeStruct((b, MLA_HEADS, n_ctx, LANES), BF16),
                   jax.ShapeDtypeStruct((b, MLA_HEADS // 2, n_ctx, LANES), BF16)],
        compiler_params=_params("arbitrary"),
        name="mla_ctx_kv",
    )(ckv, krope, w["wk"], w["wv"], w["e"][:MLA_ROPE])


def _two_map_flash(q0, q1, sources):
    state = [None, None]
    for k0_ref, k1_ref, v_ref in sources:
        rows = v_ref.shape[0]
        tk = min(ATTN_K_TILE, rows)
        ones = jnp.ones((tk, LANES), BF16)
        for c in range(rows // tk):
            r = slice(c * tk, (c + 1) * tk)
            v = jnp.concatenate([v_ref[r, :], ones], axis=1)
            for j, (q, k_ref) in enumerate(((q0, k0_ref), (q1, k1_ref))):
                s = _dot_nt(q, k_ref[r, :])
                m_blk = jnp.max(s, axis=-1, keepdims=True)
                if state[j] is None:
                    p = jnp.exp2(s - m_blk)
                    state[j] = (m_blk, _dot(p.astype(BF16), v))
                else:
                    m, acc = state[j]
                    m_new = jnp.maximum(m, m_blk)
                    p = jnp.exp2(s - m_new)
                    state[j] = (m_new, jnp.exp2(m - m_new) * acc + _dot(p.astype(BF16), v))
    return tuple((acc[:, LANES:], acc[:, :LANES]) for _, acc in state)


def _mla_attn_kernel(has_ctx, *refs):
    if has_ctx:
        q_ref, kn, vn, kc, vc, o_ref = refs
    else:
        q_ref, kn, vn, o_ref = refs
    for i in range(vn.shape[0]):
        e, o = 2 * i, 2 * i + 1
        sources = [(kn.at[e], kn.at[o], vn.at[i])]
        if has_ctx:
            sources.insert(0, (kc.at[e], kc.at[o], vc.at[i]))
        (l0, acc0), (l1, acc1) = _two_map_flash(q_ref[e], q_ref[o], sources)
        lane = lax.broadcasted_iota(jnp.int32, acc0.shape, 1)
        o_ref[:, i * LANES:(i + 1) * LANES] = jnp.where(
            lane < MLA_V, acc0 * (1.0 / l0), acc1 * (1.0 / l1)).astype(BF16)


def _mla_attn(q, k_new, v_new, ctx, tq, pairs_per_step):
    b, nh, n, _ = q.shape
    pp = pairs_per_step
    slab = lambda heads, rows, f: pl.BlockSpec((None, heads, rows, LANES), f)
    whole = lambda bi, p, qi: (bi, p, 0, 0)
    in_specs = [slab(2 * pp, tq, lambda bi, p, qi: (bi, p, qi, 0)), slab(2 * pp, n, whole), slab(pp, n, whole)]
    args = [q, k_new, v_new]
    if ctx is not None:
        nc = ctx[0].shape[2]
        in_specs += [slab(2 * pp, nc, whole), slab(pp, nc, whole)]
        args += list(ctx)
    return pl.pallas_call(
        functools.partial(_mla_attn_kernel, ctx is not None),
        grid=(b, nh // (2 * pp), n // tq),
        in_specs=in_specs,
        out_specs=pl.BlockSpec((None, tq, pp * LANES), lambda bi, p, qi: (bi, qi, p)),
        out_shape=jax.ShapeDtypeStruct((b, n, (nh // 2) * LANES), BF16),
        compiler_params=_params("arbitrary", "arbitrary", "arbitrary"),
        name="mla_attn",
    )(*args)


def _gla_kernel(has_s0, lookahead, *refs):
    fwd_in, bwd_in = refs[0:4], refs[4:8]
    refs = refs[8:]
    if lookahead:
        next_misc = refs[:2]
        refs = refs[2:]
    wg_ref, bg_ref, tf_ref, tb_ref = refs[:4]
    refs = refs[4:]
    if has_s0:
        s0f_ref, s0b_ref = refs[:2]
        refs = refs[2:]
    of_ref, ob_ref, sf_ref, sb_ref = refs[:4]
    tb_rows = of_ref.shape[1]
    nch = tb_rows // GLA_CHUNK
    n_bb = of_ref.shape[0]
    t_refs = (tf_ref, tb_ref)

    def log_decay(m_ref, d, bb):
        pre = _dot(m_ref[bb].astype(BF16), wg_ref[d]) + bg_ref[d]
        g = _log_sigmoid(pre) * (1.0 / GLA_GATE_NORM)
        g_hi = g.astype(BF16)
        g_lo = (g - g_hi.astype(F32)).astype(BF16)
        t = t_refs[d][...]
        return _dot(t, g_hi) + _dot(t, g_lo)

    @pl.when(pl.program_id(1) == 0)
    def _():
        if has_s0:
            sf_ref[...] = s0f_ref[...]
            sb_ref[...] = s0b_ref[...]
        else:
            sf_ref[...] = jnp.zeros(sf_ref.shape, F32)
            sb_ref[...] = jnp.zeros(sb_ref.shape, F32)
        if lookahead:
            cum_scr = refs[4]
            for d, m_ref in enumerate((fwd_in[3], bwd_in[3])):
                for bb in range(n_bb):
                    cum_scr[d, bb] = log_decay(m_ref, d, bb)

    scans = []
    for d, ((q_ref, k_ref, v_ref, m_ref), o_ref, s_ref) in enumerate(
            ((fwd_in, of_ref, sf_ref), (bwd_in, ob_ref, sb_ref))):
        row = lax.broadcasted_iota(jnp.int32, (tb_rows, tb_rows), 0)
        col = lax.broadcasted_iota(jnp.int32, (tb_rows, tb_rows), 1)
        chunk_lo = (row // GLA_CHUNK) * GLA_CHUNK
        if d == 0:
            tmask = (col <= row) & (col >= chunk_lo)
        else:
            tmask = (col >= row) & (col < chunk_lo + GLA_CHUNK)
        for bb in range(n_bb):
            cum = refs[4][d, bb] if lookahead else log_decay(m_ref, d, bb)
            for hd in range(GLA_HEADS):
                sl = slice(hd * LANES, (hd + 1) * LANES)
                scans.append(dict(d=d, bb=bb, hd=hd, sl=sl, cs=cum[:, sl], tmask=tmask,
                                  q_ref=q_ref, k_ref=k_ref, v_ref=v_ref, o_ref=o_ref, s_ref=s_ref))

    for sc in scans:
        cs, bb, sl = sc["cs"], sc["bb"], sc["sl"]
        k = sc["k_ref"][bb, :, sl].astype(F32)
        v = sc["v_ref"][bb, :, sl]
        sc["q_in"] = (sc["q_ref"][bb, :, sl].astype(F32) * jnp.exp(cs)).astype(BF16)
        k_neg = k * jnp.exp(-cs)
        a = jnp.where(sc["tmask"], _dot_nt(sc["q_in"], k_neg.astype(BF16)), 0.0)
        sc["o_intra"] = _dot(a.astype(BF16), v)
        sc["decay"], sc["u_t"] = [], []
        for n in range(nch):
            r = slice(n * GLA_CHUNK, (n + 1) * GLA_CHUNK)
            edge = (n + 1) * GLA_CHUNK - 1 if sc["d"] == 0 else n * GLA_CHUNK
            decay = jnp.exp(cs[edge:edge + 1, :])
            k_dec = (k_neg[r] * decay).astype(BF16)
            sc["decay"].append(decay)
            sc["u_t"].append(_dot_tn(v[r], k_dec))
        sc["s_t"] = sc["s_ref"][bb, sc["hd"]]

    for step in range(nch):
        for sc in scans:
            n = step if sc["d"] == 0 else nch - 1 - step
            r = slice(n * GLA_CHUNK, (n + 1) * GLA_CHUNK)
            sc["o_ref"][sc["bb"], r, sc["sl"]] = (
                sc["o_intra"][r] + _dot_nt(sc["q_in"][r], sc["s_t"].astype(BF16)))
            sc["s_t"] = sc["s_t"] * sc["decay"][n] + sc["u_t"][n]

    for sc in scans:
        sc["s_ref"][sc["bb"], sc["hd"]] = sc["s_t"]

    if lookahead:
        for d, m_ref in enumerate(next_misc):
            for bb in range(n_bb):
                refs[4][d, bb] = log_decay(m_ref, d, bb)


def _gla_masks(tb_rows):
    idx = np.arange(tb_rows)
    same = (idx[:, None] // GLA_CHUNK) == (idx[None, :] // GLA_CHUNK)
    fwd = same & (idx[None, :] <= idx[:, None])
    bwd = same & (idx[None, :] >= idx[:, None])
    return jnp.asarray(fwd, dtype=BF16), jnp.asarray(bwd, dtype=BF16)


def _gla(gq, gk, gv, misc, w, s0, tb_rows):
    b, n, width = gq.shape
    nb = n // tb_rows
    gb = GLA_BATCH
    fw = lambda cols: pl.BlockSpec((gb, tb_rows, cols), lambda bi, i: (bi, i, 0))
    bw = lambda cols: pl.BlockSpec((gb, tb_rows, cols), lambda bi, i: (bi, nb - 1 - i, 0))
    st = pl.BlockSpec((gb, GLA_HEADS, GLA_DV, GLA_DK), lambda bi, i: (bi, 0, 0, 0))
    tf, tb = _gla_masks(tb_rows)
    lookahead = nb > 1
    in_specs = [fw(width)] * 3 + [fw(LANES)] + [bw(width)] * 3 + [bw(LANES)]
    args = [gq, gk, gv, misc, gq, gk, gv, misc]
    if lookahead:
        in_specs += [pl.BlockSpec((gb, tb_rows, LANES), lambda bi, i: (bi, jnp.minimum(i + 1, nb - 1), 0)),
                     pl.BlockSpec((gb, tb_rows, LANES), lambda bi, i: (bi, jnp.maximum(nb - 2 - i, 0), 0))]
        args += [misc, misc]
    in_specs += [_full((2, LANES, width)), _full((2, 1, width)),
                 _full((tb_rows, tb_rows)), _full((tb_rows, tb_rows))]
    args += [w["wg"], w["bg"], tf, tb]
    if s0 is not None:
        in_specs += [st, st]
        args += list(s0)
    scratch = [pltpu.VMEM((2, gb, tb_rows, width), F32)] if lookahead else []
    return pl.pallas_call(
        functools.partial(_gla_kernel, s0 is not None, lookahead),
        grid=(b // gb, nb),
        in_specs=in_specs,
        out_specs=[fw(width), bw(width), st, st],
        out_shape=[jax.ShapeDtypeStruct((b, n, width), F32)] * 2
        + [jax.ShapeDtypeStruct((b, GLA_HEADS, GLA_DV, GLA_DK), F32)] * 2,
        scratch_shapes=scratch,
        compiler_params=_params("arbitrary", "arbitrary"),
        name="gla",
    )(*args)


def _layer_tail_kernel(has_gla, *refs):
    if has_gla:
        mix_ref, of_ref, ob_ref, gout_ref, ggo_ref = refs[:5]
        refs = refs[5:]
    else:
        mix_ref = refs[0]
        refs = refs[1:]
    (wmix_ref, x_ref, mod_ref, g_ref, wg_ref, wu_ref, wo_ref, o_ref, h_scr, x1_scr, acc_scr) = refs
    j = pl.program_id(1)
    last = pl.num_programs(1) - 1

    def ffn_slice(hb):
        act = (_silu(_dot(hb, wg_ref[...])) * _dot(hb, wu_ref[...])).astype(BF16)
        return _dot(act, wo_ref[...])

    @pl.when(j == 0)
    def _():
        mod = mod_ref[0]
        if has_gla:
            o = of_ref[...] + ob_ref[...]
            parts = []
            for hd in range(GLA_HEADS):
                sl = slice(hd * LANES, (hd + 1) * LANES)
                gate = gout_ref[:, sl].astype(F32)
                parts.append((_rms(o[:, sl], ggo_ref[...]) * _silu(gate)).astype(BF16))
            n_mla = mix_ref.shape[1]
            out = (_dot(mix_ref[...], wmix_ref[0:n_mla, :])
                   + _dot(jnp.concatenate(parts, axis=-1), wmix_ref[n_mla:, :]))
        else:
            out = _dot(mix_ref[...], wmix_ref[...])
        gate1 = mod[:, 2 * D_MODEL:3 * D_MODEL]
        x1 = x_ref[...] + gate1 * _rms(out, g_ref[1:2, :])
        x1_scr[...] = x1
        shift, scale = mod[:, 3 * D_MODEL:4 * D_MODEL], mod[:, 4 * D_MODEL:5 * D_MODEL]
        hb = (_rms(x1, g_ref[2:3, :]) * (1.0 + scale) + shift).astype(BF16)
        h_scr[...] = hb
        acc_scr[...] = ffn_slice(hb)

    if FFN_SPLIT > 2:
        @pl.when((j > 0) & (j < last))
        def _():
            acc_scr[...] += ffn_slice(h_scr[...])

    @pl.when(j == last)
    def _():
        f = acc_scr[...] + ffn_slice(h_scr[...])
        gate2 = mod_ref[0][:, 5 * D_MODEL:6 * D_MODEL]
        o_ref[...] = x1_scr[...] + gate2 * _rms(f, g_ref[3:4, :])


def _layer_tail(mix, gla, w_mix, x, mod3, modrow, g, w_in, w_out, tm):
    t = x.shape[0]
    th = FFN_HIDDEN // FFN_SPLIT
    tok = lambda n: pl.BlockSpec((tm, n), lambda i, j: (i, 0))
    const = lambda shape: pl.BlockSpec(shape, lambda i, j: (0,) * len(shape))
    in_specs = [tok(mix.shape[1])]
    args = [mix]
    if gla is not None:
        o_f, o_b, gout, g_gla = gla
        in_specs += [tok(o_f.shape[1])] * 3 + [const((1, GLA_DV))]
        args += [o_f, o_b, gout, g_gla]
    in_specs += [const(w_mix.shape), tok(D_MODEL),
                 pl.BlockSpec((1, 1, 6 * D_MODEL), lambda i, j: (modrow(i), 0, 0)), const(g.shape),
                 pl.BlockSpec((D_MODEL, th), lambda i, j: (0, j)),
                 pl.BlockSpec((D_MODEL, th), lambda i, j: (0, j + FFN_SPLIT)),
                 pl.BlockSpec((th, D_MODEL), lambda i, j: (j, 0))]
    args += [w_mix, x, mod3, g, w_in, w_in, w_out]
    return pl.pallas_call(
        functools.partial(_layer_tail_kernel, gla is not None),
        grid=(t // tm, FFN_SPLIT),
        in_specs=in_specs,
        out_specs=tok(D_MODEL),
        out_shape=jax.ShapeDtypeStruct((t, D_MODEL), F32),
        scratch_shapes=[pltpu.VMEM((tm, D_MODEL), BF16), pltpu.VMEM((tm, D_MODEL), F32),
                        pltpu.VMEM((tm, D_MODEL), F32)],
        compiler_params=_params("arbitrary", "arbitrary"),
        name="layer_tail",
    )(*args)


def _c_in_kernel(use_rope, want_f32, *refs):
    x_ref, mod_ref, g0_ref, w_ref = refs[:4]
    refs = refs[4:]
    if use_rope:
        c_ref, se_ref, so_ref = refs[:3]
        refs = refs[3:]
    q_out, k_out, v_out = refs[:3]
    mod = mod_ref[0]
    shift, scale = mod[:, 0:D_MODEL], mod[:, D_MODEL:2 * D_MODEL]
    h = _rms(x_ref[...], g0_ref[...]) * (1.0 + scale) + shift
    proj = _dot(h.astype(BF16), w_ref[...])
    width = DIFF_HEADS * LANES
    sm_scale = DIFF_HEAD_DIM ** -0.5 * LOG2E
    for hd in range(DIFF_HEADS):
        sl = slice(hd * LANES, (hd + 1) * LANES)
        qh = proj[:, hd * LANES:(hd + 1) * LANES]
        kh = proj[:, width + hd * LANES:width + (hd + 1) * LANES]
        vh = proj[:, 2 * width + hd * LANES:2 * width + (hd + 1) * LANES]
        if use_rope:
            qh = _rope(qh, c_ref[...], se_ref[...], so_ref[...])
            kh = _rope(kh, c_ref[...], se_ref[...], so_ref[...])
        q_out[0, hd] = (qh * sm_scale).astype(BF16)
        k_out[0, hd] = kh.astype(BF16)
        v_out[0, hd] = vh.astype(BF16)
        if want_f32:
            refs[3][:, sl] = kh
            refs[4][:, sl] = vh


def _c_in(x, mod3, modrow, g0, w_qkv, rope, want_f32, tm, seq_len):
    t = x.shape[0]
    width = DIFF_HEADS * LANES
    tok = lambda n: pl.BlockSpec((tm, n), lambda i: (i, 0))
    in_specs = [tok(D_MODEL), pl.BlockSpec((1, 1, 6 * D_MODEL), lambda i: (modrow(i), 0, 0)),
                _full((1, D_MODEL)), _full((D_MODEL, 3 * width))]
    args = [x, mod3, g0, w_qkv]
    per = seq_len // tm
    nb = t // seq_len
    if rope is not None:
        in_specs += [pl.BlockSpec((tm, LANES), lambda i: (i % per, 0))] * 3
        args += list(rope)
    heads = pl.BlockSpec((1, DIFF_HEADS, tm, LANES), lambda i: (i // per, 0, i % per, 0))
    head_shape = jax.ShapeDtypeStruct((nb, DIFF_HEADS, seq_len, LANES), BF16)
    n_f32 = 2 if want_f32 else 0
    return pl.pallas_call(
        functools.partial(_c_in_kernel, rope is not None, want_f32),
        grid=(t // tm,),
        in_specs=in_specs,
        out_specs=[heads] * 3 + [tok(width)] * n_f32,
        out_shape=[head_shape] * 3 + [jax.ShapeDtypeStruct((t, width), F32)] * n_f32,
        compiler_params=_params("arbitrary"),
        name="c_in",
    )(*args)


def _diff_attn_kernel(has_ctx, lam_init, *refs):
    if has_ctx:
        q_ref, kn, vn, kc, vc, lam_ref, g_ref, o_ref = refs
    else:
        q_ref, kn, vn, lam_ref, g_ref, o_ref = refs
    lp = lam_ref[...]
    lam = (jnp.exp(jnp.sum(lp[0:1] * lp[1:2], axis=-1, keepdims=True))
           - jnp.exp(jnp.sum(lp[2:3] * lp[3:4], axis=-1, keepdims=True)) + lam_init)
    for i in range(q_ref.shape[0]):
        k_new, k_ctx = kn.at[i], (kc.at[i] if has_ctx else None)
        sources = [(k_new, k_new, vn.at[i])]
        if has_ctx:
            sources.insert(0, (k_ctx, k_ctx, vc.at[i]))
        qh = q_ref[i]
        lane = lax.broadcasted_iota(jnp.int32, qh.shape, 1)
        zero = jnp.zeros_like(qh)
        q0 = jnp.where(lane < DIFF_HEAD_DIM, qh, zero)
        q1 = jnp.where(lane < DIFF_HEAD_DIM, zero, qh)
        (l0, acc0), (l1, acc1) = _two_map_flash(q0, q1, sources)
        o = acc0 * (1.0 / l0) - lam * (acc1 * (1.0 / l1))
        o_ref[:, i * LANES:(i + 1) * LANES] = (_rms(o, g_ref[...]) * (1.0 - lam_init)).astype(BF16)


def _diff_attn(q, k_new, v_new, ctx, lam_p, g_out, lam_init, tq, heads_per_step):
    b, nh, n, _ = q.shape
    hps = heads_per_step
    slab = lambda rows, f: pl.BlockSpec((None, hps, rows, LANES), f)
    whole = lambda bi, h, qi: (bi, h, 0, 0)
    in_specs = [slab(tq, lambda bi, h, qi: (bi, h, qi, 0)), slab(n, whole), slab(n, whole)]
    args = [q, k_new, v_new]
    if ctx is not None:
        in_specs += [slab(ctx[0].shape[2], whole)] * 2
        args += list(ctx)
    in_specs += [_full(lam_p.shape), _full(g_out.shape)]
    args += [lam_p, g_out]
    return pl.pallas_call(
        functools.partial(_diff_attn_kernel, ctx is not None, lam_init),
        grid=(b, nh // hps, n // tq),
        in_specs=in_specs,
        out_specs=pl.BlockSpec((None, tq, hps * LANES), lambda bi, h, qi: (bi, qi, h)),
        out_shape=jax.ShapeDtypeStruct((b, n, nh * LANES), BF16),
        compiler_params=_params("arbitrary", "arbitrary", "arbitrary"),
        name="diff_attn",
    )(*args)


def _prep_ab_weights(ai, ab_w_in, mla_g_q, mla_g_kv, mla_w_uq, mla_w_ukv, gla_w_gate_up, gla_b_gate):
    w = ab_w_in[ai]
    d = w.shape[0]
    o = np.cumsum([0, MLA_Q_LORA, MLA_KV_LORA, MLA_ROPE, GLA_HEADS * GLA_DK, GLA_HEADS * GLA_DK,
                   GLA_HEADS * GLA_DV, 2 * GLA_GATE_RANK, GLA_HEADS * GLA_DV])
    q_lat, kv_lat, k_rope, gq, gk, gv, ggate, gout = (w[:, o[i]:o[i + 1]] for i in range(8))
    pad = jnp.zeros((d, LANES - MLA_ROPE - 2 * GLA_GATE_RANK), w.dtype)
    win = jnp.concatenate([q_lat, kv_lat, gq, gk, gv, gout, k_rope, ggate, pad], axis=1).astype(BF16)

    head_pad = LANES - MLA_NOPE - MLA_ROPE
    wq = mla_w_uq[ai].reshape(MLA_Q_LORA, MLA_HEADS, MLA_NOPE + MLA_ROPE)
    wq = jnp.pad(wq, ((0, 0), (0, 0), (0, head_pad))).reshape(MLA_Q_LORA, MLA_HEADS * LANES).astype(BF16)
    wkv = mla_w_ukv[ai].reshape(MLA_KV_LORA, MLA_HEADS, MLA_NOPE + MLA_V)
    wk = jnp.pad(wkv[:, :, :MLA_NOPE], ((0, 0), (0, 0), (0, LANES - MLA_NOPE)))
    wk = wk.reshape(MLA_KV_LORA, MLA_HEADS * LANES).astype(BF16)
    wv = wkv[:, :, MLA_NOPE:].reshape(MLA_KV_LORA, MLA_HEADS * MLA_V).astype(BF16)

    e = np.zeros((LANES, MLA_HEADS * LANES), np.float32)
    for hd in range(MLA_HEADS):
        e[np.arange(MLA_ROPE), hd * LANES + MLA_NOPE + np.arange(MLA_ROPE)] = 1.0

    wg = jnp.zeros((2, LANES, GLA_HEADS * GLA_DK), F32)
    for dr in range(2):
        lo = _MISC_GATE + dr * GLA_GATE_RANK
        wg = wg.at[dr, lo:lo + GLA_GATE_RANK, :].set(gla_w_gate_up[ai, dr])
    return dict(win=win, g_q=mla_g_q[ai][None, :], g_kv=mla_g_kv[ai][None, :], wq=wq, wk=wk, wv=wv,
                e=jnp.asarray(e, dtype=BF16), wg=wg.astype(BF16),
                bg=gla_b_gate[ai].reshape(2, 1, GLA_HEADS * GLA_DK))


def _diff_lambda_init(layer):
    return 0.8 - 0.6 * math.exp(-0.3 * layer)


def kernel(x_prompt, x_sample, cache_mla_ckv, cache_mla_krope, state_gla_fwd, state_gla_bwd,
           cache_diff_k, cache_diff_v, c, c_ctx, w_mod, b_mod, g_norm, ab_w_in, mla_g_q, mla_g_kv,
           mla_w_uq, mla_w_ukv, gla_w_gate_up, gla_b_gate, gla_g_out, ab_w_out, c_w_qkv, diff_lambda,
           diff_g_out, c_w_out, w_ffn_in, w_ffn_out):
    depth = w_mod.shape[0]
    bp, lp, d = x_prompt.shape
    bs, ls, _ = x_sample.shape
    n_ctx = cache_mla_ckv.shape[2]

    n_cond = 1 + bs
    rows = -(-n_cond // 8) * 8
    conds = jnp.concatenate([c_ctx[None, :], c, jnp.zeros((rows - n_cond, d), F32)], axis=0)
    mod3 = _modulation(conds, w_mod, b_mod).reshape(depth * rows, 1, 6 * d)

    per_s = ls // min(TOKEN_TILE, ls)
    rope_q = _rope_tables(ls, MLA_ROPE, MLA_NOPE, LANES)
    rope_misc = _rope_tables(ls, MLA_ROPE, 0, LANES)
    rope_diff = _rope_tables(ls, DIFF_HEAD_DIM, 0, DIFF_HEAD_DIM)

    groups = {
        "prompt": dict(x=x_prompt.reshape(bp * lp, d), b=bp, n=lp, rope=False,
                       modrow=lambda li: (lambda i: li * rows)),
        "sample": dict(x=x_sample.reshape(bs * ls, d), b=bs, n=ls, rope=True,
                       modrow=lambda li: (lambda i: li * rows + 1 + i // per_s)),
    }
    ab_states, c_states = [], []

    ab_w = [_prep_ab_weights(ai, ab_w_in, mla_g_q, mla_g_kv, mla_w_uq, mla_w_ukv, gla_w_gate_up, gla_b_gate)
            for ai in range(ab_w_in.shape[0])]
    ab_w_out_b = ab_w_out.astype(BF16)
    c_w_qkv_b = c_w_qkv.astype(BF16)
    c_w_out_b = c_w_out.astype(BF16)
    w_ffn_in_b = w_ffn_in.astype(BF16)
    w_ffn_out_b = w_ffn_out.astype(BF16)

    for name, grp in groups.items():
        x, b, n = grp["x"], grp["b"], grp["n"]
        is_sample = grp["rope"]
        tm = min(TOKEN_TILE, n)
        for li in range(depth):
            modrow = grp["modrow"](li)
            g = g_norm[li]
            if li % 2 == 0:
                ai = li // 2
                w = ab_w[ai]
                rope = rope_q + rope_misc if is_sample else None
                q, kcat, v, ckv, misc, gq, gk, gv, gout = _ab_in(x, mod3, modrow, g[0:1], w, rope, tm, n)
                sh = lambda a: a.reshape(b, n, a.shape[-1])
                if is_sample:
                    ctx = _ctx_kv(cache_mla_ckv[:, ai], cache_mla_krope[:, ai], w)
                    s0 = (jnp.swapaxes(state_gla_fwd[:, ai], -1, -2), jnp.swapaxes(state_gla_bwd[:, ai], -1, -2))
                else:
                    ctx, s0 = None, None
                mla_o = _mla_attn(q, kcat, v, ctx, min(ATTN_Q_TILE, n),
                                  max(1, min(MLA_HEADS // 2, ATTN_STEP_ROWS // n)))
                o_f, o_b, s_f, s_b = _gla(sh(gq), sh(gk), sh(gv), sh(misc), w, s0, min(GLA_BLOCK, n))
                if not is_sample:
                    ab_states.append((ckv.reshape(b, n, MLA_KV_LORA), misc[:, :MLA_ROPE].reshape(b, n, MLA_ROPE),
                                      jnp.swapaxes(s_f, -1, -2), jnp.swapaxes(s_b, -1, -2)))
                mix = mla_o.reshape(b * n, -1)
                gla = (o_f.reshape(b * n, -1), o_b.reshape(b * n, -1), gout, gla_g_out[ai][None, :])
                w_mix = ab_w_out_b[ai]
            else:
                ci = li // 2
                rope = rope_diff if is_sample else None
                outs = _c_in(x, mod3, modrow, g[0:1], c_w_qkv_b[ci], rope, not is_sample, tm, n)
                q, k, v = outs[:3]
                if is_sample:
                    head_major = lambda a: jnp.swapaxes(a.reshape(bs, n_ctx, DIFF_HEADS, LANES), 1, 2).astype(BF16)
                    ctx = (head_major(cache_diff_k[:, ci]), head_major(cache_diff_v[:, ci]))
                else:
                    ctx = None
                    c_states.append((outs[3].reshape(b, n, DIFF_HEADS, 2, DIFF_HEAD_DIM),
                                     outs[4].reshape(b, n, DIFF_HEADS, 2 * DIFF_HEAD_DIM)))
                o = _diff_attn(q, k, v, ctx, diff_lambda[ci], diff_g_out[ci][None, :],
                               _diff_lambda_init(li), min(ATTN_Q_TILE, n),
                               max(1, min(DIFF_HEADS, ATTN_STEP_ROWS // n)))
                mix, gla, w_mix = o.reshape(b * n, -1), None, c_w_out_b[ci]
            tm_tail = tm if is_sample else min(TOKEN_TILE, b * n)
            x = _layer_tail(mix, gla, w_mix, x, mod3, modrow, g, w_ffn_in_b[li], w_ffn_out_b[li], tm_tail)
        grp["y"] = x.reshape(b, n, d)

    stack = lambda states, i: jnp.stack([s[i] for s in states], axis=1)
    return (groups["prompt"]["y"], groups["sample"]["y"],
            stack(ab_states, 0), stack(ab_states, 1), stack(ab_states, 2), stack(ab_states, 3),
            stack(c_states, 0), stack(c_states, 1))
```

```python
import functools
import math

import numpy as np
import jax
import jax.numpy as jnp
from jax import lax
from jax.experimental import pallas as pl
from jax.experimental.pallas import tpu as pltpu

F32 = jnp.float32
BF16 = jnp.bfloat16

D_MODEL = 1024
GRID_W = 64
ROPE_BASE = 10000.0
NORM_EPS = 1e-6

MLA_HEADS = 8
MLA_NOPE = 64
MLA_ROPE = 32
MLA_V = 64
MLA_Q_LORA = 384
MLA_KV_LORA = 256
GLA_HEADS = 4
GLA_DK = 128
GLA_DV = 128
GLA_GATE_RANK = 16
GLA_GATE_NORM = 16.0
GLA_CHUNK = 64
DIFF_HEADS = 8
DIFF_HEAD_DIM = 64
FFN_HIDDEN = 2816

LANES = 128
VMEM_LIMIT = 56 * 1024 * 1024

_OFF_QLAT = 0
_OFF_KVLAT = _OFF_QLAT + MLA_Q_LORA
_OFF_GQ = _OFF_KVLAT + MLA_KV_LORA
_OFF_GK = _OFF_GQ + GLA_HEADS * GLA_DK
_OFF_GV = _OFF_GK + GLA_HEADS * GLA_DK
_OFF_GOUT = _OFF_GV + GLA_HEADS * GLA_DV
_OFF_MISC = _OFF_GOUT + GLA_HEADS * GLA_DV
_AB_COLS = _OFF_MISC + LANES
_MISC_GATE = MLA_ROPE

LOG2E = math.log2(math.e)

TOKEN_TILE = 512
ATTN_Q_TILE = 1024
ATTN_K_TILE = 256
ATTN_STEP_ROWS = 2048
GLA_BLOCK = 256
GLA_BATCH = 2
FFN_SPLIT = 11


def _rms(x, g):
    var = jnp.mean(x * x, axis=-1, keepdims=True)
    return x * lax.rsqrt(var + NORM_EPS) * g


def _silu(x):
    return x * (1.0 / (1.0 + jnp.exp(-x)))


def _log_sigmoid(x):
    return -(jnp.maximum(-x, 0.0) + jnp.log1p(jnp.exp(-jnp.abs(x))))


def _rope(x, c, se, so):
    n = x.shape[-1]
    return x * c + pltpu.roll(x, n - 1, 1) * se + pltpu.roll(x, 1, 1) * so


def _dot(a, b):
    return jnp.dot(a, b, preferred_element_type=F32)


def _dot_nt(a, b):
    return lax.dot_general(a, b, (((1,), (1,)), ((), ())), preferred_element_type=F32)


def _dot_tn(a, b):
    return lax.dot_general(a, b, (((0,), (0,)), ((), ())), preferred_element_type=F32)


def _params(*sem):
    return pltpu.CompilerParams(dimension_semantics=sem, vmem_limit_bytes=VMEM_LIMIT)


def _full(shape):
    zeros = (0,) * len(shape)
    return pl.BlockSpec(shape, lambda *_: zeros)


def _rope_tables(n_tokens, rot_dim, lane_lo, period):
    rows = n_tokens // GRID_W
    row = np.repeat(np.arange(rows, dtype=np.float64), GRID_W)
    col = np.tile(np.arange(GRID_W, dtype=np.float64), rows)
    n_freq = rot_dim // 4
    inv = ROPE_BASE ** (-np.arange(n_freq, dtype=np.float64) / n_freq)
    ang = np.concatenate([row[:, None] * inv, col[:, None] * inv], axis=-1)
    cos, sin = np.cos(ang), np.sin(ang)
    c = np.ones((n_tokens, period))
    se = np.zeros((n_tokens, period))
    so = np.zeros((n_tokens, period))
    c[:, lane_lo:lane_lo + rot_dim] = np.repeat(cos, 2, axis=-1)
    se[:, lane_lo:lane_lo + rot_dim:2] = -sin
    so[:, lane_lo + 1:lane_lo + rot_dim:2] = sin
    reps = LANES // period
    return tuple(jnp.asarray(np.tile(t, (1, reps)), dtype=F32) for t in (c, se, so))


def _mod_kernel(c_ref, w_ref, b_ref, o_ref):
    a = _silu(c_ref[...]).astype(BF16)
    o_ref[0] = _dot(a, w_ref[0].astype(BF16)) + b_ref[0]


def _modulation(conds, w_mod, b_mod):
    depth, d, n = w_mod.shape
    rows = conds.shape[0]
    tn = 1536
    return pl.pallas_call(
        _mod_kernel,
        grid=(depth, n // tn),
        in_specs=[_full((rows, d)),
                  pl.BlockSpec((1, d, tn), lambda l, j: (l, 0, j)),
                  pl.BlockSpec((1, 1, tn), lambda l, j: (l, 0, j))],
        out_specs=pl.BlockSpec((1, rows, tn), lambda l, j: (l, 0, j)),
        out_shape=jax.ShapeDtypeStruct((depth, rows, n), F32),
        compiler_params=_params("arbitrary", "arbitrary"),
        name="modulation",
    )(conds, w_mod, b_mod.reshape(depth, 1, n))


def _ab_in_kernel(use_rope, *refs):
    (x_ref, mod_ref, g0_ref, win_ref, gq_ref, gkv_ref, wq_ref, wk_ref, wv_ref, e_ref) = refs[:10]
    refs = refs[10:]
    if use_rope:
        cq, seq, soq, cm, sem, som = refs[:6]
        refs = refs[6:]
    (q_out, kcat_out, v_out, ckv_out, misc_out, gq_out, gk_out, gv_out, gout_out) = refs

    mod = mod_ref[0]
    shift, scale = mod[:, 0:D_MODEL], mod[:, D_MODEL:2 * D_MODEL]
    h = _rms(x_ref[...], g0_ref[...]) * (1.0 + scale) + shift
    proj = _dot(h.astype(BF16), win_ref[...])

    gq_out[...] = (proj[:, _OFF_GQ:_OFF_GK] * (GLA_DK ** -0.5)).astype(BF16)
    gk_out[...] = proj[:, _OFF_GK:_OFF_GV].astype(BF16)
    gv_out[...] = proj[:, _OFF_GV:_OFF_GOUT].astype(BF16)
    gout_out[...] = proj[:, _OFF_GOUT:_OFF_MISC].astype(BF16)

    misc = proj[:, _OFF_MISC:_AB_COLS]
    if use_rope:
        misc = _rope(misc, cm[...], sem[...], som[...])
    misc_out[...] = misc

    ckv = _rms(proj[:, _OFF_KVLAT:_OFF_GQ], gkv_ref[...])
    ckv_out[...] = ckv
    ckv_b = ckv.astype(BF16)
    kcat = (_dot(ckv_b, wk_ref[...]) + _dot(misc.astype(BF16), e_ref[...])).astype(BF16)
    v = _dot(ckv_b, wv_ref[...]).astype(BF16)
    for hp in range(MLA_HEADS // 2):
        v_out[0, hp] = v[:, hp * LANES:(hp + 1) * LANES]

    qn = _rms(proj[:, _OFF_QLAT:_OFF_KVLAT], gq_ref[...]).astype(BF16)
    q = _dot(qn, wq_ref[...])
    sm_scale = (MLA_NOPE + MLA_ROPE) ** -0.5 * LOG2E
    for hd in range(MLA_HEADS):
        sl = slice(hd * LANES, (hd + 1) * LANES)
        kcat_out[0, hd] = kcat[:, sl]
        qh = q[:, sl]
        if use_rope:
            qh = _rope(qh, cq[...], seq[...], soq[...])
        q_out[0, hd] = (qh * sm_scale).astype(BF16)


def _ab_in(x, mod3, modrow, g0, w, rope, tm, seq_len):
    t = x.shape[0]
    nt = t // tm
    tok = lambda n: pl.BlockSpec((tm, n), lambda i: (i, 0))
    in_specs = [tok(D_MODEL),
                pl.BlockSpec((1, 1, 6 * D_MODEL), lambda i: (modrow(i), 0, 0)),
                _full((1, D_MODEL)), _full((D_MODEL, _AB_COLS)),
                _full((1, MLA_Q_LORA)), _full((1, MLA_KV_LORA)),
                _full((MLA_Q_LORA, MLA_HEADS * LANES)), _full((MLA_KV_LORA, MLA_HEADS * LANES)),
                _full((MLA_KV_LORA, MLA_HEADS * MLA_V)), _full((LANES, MLA_HEADS * LANES))]
    args = [x, mod3, g0, w["win"], w["g_q"], w["g_kv"], w["wq"], w["wk"], w["wv"], w["e"]]
    per = seq_len // tm
    if rope is not None:
        in_specs += [pl.BlockSpec((tm, LANES), lambda i: (i % per, 0))] * 6
        args += list(rope)
    nb = t // seq_len
    heads = lambda nh: pl.BlockSpec((1, nh, tm, LANES), lambda i: (i // per, 0, i % per, 0))
    head_shape = lambda nh: jax.ShapeDtypeStruct((nb, nh, seq_len, LANES), BF16)
    widths = [(MLA_KV_LORA, F32), (LANES, F32)] + [(GLA_HEADS * GLA_DK, BF16)] * 4
    return pl.pallas_call(
        functools.partial(_ab_in_kernel, rope is not None),
        grid=(nt,),
        in_specs=in_specs,
        out_specs=[heads(MLA_HEADS), heads(MLA_HEADS), heads(MLA_HEADS // 2)] + [tok(n) for n, _ in widths],
        out_shape=[head_shape(MLA_HEADS), head_shape(MLA_HEADS), head_shape(MLA_HEADS // 2)]
        + [jax.ShapeDtypeStruct((t, n), dt) for n, dt in widths],
        compiler_params=_params("arbitrary"),
        name="ab_in",
    )(*args)


def _ctx_kv_kernel(ckv_ref, kr_ref, wk_ref, wv_ref, e_ref, kcat_out, v_out):
    c = ckv_ref[...].astype(BF16)
    kr = kr_ref[...].astype(BF16)
    kcat = (_dot(c, wk_ref[...]) + _dot(kr, e_ref[...])).astype(BF16)
    v = _dot(c, wv_ref[...]).astype(BF16)
    for hd in range(MLA_HEADS):
        kcat_out[0, hd] = kcat[:, hd * LANES:(hd + 1) * LANES]
    for hp in range(MLA_HEADS // 2):
        v_out[0, hp] = v[:, hp * LANES:(hp + 1) * LANES]


def _ctx_kv(ckv, krope, w):
    b, n_ctx, _ = ckv.shape
    tok = lambda n: pl.BlockSpec((None, n_ctx, n), lambda i: (i, 0, 0))
    heads = lambda nh: pl.BlockSpec((1, nh, n_ctx, LANES), lambda i: (i, 0, 0, 0))
    return pl.pallas_call(
        _ctx_kv_kernel,
        grid=(b,),
        in_specs=[tok(MLA_KV_LORA), tok(MLA_ROPE),
                  _full((MLA_KV_LORA, MLA_HEADS * LANES)), _full((MLA_KV_LORA, MLA_HEADS * MLA_V)),
                  _full((MLA_ROPE, MLA_HEADS * LANES))],
        out_specs=[heads(MLA_HEADS), heads(MLA_HEADS // 2)],
        out_shape=[jax.ShapeDtypeStruct((b, MLA_HEADS, n_ctx, LANES), BF16),
                   jax.ShapeDtypeStruct((b, MLA_HEADS // 2, n_ctx, LANES), BF16)],
        compiler_params=_params("arbitrary"),
        name="mla_ctx_kv",
    )(ckv, krope, w["wk"], w["wv"], w["e"][:MLA_ROPE])


def _two_map_flash(q0, q1, sources):
    state = [None, None]
    for k0_ref, k1_ref, v_ref in sources:
        rows = v_ref.shape[0]
        tk = min(ATTN_K_TILE, rows)
        ones = jnp.ones((tk, LANES), BF16)
        for c in range(rows // tk):
            r = slice(c * tk, (c + 1) * tk)
            v = jnp.concatenate([v_ref[r, :], ones], axis=1)
            for j, (q, k_ref) in enumerate(((q0, k0_ref), (q1, k1_ref))):
                s = _dot_nt(q, k_ref[r, :])
                m_blk = jnp.max(s, axis=-1, keepdims=True)
                if state[j] is None:
                    p = jnp.exp2(s - m_blk)
                    state[j] = (m_blk, _dot(p.astype(BF16), v))
                else:
                    m, acc = state[j]
                    m_new = jnp.maximum(m, m_blk)
                    p = jnp.exp2(s - m_new)
                    state[j] = (m_new, jnp.exp2(m - m_new) * acc + _dot(p.astype(BF16), v))
    return tuple((acc[:, LANES:], acc[:, :LANES]) for _, acc in state)


def _mla_attn_kernel(has_ctx, *refs):
    if has_ctx:
        q_ref, kn, vn, kc, vc, o_ref = refs
    else:
        q_ref, kn, vn, o_ref = refs
    for i in range(vn.shape[0]):
        e, o = 2 * i, 2 * i + 1
        sources = [(kn.at[e], kn.at[o], vn.at[i])]
        if has_ctx:
            sources.insert(0, (kc.at[e], kc.at[o], vc.at[i]))
        (l0, acc0), (l1, acc1) = _two_map_flash(q_ref[e], q_ref[o], sources)
        lane = lax.broadcasted_iota(jnp.int32, acc0.shape, 1)
        o_ref[:, i * LANES:(i + 1) * LANES] = jnp.where(
            lane < MLA_V, acc0 * (1.0 / l0), acc1 * (1.0 / l1)).astype(BF16)


def _mla_attn(q, k_new, v_new, ctx, tq, pairs_per_step):
    b, nh, n, _ = q.shape
    pp = pairs_per_step
    slab = lambda heads, rows, f: pl.BlockSpec((None, heads, rows, LANES), f)
    whole = lambda bi, p, qi: (bi, p, 0, 0)
    in_specs = [slab(2 * pp, tq, lambda bi, p, qi: (bi, p, qi, 0)), slab(2 * pp, n, whole), slab(pp, n, whole)]
    args = [q, k_new, v_new]
    if ctx is not None:
        nc = ctx[0].shape[2]
        in_specs += [slab(2 * pp, nc, whole), slab(pp, nc, whole)]
        args += list(ctx)
    return pl.pallas_call(
        functools.partial(_mla_attn_kernel, ctx is not None),
        grid=(b, nh // (2 * pp), n // tq),
        in_specs=in_specs,
        out_specs=pl.BlockSpec((None, tq, pp * LANES), lambda bi, p, qi: (bi, qi, p)),
        out_shape=jax.ShapeDtypeStruct((b, n, (nh // 2) * LANES), BF16),
        compiler_params=_params("arbitrary", "arbitrary", "arbitrary"),
        name="mla_attn",
    )(*args)


def _gla_kernel(has_s0, lookahead, *refs):
    fwd_in, bwd_in = refs[0:4], refs[4:8]
    refs = refs[8:]
    if lookahead:
        next_misc = refs[:2]
        refs = refs[2:]
    wg_ref, bg_ref, tf_ref, tb_ref = refs[:4]
    refs = refs[4:]
    if has_s0:
        s0f_ref, s0b_ref = refs[:2]
        refs = refs[2:]
    of_ref, ob_ref, sf_ref, sb_ref = refs[:4]
    tb_rows = of_ref.shape[1]
    nch = tb_rows // GLA_CHUNK
    n_bb = of_ref.shape[0]
    t_refs = (tf_ref, tb_ref)

    def log_decay(m_ref, d, bb):
        pre = _dot(m_ref[bb].astype(BF16), wg_ref[d]) + bg_ref[d]
        g = _log_sigmoid(pre) * (1.0 / GLA_GATE_NORM)
        g_hi = g.astype(BF16)
        g_lo = (g - g_hi.astype(F32)).astype(BF16)
        t = t_refs[d][...]
        return _dot(t, g_hi) + _dot(t, g_lo)

    @pl.when(pl.program_id(1) == 0)
    def _():
        if has_s0:
            sf_ref[...] = s0f_ref[...]
            sb_ref[...] = s0b_ref[...]
        else:
            sf_ref[...] = jnp.zeros(sf_ref.shape, F32)
            sb_ref[...] = jnp.zeros(sb_ref.shape, F32)
        if lookahead:
            cum_scr = refs[4]
            for d, m_ref in enumerate((fwd_in[3], bwd_in[3])):
                for bb in range(n_bb):
                    cum_scr[d, bb] = log_decay(m_ref, d, bb)

    scans = []
    for d, ((q_ref, k_ref, v_ref, m_ref), o_ref, s_ref) in enumerate(
            ((fwd_in, of_ref, sf_ref), (bwd_in, ob_ref, sb_ref))):
        row = lax.broadcasted_iota(jnp.int32, (tb_rows, tb_rows), 0)
        col = lax.broadcasted_iota(jnp.int32, (tb_rows, tb_rows), 1)
        chunk_lo = (row // GLA_CHUNK) * GLA_CHUNK
        if d == 0:
            tmask = (col <= row) & (col >= chunk_lo)
        else:
            tmask = (col >= row) & (col < chunk_lo + GLA_CHUNK)
        for bb in range(n_bb):
            cum = refs[4][d, bb] if lookahead else log_decay(m_ref, d, bb)
            for hd in range(GLA_HEADS):
                sl = slice(hd * LANES, (hd + 1) * LANES)
                scans.append(dict(d=d, bb=bb, hd=hd, sl=sl, cs=cum[:, sl], tmask=tmask,
                                  q_ref=q_ref, k_ref=k_ref, v_ref=v_ref, o_ref=o_ref, s_ref=s_ref))

    for sc in scans:
        cs, bb, sl = sc["cs"], sc["bb"], sc["sl"]
        k = sc["k_ref"][bb, :, sl].astype(F32)
        v = sc["v_ref"][bb, :, sl]
        sc["q_in"] = (sc["q_ref"][bb, :, sl].astype(F32) * jnp.exp(cs)).astype(BF16)
        k_neg = k * jnp.exp(-cs)
        a = jnp.where(sc["tmask"], _dot_nt(sc["q_in"], k_neg.astype(BF16)), 0.0)
        sc["o_intra"] = _dot(a.astype(BF16), v)
        sc["decay"], sc["u_t"] = [], []
        for n in range(nch):
            r = slice(n * GLA_CHUNK, (n + 1) * GLA_CHUNK)
            edge = (n + 1) * GLA_CHUNK - 1 if sc["d"] == 0 else n * GLA_CHUNK
            decay = jnp.exp(cs[edge:edge + 1, :])
            k_dec = (k_neg[r] * decay).astype(BF16)
            sc["decay"].append(decay)
            sc["u_t"].append(_dot_tn(v[r], k_dec))
        sc["s_t"] = sc["s_ref"][bb, sc["hd"]]

    for step in range(nch):
        for sc in scans:
            n = step if sc["d"] == 0 else nch - 1 - step
            r = slice(n * GLA_CHUNK, (n + 1) * GLA_CHUNK)
            sc["o_ref"][sc["bb"], r, sc["sl"]] = (
                sc["o_intra"][r] + _dot_nt(sc["q_in"][r], sc["s_t"].astype(BF16)))
            sc["s_t"] = sc["s_t"] * sc["decay"][n] + sc["u_t"][n]

    for sc in scans:
        sc["s_ref"][sc["bb"], sc["hd"]] = sc["s_t"]

    if lookahead:
        for d, m_ref in enumerate(next_misc):
            for bb in range(n_bb):
                refs[4][d, bb] = log_decay(m_ref, d, bb)


def _gla_masks(tb_rows):
    idx = np.arange(tb_rows)
    same = (idx[:, None] // GLA_CHUNK) == (idx[None, :] // GLA_CHUNK)
    fwd = same & (idx[None, :] <= idx[:, None])
    bwd = same & (idx[None, :] >= idx[:, None])
    return jnp.asarray(fwd, dtype=BF16), jnp.asarray(bwd, dtype=BF16)


def _gla(gq, gk, gv, misc, w, s0, tb_rows):
    b, n, width = gq.shape
    nb = n // tb_rows
    gb = GLA_BATCH
    fw = lambda cols: pl.BlockSpec((gb, tb_rows, cols), lambda bi, i: (bi, i, 0))
    bw = lambda cols: pl.BlockSpec((gb, tb_rows, cols), lambda bi, i: (bi, nb - 1 - i, 0))
    st = pl.BlockSpec((gb, GLA_HEADS, GLA_DV, GLA_DK), lambda bi, i: (bi, 0, 0, 0))
    tf, tb = _gla_masks(tb_rows)
    lookahead = nb > 1
    in_specs = [fw(width)] * 3 + [fw(LANES)] + [bw(width)] * 3 + [bw(LANES)]
    args = [gq, gk, gv, misc, gq, gk, gv, misc]
    if lookahead:
        in_specs += [pl.BlockSpec((gb, tb_rows, LANES), lambda bi, i: (bi, jnp.minimum(i + 1, nb - 1), 0)),
                     pl.BlockSpec((gb, tb_rows, LANES), lambda bi, i: (bi, jnp.maximum(nb - 2 - i, 0), 0))]
        args += [misc, misc]
    in_specs += [_full((2, LANES, width)), _full((2, 1, width)),
                 _full((tb_rows, tb_rows)), _full((tb_rows, tb_rows))]
    args += [w["wg"], w["bg"], tf, tb]
    if s0 is not None:
        in_specs += [st, st]
        args += list(s0)
    scratch = [pltpu.VMEM((2, gb, tb_rows, width), F32)] if lookahead else []
    return pl.pallas_call(
        functools.partial(_gla_kernel, s0 is not None, lookahead),
        grid=(b // gb, nb),
        in_specs=in_specs,
        out_specs=[fw(width), bw(width), st, st],
        out_shape=[jax.ShapeDtypeStruct((b, n, width), F32)] * 2
        + [jax.ShapeDtypeStruct((b, GLA_HEADS, GLA_DV, GLA_DK), F32)] * 2,
        scratch_shapes=scratch,
        compiler_params=_params("arbitrary", "arbitrary"),
        name="gla",
    )(*args)


def _layer_tail_kernel(has_gla, *refs):
    if has_gla:
        mix_ref, of_ref, ob_ref, gout_ref, ggo_ref = refs[:5]
        refs = refs[5:]
    else:
        mix_ref = refs[0]
        refs = refs[1:]
    wmix_ref, x_ref, mod_ref, g_ref, win_ref, wo_ref, o_ref = refs
    mod = mod_ref[0]
    if has_gla:
        o = of_ref[...] + ob_ref[...]
        parts = []
        for hd in range(GLA_HEADS):
            sl = slice(hd * LANES, (hd + 1) * LANES)
            gate = gout_ref[:, sl].astype(F32)
            parts.append((_rms(o[:, sl], ggo_ref[...]) * _silu(gate)).astype(BF16))
        n_mla = mix_ref.shape[1]
        out = (_dot(mix_ref[...], wmix_ref[0:n_mla, :])
               + _dot(jnp.concatenate(parts, axis=-1), wmix_ref[n_mla:, :]))
    else:
        out = _dot(mix_ref[...], wmix_ref[...])
    gate1 = mod[:, 2 * D_MODEL:3 * D_MODEL]
    x1 = x_ref[...] + gate1 * _rms(out, g_ref[1:2, :])
    shift, scale = mod[:, 3 * D_MODEL:4 * D_MODEL], mod[:, 4 * D_MODEL:5 * D_MODEL]
    hb = (_rms(x1, g_ref[2:3, :]) * (1.0 + scale) + shift).astype(BF16)

    th = FFN_HIDDEN // FFN_SPLIT
    f = None
    for s in range(FFN_SPLIT):
        gate = _dot(hb, win_ref[:, s * th:(s + 1) * th])
        up = _dot(hb, win_ref[:, FFN_HIDDEN + s * th:FFN_HIDDEN + (s + 1) * th])
        part = _dot((_silu(gate) * up).astype(BF16), wo_ref[s * th:(s + 1) * th, :])
        f = part if f is None else f + part

    gate2 = mod[:, 5 * D_MODEL:6 * D_MODEL]
    o_ref[...] = x1 + gate2 * _rms(f, g_ref[3:4, :])


def _layer_tail(mix, gla, w_mix, x, mod3, modrow, g, w_in, w_out, tm):
    t = x.shape[0]
    tok = lambda n: pl.BlockSpec((tm, n), lambda i: (i, 0))
    resident = lambda shape: pl.BlockSpec(shape, lambda i: (0,) * len(shape), pipeline_mode=pl.Buffered(1))
    in_specs = [tok(mix.shape[1])]
    args = [mix]
    if gla is not None:
        o_f, o_b, gout, g_gla = gla
        in_specs += [tok(o_f.shape[1])] * 3 + [resident((1, GLA_DV))]
        args += [o_f, o_b, gout, g_gla]
    in_specs += [resident(w_mix.shape), tok(D_MODEL),
                 pl.BlockSpec((1, 1, 6 * D_MODEL), lambda i: (modrow(i), 0, 0)), resident(g.shape),
                 resident(w_in.shape), resident(w_out.shape)]
    args += [w_mix, x, mod3, g, w_in, w_out]
    return pl.pallas_call(
        functools.partial(_layer_tail_kernel, gla is not None),
        grid=(t // tm,),
        in_specs=in_specs,
        out_specs=tok(D_MODEL),
        out_shape=jax.ShapeDtypeStruct((t, D_MODEL), F32),
        compiler_params=_params("arbitrary"),
        name="layer_tail",
    )(*args)


def _c_in_kernel(use_rope, want_f32, *refs):
    x_ref, mod_ref, g0_ref, w_ref = refs[:4]
    refs = refs[4:]
    if use_rope:
        c_ref, se_ref, so_ref = refs[:3]
        refs = refs[3:]
    q_out, k_out, v_out = refs[:3]
    mod = mod_ref[0]
    shift, scale = mod[:, 0:D_MODEL], mod[:, D_MODEL:2 * D_MODEL]
    h = _rms(x_ref[...], g0_ref[...]) * (1.0 + scale) + shift
    proj = _dot(h.astype(BF16), w_ref[...])
    width = DIFF_HEADS * LANES
    sm_scale = DIFF_HEAD_DIM ** -0.5 * LOG2E
    for hd in range(DIFF_HEADS):
        sl = slice(hd * LANES, (hd + 1) * LANES)
        qh = proj[:, hd * LANES:(hd + 1) * LANES]
        kh = proj[:, width + hd * LANES:width + (hd + 1) * LANES]
        vh = proj[:, 2 * width + hd * LANES:2 * width + (hd + 1) * LANES]
        if use_rope:
            qh = _rope(qh, c_ref[...], se_ref[...], so_ref[...])
            kh = _rope(kh, c_ref[...], se_ref[...], so_ref[...])
        q_out[0, hd] = (qh * sm_scale).astype(BF16)
        k_out[0, hd] = kh.astype(BF16)
        v_out[0, hd] = vh.astype(BF16)
        if want_f32:
            refs[3][:, sl] = kh
            refs[4][:, sl] = vh


def _c_in(x, mod3, modrow, g0, w_qkv, rope, want_f32, tm, seq_len):
    t = x.shape[0]
    width = DIFF_HEADS * LANES
    tok = lambda n: pl.BlockSpec((tm, n), lambda i: (i, 0))
    in_specs = [tok(D_MODEL), pl.BlockSpec((1, 1, 6 * D_MODEL), lambda i: (modrow(i), 0, 0)),
                _full((1, D_MODEL)), _full((D_MODEL, 3 * width))]
    args = [x, mod3, g0, w_qkv]
    per = seq_len // tm
    nb = t // seq_len
    if rope is not None:
        in_specs += [pl.BlockSpec((tm, LANES), lambda i: (i % per, 0))] * 3
        args += list(rope)
    heads = pl.BlockSpec((1, DIFF_HEADS, tm, LANES), lambda i: (i // per, 0, i % per, 0))
    head_shape = jax.ShapeDtypeStruct((nb, DIFF_HEADS, seq_len, LANES), BF16)
    n_f32 = 2 if want_f32 else 0
    return pl.pallas_call(
        functools.partial(_c_in_kernel, rope is not None, want_f32),
        grid=(t // tm,),
        in_specs=in_specs,
        out_specs=[heads] * 3 + [tok(width)] * n_f32,
        out_shape=[head_shape] * 3 + [jax.ShapeDtypeStruct((t, width), F32)] * n_f32,
        compiler_params=_params("arbitrary"),
        name="c_in",
    )(*args)


def _diff_attn_kernel(has_ctx, lam_init, *refs):
    if has_ctx:
        q_ref, kn, vn, kc, vc, lam_ref, g_ref, o_ref = refs
    else:
        q_ref, kn, vn, lam_ref, g_ref, o_ref = refs
    lp = lam_ref[...]
    lam = (jnp.exp(jnp.sum(lp[0:1] * lp[1:2], axis=-1, keepdims=True))
           - jnp.exp(jnp.sum(lp[2:3] * lp[3:4], axis=-1, keepdims=True)) + lam_init)
    for i in range(q_ref.shape[0]):
        k_new, k_ctx = kn.at[i], (kc.at[i] if has_ctx else None)
        sources = [(k_new, k_new, vn.at[i])]
        if has_ctx:
            sources.insert(0, (k_ctx, k_ctx, vc.at[i]))
        qh = q_ref[i]
        lane = lax.broadcasted_iota(jnp.int32, qh.shape, 1)
        zero = jnp.zeros_like(qh)
        q0 = jnp.where(lane < DIFF_HEAD_DIM, qh, zero)
        q1 = jnp.where(lane < DIFF_HEAD_DIM, zero, qh)
        (l0, acc0), (l1, acc1) = _two_map_flash(q0, q1, sources)
        o = acc0 * (1.0 / l0) - lam * (acc1 * (1.0 / l1))
        o_ref[:, i * LANES:(i + 1) * LANES] = (_rms(o, g_ref[...]) * (1.0 - lam_init)).astype(BF16)


def _diff_attn(q, k_new, v_new, ctx, lam_p, g_out, lam_init, tq, heads_per_step):
    b, nh, n, _ = q.shape
    hps = heads_per_step
    slab = lambda rows, f: pl.BlockSpec((None, hps, rows, LANES), f)
    whole = lambda bi, h, qi: (bi, h, 0, 0)
    in_specs = [slab(tq, lambda bi, h, qi: (bi, h, qi, 0)), slab(n, whole), slab(n, whole)]
    args = [q, k_new, v_new]
    if ctx is not None:
        in_specs += [slab(ctx[0].shape[2], whole)] * 2
        args += list(ctx)
    in_specs += [_full(lam_p.shape), _full(g_out.shape)]
    args += [lam_p, g_out]
    return pl.pallas_call(
        functools.partial(_diff_attn_kernel, ctx is not None, lam_init),
        grid=(b, nh // hps, n // tq),
        in_specs=in_specs,
        out_specs=pl.BlockSpec((None, tq, hps * LANES), lambda bi, h, qi: (bi, qi, h)),
        out_shape=jax.ShapeDtypeStruct((b, n, nh * LANES), BF16),
        compiler_params=_params("arbitrary", "arbitrary", "arbitrary"),
        name="diff_attn",
    )(*args)


def _prep_ab_weights(ai, ab_w_in, mla_g_q, mla_g_kv, mla_w_uq, mla_w_ukv, gla_w_gate_up, gla_b_gate):
    w = ab_w_in[ai]
    d = w.shape[0]
    o = np.cumsum([0, MLA_Q_LORA, MLA_KV_LORA, MLA_ROPE, GLA_HEADS * GLA_DK, GLA_HEADS * GLA_DK,
                   GLA_HEADS * GLA_DV, 2 * GLA_GATE_RANK, GLA_HEADS * GLA_DV])
    q_lat, kv_lat, k_rope, gq, gk, gv, ggate, gout = (w[:, o[i]:o[i + 1]] for i in range(8))
    pad = jnp.zeros((d, LANES - MLA_ROPE - 2 * GLA_GATE_RANK), w.dtype)
    win = jnp.concatenate([q_lat, kv_lat, gq, gk, gv, gout, k_rope, ggate, pad], axis=1).astype(BF16)

    head_pad = LANES - MLA_NOPE - MLA_ROPE
    wq = mla_w_uq[ai].reshape(MLA_Q_LORA, MLA_HEADS, MLA_NOPE + MLA_ROPE)
    wq = jnp.pad(wq, ((0, 0), (0, 0), (0, head_pad))).reshape(MLA_Q_LORA, MLA_HEADS * LANES).astype(BF16)
    wkv = mla_w_ukv[ai].reshape(MLA_KV_LORA, MLA_HEADS, MLA_NOPE + MLA_V)
    wk = jnp.pad(wkv[:, :, :MLA_NOPE], ((0, 0), (0, 0), (0, LANES - MLA_NOPE)))
    wk = wk.reshape(MLA_KV_LORA, MLA_HEADS * LANES).astype(BF16)
    wv = wkv[:, :, MLA_NOPE:].reshape(MLA_KV_LORA, MLA_HEADS * MLA_V).astype(BF16)

    e = np.zeros((LANES, MLA_HEADS * LANES), np.float32)
    for hd in range(MLA_HEADS):
        e[np.arange(MLA_ROPE), hd * LANES + MLA_NOPE + np.arange(MLA_ROPE)] = 1.0

    wg = jnp.zeros((2, LANES, GLA_HEADS * GLA_DK), F32)
    for dr in range(2):
        lo = _MISC_GATE + dr * GLA_GATE_RANK
        wg = wg.at[dr, lo:lo + GLA_GATE_RANK, :].set(gla_w_gate_up[ai, dr])
    return dict(win=win, g_q=mla_g_q[ai][None, :], g_kv=mla_g_kv[ai][None, :], wq=wq, wk=wk, wv=wv,
                e=jnp.asarray(e, dtype=BF16), wg=wg.astype(BF16),
                bg=gla_b_gate[ai].reshape(2, 1, GLA_HEADS * GLA_DK))


def _diff_lambda_init(layer):
    return 0.8 - 0.6 * math.exp(-0.3 * layer)


def kernel(x_prompt, x_sample, cache_mla_ckv, cache_mla_krope, state_gla_fwd, state_gla_bwd,
           cache_diff_k, cache_diff_v, c, c_ctx, w_mod, b_mod, g_norm, ab_w_in, mla_g_q, mla_g_kv,
           mla_w_uq, mla_w_ukv, gla_w_gate_up, gla_b_gate, gla_g_out, ab_w_out, c_w_qkv, diff_lambda,
           diff_g_out, c_w_out, w_ffn_in, w_ffn_out):
    depth = w_mod.shape[0]
    bp, lp, d = x_prompt.shape
    bs, ls, _ = x_sample.shape
    n_ctx = cache_mla_ckv.shape[2]

    n_cond = 1 + bs
    rows = -(-n_cond // 8) * 8
    conds = jnp.concatenate([c_ctx[None, :], c, jnp.zeros((rows - n_cond, d), F32)], axis=0)
    mod3 = _modulation(conds, w_mod, b_mod).reshape(depth * rows, 1, 6 * d)

    per_s = ls // min(TOKEN_TILE, ls)
    rope_q = _rope_tables(ls, MLA_ROPE, MLA_NOPE, LANES)
    rope_misc = _rope_tables(ls, MLA_ROPE, 0, LANES)
    rope_diff = _rope_tables(ls, DIFF_HEAD_DIM, 0, DIFF_HEAD_DIM)

    groups = {
        "prompt": dict(x=x_prompt.reshape(bp * lp, d), b=bp, n=lp, rope=False,
                       modrow=lambda li: (lambda i: li * rows)),
        "sample": dict(x=x_sample.reshape(bs * ls, d), b=bs, n=ls, rope=True,
                       modrow=lambda li: (lambda i: li * rows + 1 + i // per_s)),
    }
    ab_states, c_states = [], []

    ab_w = [_prep_ab_weights(ai, ab_w_in, mla_g_q, mla_g_kv, mla_w_uq, mla_w_ukv, gla_w_gate_up, gla_b_gate)
            for ai in range(ab_w_in.shape[0])]
    ab_w_out_b = ab_w_out.astype(BF16)
    c_w_qkv_b = c_w_qkv.astype(BF16)
    c_w_out_b = c_w_out.astype(BF16)
    w_ffn_in_b = w_ffn_in.astype(BF16)
    w_ffn_out_b = w_ffn_out.astype(BF16)

    for name, grp in groups.items():
        x, b, n = grp["x"], grp["b"], grp["n"]
        is_sample = grp["rope"]
        tm = min(TOKEN_TILE, n)
        for li in range(depth):
            modrow = grp["modrow"](li)
            g = g_norm[li]
            if li % 2 == 0:
                ai = li // 2
                w = ab_w[ai]
                rope = rope_q + rope_misc if is_sample else None
                q, kcat, v, ckv, misc, gq, gk, gv, gout = _ab_in(x, mod3, modrow, g[0:1], w, rope, tm, n)
                sh = lambda a: a.reshape(b, n, a.shape[-1])
                if is_sample:
                    ctx = _ctx_kv(cache_mla_ckv[:, ai], cache_mla_krope[:, ai], w)
                    s0 = (jnp.swapaxes(state_gla_fwd[:, ai], -1, -2), jnp.swapaxes(state_gla_bwd[:, ai], -1, -2))
                else:
                    ctx, s0 = None, None
                mla_o = _mla_attn(q, kcat, v, ctx, min(ATTN_Q_TILE, n),
                                  max(1, min(MLA_HEADS // 2, ATTN_STEP_ROWS // n)))
                o_f, o_b, s_f, s_b = _gla(sh(gq), sh(gk), sh(gv), sh(misc), w, s0, min(GLA_BLOCK, n))
                if not is_sample:
                    ab_states.append((ckv.reshape(b, n, MLA_KV_LORA), misc[:, :MLA_ROPE].reshape(b, n, MLA_ROPE),
                                      jnp.swapaxes(s_f, -1, -2), jnp.swapaxes(s_b, -1, -2)))
                mix = mla_o.reshape(b * n, -1)
                gla = (o_f.reshape(b * n, -1), o_b.reshape(b * n, -1), gout, gla_g_out[ai][None, :])
                w_mix = ab_w_out_b[ai]
            else:
                ci = li // 2
                rope = rope_diff if is_sample else None
                outs = _c_in(x, mod3, modrow, g[0:1], c_w_qkv_b[ci], rope, not is_sample, tm, n)
                q, k, v = outs[:3]
                if is_sample:
                    head_major = lambda a: jnp.swapaxes(a.reshape(bs, n_ctx, DIFF_HEADS, LANES), 1, 2).astype(BF16)
                    ctx = (head_major(cache_diff_k[:, ci]), head_major(cache_diff_v[:, ci]))
                else:
                    ctx = None
                    c_states.append((outs[3].reshape(b, n, DIFF_HEADS, 2, DIFF_HEAD_DIM),
                                     outs[4].reshape(b, n, DIFF_HEADS, 2 * DIFF_HEAD_DIM)))
                o = _diff_attn(q, k, v, ctx, diff_lambda[ci], diff_g_out[ci][None, :],
                               _diff_lambda_init(li), min(ATTN_Q_TILE, n),
                               max(1, min(DIFF_HEADS, ATTN_STEP_ROWS // n)))
                mix, gla, w_mix = o.reshape(b * n, -1), None, c_w_out_b[ci]
            tm_tail = tm if is_sample else min(TOKEN_TILE, b * n)
            x = _layer_tail(mix, gla, w_mix, x, mod3, modrow, g, w_ffn_in_b[li], w_ffn_out_b[li], tm_tail)
        grp["y"] = x.reshape(b, n, d)

    stack = lambda states, i: jnp.stack([s[i] for s in states], axis=1)
    return (groups["prompt"]["y"], groups["sample"]["y"],
            stack(ab_states, 0), stack(ab_states, 1), stack(ab_states, 2), stack(ab_states, 3),
            stack(c_states, 0), stack(c_states, 1))
```

```python
import functools
import math

import numpy as np
import jax
import jax.numpy as jnp
from jax import lax
from jax.experimental import pallas as pl
from jax.experimental.pallas import tpu as pltpu

F32 = jnp.float32
BF16 = jnp.bfloat16

D_MODEL = 1024
GRID_W = 64
ROPE_BASE = 10000.0
NORM_EPS = 1e-6

MLA_HEADS = 8
MLA_NOPE = 64
MLA_ROPE = 32
MLA_V = 64
MLA_Q_LORA = 384
MLA_KV_LORA = 256
GLA_HEADS = 4
GLA_DK = 128
GLA_DV = 128
GLA_GATE_RANK = 16
GLA_GATE_NORM = 16.0
GLA_CHUNK = 64
DIFF_HEADS = 8
DIFF_HEAD_DIM = 64
FFN_HIDDEN = 2816

LANES = 128
VMEM_LIMIT = 56 * 1024 * 1024

_OFF_QLAT = 0
_OFF_MISC = _OFF_QLAT + MLA_Q_LORA
_OFF_KVLAT = _OFF_MISC + LANES
_OFF_GQ = _OFF_KVLAT + MLA_KV_LORA
_OFF_GK = _OFF_GQ + GLA_HEADS * GLA_DK
_OFF_GV = _OFF_GK + GLA_HEADS * GLA_DK
_OFF_GOUT = _OFF_GV + GLA_HEADS * GLA_DV
_AB_COLS = _OFF_GOUT + GLA_HEADS * GLA_DV
_MISC_GATE = MLA_ROPE

LOG2E = math.log2(math.e)

TOKEN_TILE = 512
ATTN_Q_TILE = 1024
ATTN_K_TILE = 256
ATTN_STEP_ROWS = 2048
GLA_BLOCK = 256
GLA_BATCH = 2
FFN_SPLIT = 11


def _rms(x, g):
    var = jnp.mean(x * x, axis=-1, keepdims=True)
    return x * lax.rsqrt(var + NORM_EPS) * g


def _silu(x):
    return x * (1.0 / (1.0 + jnp.exp(-x)))


def _log_sigmoid(x):
    return jnp.minimum(x, 0.0) - jnp.log(1.0 + jnp.exp(-jnp.abs(x)))


def _rope(x, c, se, so):
    n = x.shape[-1]
    return x * c + pltpu.roll(x, n - 1, 1) * se + pltpu.roll(x, 1, 1) * so


def _dot(a, b):
    return jnp.dot(a, b, preferred_element_type=F32)


def _dot_nt(a, b):
    return lax.dot_general(a, b, (((1,), (1,)), ((), ())), preferred_element_type=F32)


def _dot_tn(a, b):
    return lax.dot_general(a, b, (((0,), (0,)), ((), ())), preferred_element_type=F32)


def _params(*sem):
    return pltpu.CompilerParams(dimension_semantics=sem, vmem_limit_bytes=VMEM_LIMIT)


def _full(shape):
    zeros = (0,) * len(shape)
    return pl.BlockSpec(shape, lambda *_: zeros)


def _rope_tables(n_tokens, rot_dim, lane_lo, period):
    rows = n_tokens // GRID_W
    row = np.repeat(np.arange(rows, dtype=np.float64), GRID_W)
    col = np.tile(np.arange(GRID_W, dtype=np.float64), rows)
    n_freq = rot_dim // 4
    inv = ROPE_BASE ** (-np.arange(n_freq, dtype=np.float64) / n_freq)
    ang = np.concatenate([row[:, None] * inv, col[:, None] * inv], axis=-1)
    cos, sin = np.cos(ang), np.sin(ang)
    c = np.ones((n_tokens, period))
    se = np.zeros((n_tokens, period))
    so = np.zeros((n_tokens, period))
    c[:, lane_lo:lane_lo + rot_dim] = np.repeat(cos, 2, axis=-1)
    se[:, lane_lo:lane_lo + rot_dim:2] = -sin
    so[:, lane_lo + 1:lane_lo + rot_dim:2] = sin
    reps = LANES // period
    return tuple(jnp.asarray(np.tile(t, (1, reps)), dtype=F32) for t in (c, se, so))


def _mod_kernel(c_ref, w_ref, b_ref, o_ref):
    a = _silu(c_ref[...]).astype(BF16)
    o_ref[0] = _dot(a, w_ref[0].astype(BF16)) + b_ref[0]


def _modulation(conds, w_mod, b_mod):
    depth, d, n = w_mod.shape
    rows = conds.shape[0]
    tn = 1536
    return pl.pallas_call(
        _mod_kernel,
        grid=(depth, n // tn),
        in_specs=[_full((rows, d)),
                  pl.BlockSpec((1, d, tn), lambda l, j: (l, 0, j)),
                  pl.BlockSpec((1, 1, tn), lambda l, j: (l, 0, j))],
        out_specs=pl.BlockSpec((1, rows, tn), lambda l, j: (l, 0, j)),
        out_shape=jax.ShapeDtypeStruct((depth, rows, n), F32),
        compiler_params=_params("arbitrary", "arbitrary"),
        name="modulation",
    )(conds, w_mod, b_mod.reshape(depth, 1, n))


def _ab_in_kernel(use_rope, *refs):
    (x_ref, mod_ref, g0_ref, win_ref, gq_ref, gkv_ref, wq_ref, wk_ref, wv_ref) = refs[:9]
    refs = refs[9:]
    if use_rope:
        cq, seq, soq, cm, sem, som = refs[:6]
        refs = refs[6:]
    (q_out, kcat_out, v_out, ckv_out, misc_out, gq_out, gk_out, gv_out, gout_out) = refs

    mod = mod_ref[0]
    shift, scale = mod[:, 0:D_MODEL], mod[:, D_MODEL:2 * D_MODEL]
    hb = (_rms(x_ref[...], g0_ref[...]) * (1.0 + scale) + shift).astype(BF16)
    project = lambda lo, hi: _dot(hb, win_ref[:, lo:hi])

    q_misc = project(_OFF_QLAT, _OFF_KVLAT)
    misc = q_misc[:, _OFF_MISC:]
    if use_rope:
        misc = _rope(misc, cm[...], sem[...], som[...])
    misc_out[...] = misc

    ckv = _rms(project(_OFF_KVLAT, _OFF_GQ), gkv_ref[...])
    ckv_out[...] = ckv
    ckv_b = ckv.astype(BF16)
    key_rope = pltpu.roll(misc, MLA_NOPE, 1)
    lane = lax.broadcasted_iota(jnp.int32, misc.shape, 1)
    rope_lanes = (lane >= MLA_NOPE) & (lane < MLA_NOPE + MLA_ROPE)
    pair = 2 * LANES
    for hp in range(MLA_HEADS // 2):
        k_nope = _dot(ckv_b, wk_ref[:, hp * pair:(hp + 1) * pair])
        for e in range(2):
            kh = jnp.where(rope_lanes, key_rope, k_nope[:, e * LANES:(e + 1) * LANES])
            kcat_out[0, 2 * hp + e] = kh.astype(BF16)
    v = _dot(ckv_b, wv_ref[...]).astype(BF16)
    for hp in range(MLA_HEADS // 2):
        v_out[0, hp] = v[:, hp * LANES:(hp + 1) * LANES]

    qn = _rms(q_misc[:, :_OFF_MISC], gq_ref[...]).astype(BF16)
    sm_scale = (MLA_NOPE + MLA_ROPE) ** -0.5 * LOG2E
    for hp in range(MLA_HEADS // 2):
        q2 = _dot(qn, wq_ref[:, hp * pair:(hp + 1) * pair])
        for e in range(2):
            qh = q2[:, e * LANES:(e + 1) * LANES]
            if use_rope:
                qh = _rope(qh, cq[...], seq[...], soq[...])
            q_out[0, 2 * hp + e] = (qh * sm_scale).astype(BF16)

    gq_out[...] = (project(_OFF_GQ, _OFF_GK) * (GLA_DK ** -0.5)).astype(BF16)
    gk_out[...] = project(_OFF_GK, _OFF_GV).astype(BF16)
    gv_out[...] = project(_OFF_GV, _OFF_GOUT).astype(BF16)
    gout_out[...] = project(_OFF_GOUT, _AB_COLS).astype(BF16)


def _ab_in(x, mod3, modrow, g0, w, rope, tm, seq_len):
    t = x.shape[0]
    nt = t // tm
    tok = lambda n: pl.BlockSpec((tm, n), lambda i: (i, 0))
    in_specs = [tok(D_MODEL),
                pl.BlockSpec((1, 1, 6 * D_MODEL), lambda i: (modrow(i), 0, 0)),
                _full((1, D_MODEL)), _full((D_MODEL, _AB_COLS)),
                _full((1, MLA_Q_LORA)), _full((1, MLA_KV_LORA)),
                _full((MLA_Q_LORA, MLA_HEADS * LANES)), _full((MLA_KV_LORA, MLA_HEADS * LANES)),
                _full((MLA_KV_LORA, MLA_HEADS * MLA_V))]
    args = [x, mod3, g0, w["win"], w["g_q"], w["g_kv"], w["wq"], w["wk"], w["wv"]]
    per = seq_len // tm
    if rope is not None:
        in_specs += [pl.BlockSpec((tm, LANES), lambda i: (i % per, 0))] * 6
        args += list(rope)
    nb = t // seq_len
    heads = lambda nh: pl.BlockSpec((1, nh, tm, LANES), lambda i: (i // per, 0, i % per, 0))
    head_shape = lambda nh: jax.ShapeDtypeStruct((nb, nh, seq_len, LANES), BF16)
    widths = [(MLA_KV_LORA, F32), (LANES, F32)] + [(GLA_HEADS * GLA_DK, BF16)] * 4
    return pl.pallas_call(
        functools.partial(_ab_in_kernel, rope is not None),
        grid=(nt,),
        in_specs=in_specs,
        out_specs=[heads(MLA_HEADS), heads(MLA_HEADS), heads(MLA_HEADS // 2)] + [tok(n) for n, _ in widths],
        out_shape=[head_shape(MLA_HEADS), head_shape(MLA_HEADS), head_shape(MLA_HEADS // 2)]
        + [jax.ShapeDtypeStruct((t, n), dt) for n, dt in widths],
        compiler_params=_params("arbitrary"),
        name="ab_in",
    )(*args)


def _ctx_kv_kernel(ckv_ref, kr_ref, wk_ref, wv_ref, e_ref, kcat_out, v_out):
    c = ckv_ref[...].astype(BF16)
    kr = kr_ref[...].astype(BF16)
    kcat = (_dot(c, wk_ref[...]) + _dot(kr, e_ref[...])).astype(BF16)
    v = _dot(c, wv_ref[...]).astype(BF16)
    for hd in range(MLA_HEADS):
        kcat_out[0, hd] = kcat[:, hd * LANES:(hd + 1) * LANES]
    for hp in range(MLA_HEADS // 2):
        v_out[0, hp] = v[:, hp * LANES:(hp + 1) * LANES]


def _ctx_kv(ckv, krope, w):
    b, n_ctx, _ = ckv.shape
    tok = lambda n: pl.BlockSpec((None, n_ctx, n), lambda i: (i, 0, 0))
    heads = lambda nh: pl.BlockSpec((1, nh, n_ctx, LANES), lambda i: (i, 0, 0, 0))
    return pl.pallas_call(
        _ctx_kv_kernel,
        grid=(b,),
        in_specs=[tok(MLA_KV_LORA), tok(MLA_ROPE),
                  _full((MLA_KV_LORA, MLA_HEADS * LANES)), _full((MLA_KV_LORA, MLA_HEADS * MLA_V)),
                  _full((MLA_ROPE, MLA_HEADS * LANES))],
        out_specs=[heads(MLA_HEADS), heads(MLA_HEADS // 2)],
        out_shape=[jax.ShapeDtypeStruct((b, MLA_HEADS, n_ctx, LANES), BF16),
                   jax.ShapeDtypeStruct((b, MLA_HEADS // 2, n_ctx, LANES), BF16)],
        compiler_params=_params("arbitrary"),
        name="mla_ctx_kv",
    )(ckv, krope, w["wk"], w["wv"], w["e"])


def _two_map_flash(q0, q1, sources):
    state = [None, None]
    for k0_ref, k1_ref, v_ref in sources:
        rows = v_ref.shape[0]
        tk = min(ATTN_K_TILE, rows)
        ones = jnp.ones((tk, LANES), BF16)
        for c in range(rows // tk):
            r = slice(c * tk, (c + 1) * tk)
            v = jnp.concatenate([v_ref[r, :], ones], axis=1)
            for j, (q, k_ref) in enumerate(((q0, k0_ref), (q1, k1_ref))):
                s = _dot_nt(q, k_ref[r, :])
                m_blk = jnp.max(s, axis=-1, keepdims=True)
                if state[j] is None:
                    p = jnp.exp2(s - m_blk)
                    state[j] = (m_blk, _dot(p.astype(BF16), v))
                else:
                    m, acc = state[j]
                    m_new = jnp.maximum(m, m_blk)
                    p = jnp.exp2(s - m_new)
                    state[j] = (m_new, jnp.exp2(m - m_new) * acc + _dot(p.astype(BF16), v))
    return tuple((acc[:, LANES:], acc[:, :LANES]) for _, acc in state)


def _mla_attn_kernel(has_ctx, *refs):
    if has_ctx:
        q_ref, kn, vn, kc, vc, o_ref = refs
    else:
        q_ref, kn, vn, o_ref = refs
    for i in range(vn.shape[0]):
        e, o = 2 * i, 2 * i + 1
        sources = [(kn.at[e], kn.at[o], vn.at[i])]
        if has_ctx:
            sources.insert(0, (kc.at[e], kc.at[o], vc.at[i]))
        (l0, acc0), (l1, acc1) = _two_map_flash(q_ref[e], q_ref[o], sources)
        lane = lax.broadcasted_iota(jnp.int32, acc0.shape, 1)
        o_ref[:, i * LANES:(i + 1) * LANES] = jnp.where(
            lane < MLA_V, acc0 * (1.0 / l0), acc1 * (1.0 / l1)).astype(BF16)


def _mla_attn(q, k_new, v_new, ctx, tq, pairs_per_step):
    b, nh, n, _ = q.shape
    pp = pairs_per_step
    slab = lambda heads, rows, f: pl.BlockSpec((None, heads, rows, LANES), f)
    whole = lambda bi, p, qi: (bi, p, 0, 0)
    in_specs = [slab(2 * pp, tq, lambda bi, p, qi: (bi, p, qi, 0)), slab(2 * pp, n, whole), slab(pp, n, whole)]
    args = [q, k_new, v_new]
    if ctx is not None:
        nc = ctx[0].shape[2]
        in_specs += [slab(2 * pp, nc, whole), slab(pp, nc, whole)]
        args += list(ctx)
    return pl.pallas_call(
        functools.partial(_mla_attn_kernel, ctx is not None),
        grid=(b, nh // (2 * pp), n // tq),
        in_specs=in_specs,
        out_specs=pl.BlockSpec((None, tq, pp * LANES), lambda bi, p, qi: (bi, qi, p)),
        out_shape=jax.ShapeDtypeStruct((b, n, (nh // 2) * LANES), BF16),
        compiler_params=_params("arbitrary", "arbitrary", "arbitrary"),
        name="mla_attn",
    )(*args)


def _gla_kernel(has_s0, lookahead, *refs):
    fwd_in, bwd_in = refs[0:4], refs[4:8]
    refs = refs[8:]
    if lookahead:
        next_misc = refs[:2]
        refs = refs[2:]
    wg_ref, bg_ref, tf_ref, tb_ref = refs[:4]
    refs = refs[4:]
    if has_s0:
        s0f_ref, s0b_ref = refs[:2]
        refs = refs[2:]
    of_ref, ob_ref, sf_ref, sb_ref = refs[:4]
    tb_rows = of_ref.shape[1]
    nch = tb_rows // GLA_CHUNK
    n_bb = of_ref.shape[0]
    t_refs = (tf_ref, tb_ref)

    def log_decay(m_ref, d, bb):
        pre = _dot(m_ref[bb].astype(BF16), wg_ref[d]) + bg_ref[d]
        g = _log_sigmoid(pre) * (LOG2E / GLA_GATE_NORM)
        g_hi = g.astype(BF16)
        g_lo = (g - g_hi.astype(F32)).astype(BF16)
        t = t_refs[d][...]
        return _dot(t, g_hi) + _dot(t, g_lo)

    @pl.when(pl.program_id(1) == 0)
    def _():
        if has_s0:
            sf_ref[...] = s0f_ref[...]
            sb_ref[...] = s0b_ref[...]
        else:
            sf_ref[...] = jnp.zeros(sf_ref.shape, F32)
            sb_ref[...] = jnp.zeros(sb_ref.shape, F32)
        if lookahead:
            cum_scr = refs[4]
            for d, m_ref in enumerate((fwd_in[3], bwd_in[3])):
                for bb in range(n_bb):
                    cum_scr[d, bb] = log_decay(m_ref, d, bb)

    scans = []
    for d, ((q_ref, k_ref, v_ref, m_ref), o_ref, s_ref) in enumerate(
            ((fwd_in, of_ref, sf_ref), (bwd_in, ob_ref, sb_ref))):
        row = lax.broadcasted_iota(jnp.int32, (tb_rows, tb_rows), 0)
        col = lax.broadcasted_iota(jnp.int32, (tb_rows, tb_rows), 1)
        chunk_lo = (row // GLA_CHUNK) * GLA_CHUNK
        if d == 0:
            tmask = (col <= row) & (col >= chunk_lo)
        else:
            tmask = (col >= row) & (col < chunk_lo + GLA_CHUNK)
        for bb in range(n_bb):
            cum = refs[4][d, bb] if lookahead else log_decay(m_ref, d, bb)
            for hd in range(GLA_HEADS):
                sl = slice(hd * LANES, (hd + 1) * LANES)
                scans.append(dict(d=d, bb=bb, hd=hd, sl=sl, cs=cum[:, sl], tmask=tmask,
                                  q_ref=q_ref, k_ref=k_ref, v_ref=v_ref, o_ref=o_ref, s_ref=s_ref))

    for sc in scans:
        cs, bb, sl = sc["cs"], sc["bb"], sc["sl"]
        k = sc["k_ref"][bb, :, sl].astype(F32)
        v = sc["v_ref"][bb, :, sl]
        sc["q_in"] = (sc["q_ref"][bb, :, sl].astype(F32) * jnp.exp2(cs)).astype(BF16)
        k_neg = k * jnp.exp2(-cs)
        a = jnp.where(sc["tmask"], _dot_nt(sc["q_in"], k_neg.astype(BF16)), 0.0)
        sc["o_intra"] = _dot(a.astype(BF16), v)
        sc["decay"], sc["u_t"] = [], []
        for n in range(nch):
            r = slice(n * GLA_CHUNK, (n + 1) * GLA_CHUNK)
            edge = (n + 1) * GLA_CHUNK - 1 if sc["d"] == 0 else n * GLA_CHUNK
            decay = jnp.exp2(cs[edge:edge + 1, :])
            k_dec = (k_neg[r] * decay).astype(BF16)
            sc["decay"].append(decay)
            sc["u_t"].append(_dot_tn(v[r], k_dec))
        sc["s_t"] = sc["s_ref"][bb, sc["hd"]]

    for step in range(nch):
        for sc in scans:
            n = step if sc["d"] == 0 else nch - 1 - step
            r = slice(n * GLA_CHUNK, (n + 1) * GLA_CHUNK)
            sc["o_ref"][sc["bb"], r, sc["sl"]] = (
                sc["o_intra"][r] + _dot_nt(sc["q_in"][r], sc["s_t"].astype(BF16)))
            sc["s_t"] = sc["s_t"] * sc["decay"][n] + sc["u_t"][n]

    for sc in scans:
        sc["s_ref"][sc["bb"], sc["hd"]] = sc["s_t"]

    if lookahead:
        for d, m_ref in enumerate(next_misc):
            for bb in range(n_bb):
                refs[4][d, bb] = log_decay(m_ref, d, bb)


def _gla_masks(tb_rows):
    idx = np.arange(tb_rows)
    same = (idx[:, None] // GLA_CHUNK) == (idx[None, :] // GLA_CHUNK)
    fwd = same & (idx[None, :] <= idx[:, None])
    bwd = same & (idx[None, :] >= idx[:, None])
    return jnp.asarray(fwd, dtype=BF16), jnp.asarray(bwd, dtype=BF16)


def _gla(gq, gk, gv, misc, w, s0, tb_rows):
    b, n, width = gq.shape
    nb = n // tb_rows
    gb = GLA_BATCH
    fw = lambda cols: pl.BlockSpec((gb, tb_rows, cols), lambda bi, i: (bi, i, 0))
    bw = lambda cols: pl.BlockSpec((gb, tb_rows, cols), lambda bi, i: (bi, nb - 1 - i, 0))
    st = pl.BlockSpec((gb, GLA_HEADS, GLA_DV, GLA_DK), lambda bi, i: (bi, 0, 0, 0))
    tf, tb = _gla_masks(tb_rows)
    lookahead = nb > 1
    in_specs = [fw(width)] * 3 + [fw(LANES)] + [bw(width)] * 3 + [bw(LANES)]
    args = [gq, gk, gv, misc, gq, gk, gv, misc]
    if lookahead:
        in_specs += [pl.BlockSpec((gb, tb_rows, LANES), lambda bi, i: (bi, jnp.minimum(i + 1, nb - 1), 0)),
                     pl.BlockSpec((gb, tb_rows, LANES), lambda bi, i: (bi, jnp.maximum(nb - 2 - i, 0), 0))]
        args += [misc, misc]
    in_specs += [_full((2, LANES, width)), _full((2, 1, width)),
                 _full((tb_rows, tb_rows)), _full((tb_rows, tb_rows))]
    args += [w["wg"], w["bg"], tf, tb]
    if s0 is not None:
        in_specs += [st, st]
        args += list(s0)
    scratch = [pltpu.VMEM((2, gb, tb_rows, width), F32)] if lookahead else []
    return pl.pallas_call(
        functools.partial(_gla_kernel, s0 is not None, lookahead),
        grid=(b // gb, nb),
        in_specs=in_specs,
        out_specs=[fw(width), bw(width), st, st],
        out_shape=[jax.ShapeDtypeStruct((b, n, width), F32)] * 2
        + [jax.ShapeDtypeStruct((b, GLA_HEADS, GLA_DV, GLA_DK), F32)] * 2,
        scratch_shapes=scratch,
        compiler_params=_params("arbitrary", "arbitrary"),
        name="gla",
    )(*args)


def _layer_tail_kernel(has_gla, *refs):
    if has_gla:
        mix_ref, of_ref, ob_ref, gout_ref, ggo_ref = refs[:5]
        refs = refs[5:]
    else:
        mix_ref = refs[0]
        refs = refs[1:]
    wmix_ref, x_ref, mod_ref, g_ref, win_ref, wo_ref, o_ref = refs
    mod = mod_ref[0]
    if has_gla:
        o = of_ref[...] + ob_ref[...]
        parts = []
        for hd in range(GLA_HEADS):
            sl = slice(hd * LANES, (hd + 1) * LANES)
            gate = gout_ref[:, sl].astype(F32)
            parts.append((_rms(o[:, sl], ggo_ref[...]) * _silu(gate)).astype(BF16))
        n_mla = mix_ref.shape[1]
        out = (_dot(mix_ref[...], wmix_ref[0:n_mla, :])
               + _dot(jnp.concatenate(parts, axis=-1), wmix_ref[n_mla:, :]))
    else:
        out = _dot(mix_ref[...], wmix_ref[...])
    gate1 = mod[:, 2 * D_MODEL:3 * D_MODEL]
    x1 = x_ref[...] + gate1 * _rms(out, g_ref[1:2, :])
    shift, scale = mod[:, 3 * D_MODEL:4 * D_MODEL], mod[:, 4 * D_MODEL:5 * D_MODEL]
    hb = (_rms(x1, g_ref[2:3, :]) * (1.0 + scale) + shift).astype(BF16)

    th = FFN_HIDDEN // FFN_SPLIT
    f = None
    for s in range(FFN_SPLIT):
        gate = _dot(hb, win_ref[:, s * th:(s + 1) * th])
        up = _dot(hb, win_ref[:, FFN_HIDDEN + s * th:FFN_HIDDEN + (s + 1) * th])
        part = _dot((_silu(gate) * up).astype(BF16), wo_ref[s * th:(s + 1) * th, :])
        f = part if f is None else f + part

    gate2 = mod[:, 5 * D_MODEL:6 * D_MODEL]
    o_ref[...] = x1 + gate2 * _rms(f, g_ref[3:4, :])


def _layer_tail(mix, gla, w_mix, x, mod3, modrow, g, w_in, w_out, tm):
    t = x.shape[0]
    tok = lambda n: pl.BlockSpec((tm, n), lambda i: (i, 0))
    resident = lambda shape: pl.BlockSpec(shape, lambda i: (0,) * len(shape), pipeline_mode=pl.Buffered(1))
    in_specs = [tok(mix.shape[1])]
    args = [mix]
    if gla is not None:
        o_f, o_b, gout, g_gla = gla
        in_specs += [tok(o_f.shape[1])] * 3 + [resident((1, GLA_DV))]
        args += [o_f, o_b, gout, g_gla]
    in_specs += [resident(w_mix.shape), tok(D_MODEL),
                 pl.BlockSpec((1, 1, 6 * D_MODEL), lambda i: (modrow(i), 0, 0)), resident(g.shape),
                 resident(w_in.shape), resident(w_out.shape)]
    args += [w_mix, x, mod3, g, w_in, w_out]
    return pl.pallas_call(
        functools.partial(_layer_tail_kernel, gla is not None),
        grid=(t // tm,),
        in_specs=in_specs,
        out_specs=tok(D_MODEL),
        out_shape=jax.ShapeDtypeStruct((t, D_MODEL), F32),
        compiler_params=_params("arbitrary"),
        name="layer_tail",
    )(*args)


def _c_in_kernel(use_rope, want_f32, *refs):
    x_ref, mod_ref, g0_ref, w_ref = refs[:4]
    refs = refs[4:]
    if use_rope:
        c_ref, se_ref, so_ref = refs[:3]
        refs = refs[3:]
    q_out, k_out, v_out = refs[:3]
    mod = mod_ref[0]
    shift, scale = mod[:, 0:D_MODEL], mod[:, D_MODEL:2 * D_MODEL]
    h = _rms(x_ref[...], g0_ref[...]) * (1.0 + scale) + shift
    proj = _dot(h.astype(BF16), w_ref[...])
    width = DIFF_HEADS * LANES
    sm_scale = DIFF_HEAD_DIM ** -0.5 * LOG2E
    for hd in range(DIFF_HEADS):
        sl = slice(hd * LANES, (hd + 1) * LANES)
        qh = proj[:, hd * LANES:(hd + 1) * LANES]
        kh = proj[:, width + hd * LANES:width + (hd + 1) * LANES]
        vh = proj[:, 2 * width + hd * LANES:2 * width + (hd + 1) * LANES]
        if use_rope:
            qh = _rope(qh, c_ref[...], se_ref[...], so_ref[...])
            kh = _rope(kh, c_ref[...], se_ref[...], so_ref[...])
        q_out[0, hd] = (qh * sm_scale).astype(BF16)
        k_out[0, hd] = kh.astype(BF16)
        v_out[0, hd] = vh.astype(BF16)
        if want_f32:
            refs[3][:, sl] = kh
            refs[4][:, sl] = vh


def _c_in(x, mod3, modrow, g0, w_qkv, rope, want_f32, tm, seq_len):
    t = x.shape[0]
    width = DIFF_HEADS * LANES
    tok = lambda n: pl.BlockSpec((tm, n), lambda i: (i, 0))
    in_specs = [tok(D_MODEL), pl.BlockSpec((1, 1, 6 * D_MODEL), lambda i: (modrow(i), 0, 0)),
                _full((1, D_MODEL)), _full((D_MODEL, 3 * width))]
    args = [x, mod3, g0, w_qkv]
    per = seq_len // tm
    nb = t // seq_len
    if rope is not None:
        in_specs += [pl.BlockSpec((tm, LANES), lambda i: (i % per, 0))] * 3
        args += list(rope)
    heads = pl.BlockSpec((1, DIFF_HEADS, tm, LANES), lambda i: (i // per, 0, i % per, 0))
    head_shape = jax.ShapeDtypeStruct((nb, DIFF_HEADS, seq_len, LANES), BF16)
    n_f32 = 2 if want_f32 else 0
    return pl.pallas_call(
        functools.partial(_c_in_kernel, rope is not None, want_f32),
        grid=(t // tm,),
        in_specs=in_specs,
        out_specs=[heads] * 3 + [tok(width)] * n_f32,
        out_shape=[head_shape] * 3 + [jax.ShapeDtypeStruct((t, width), F32)] * n_f32,
        compiler_params=_params("arbitrary"),
        name="c_in",
    )(*args)


def _diff_attn_kernel(has_ctx, lam_init, *refs):
    if has_ctx:
        q_ref, kn, vn, kc, vc, lam_ref, g_ref, o_ref = refs
    else:
        q_ref, kn, vn, lam_ref, g_ref, o_ref = refs
    lp = lam_ref[...]
    lam = (jnp.exp(jnp.sum(lp[0:1] * lp[1:2], axis=-1, keepdims=True))
           - jnp.exp(jnp.sum(lp[2:3] * lp[3:4], axis=-1, keepdims=True)) + lam_init)
    for i in range(q_ref.shape[0]):
        k_new, k_ctx = kn.at[i], (kc.at[i] if has_ctx else None)
        sources = [(k_new, k_new, vn.at[i])]
        if has_ctx:
            sources.insert(0, (k_ctx, k_ctx, vc.at[i]))
        qh = q_ref[i]
        lane = lax.broadcasted_iota(jnp.int32, qh.shape, 1)
        zero = jnp.zeros_like(qh)
        q0 = jnp.where(lane < DIFF_HEAD_DIM, qh, zero)
        q1 = jnp.where(lane < DIFF_HEAD_DIM, zero, qh)
        (l0, acc0), (l1, acc1) = _two_map_flash(q0, q1, sources)
        o = acc0 * (1.0 / l0) - lam * (acc1 * (1.0 / l1))
        o_ref[:, i * LANES:(i + 1) * LANES] = (_rms(o, g_ref[...]) * (1.0 - lam_init)).astype(BF16)


def _diff_attn(q, k_new, v_new, ctx, lam_p, g_out, lam_init, tq, heads_per_step):
    b, nh, n, _ = q.shape
    hps = heads_per_step
    slab = lambda rows, f: pl.BlockSpec((None, hps, rows, LANES), f)
    whole = lambda bi, h, qi: (bi, h, 0, 0)
    in_specs = [slab(tq, lambda bi, h, qi: (bi, h, qi, 0)), slab(n, whole), slab(n, whole)]
    args = [q, k_new, v_new]
    if ctx is not None:
        in_specs += [slab(ctx[0].shape[2], whole)] * 2
        args += list(ctx)
    in_specs += [_full(lam_p.shape), _full(g_out.shape)]
    args += [lam_p, g_out]
    return pl.pallas_call(
        functools.partial(_diff_attn_kernel, ctx is not None, lam_init),
        grid=(b, nh // hps, n // tq),
        in_specs=in_specs,
        out_specs=pl.BlockSpec((None, tq, hps * LANES), lambda bi, h, qi: (bi, qi, h)),
        out_shape=jax.ShapeDtypeStruct((b, n, nh * LANES), BF16),
        compiler_params=_params("arbitrary", "arbitrary", "arbitrary"),
        name="diff_attn",
    )(*args)


def _prep_ab_weights(ai, ab_w_in, mla_g_q, mla_g_kv, mla_w_uq, mla_w_ukv, gla_w_gate_up, gla_b_gate):
    w = ab_w_in[ai]
    d = w.shape[0]
    o = np.cumsum([0, MLA_Q_LORA, MLA_KV_LORA, MLA_ROPE, GLA_HEADS * GLA_DK, GLA_HEADS * GLA_DK,
                   GLA_HEADS * GLA_DV, 2 * GLA_GATE_RANK, GLA_HEADS * GLA_DV])
    q_lat, kv_lat, k_rope, gq, gk, gv, ggate, gout = (w[:, o[i]:o[i + 1]] for i in range(8))
    pad = jnp.zeros((d, LANES - MLA_ROPE - 2 * GLA_GATE_RANK), w.dtype)
    win = jnp.concatenate([q_lat, k_rope, ggate, pad, kv_lat, gq, gk, gv, gout], axis=1).astype(BF16)

    head_pad = LANES - MLA_NOPE - MLA_ROPE
    wq = mla_w_uq[ai].reshape(MLA_Q_LORA, MLA_HEADS, MLA_NOPE + MLA_ROPE)
    wq = jnp.pad(wq, ((0, 0), (0, 0), (0, head_pad))).reshape(MLA_Q_LORA, MLA_HEADS * LANES).astype(BF16)
    wkv = mla_w_ukv[ai].reshape(MLA_KV_LORA, MLA_HEADS, MLA_NOPE + MLA_V)
    wk = jnp.pad(wkv[:, :, :MLA_NOPE], ((0, 0), (0, 0), (0, LANES - MLA_NOPE)))
    wk = wk.reshape(MLA_KV_LORA, MLA_HEADS * LANES).astype(BF16)
    wv = wkv[:, :, MLA_NOPE:].reshape(MLA_KV_LORA, MLA_HEADS * MLA_V).astype(BF16)

    e = np.zeros((MLA_ROPE, MLA_HEADS * LANES), np.float32)
    for hd in range(MLA_HEADS):
        e[np.arange(MLA_ROPE), hd * LANES + MLA_NOPE + np.arange(MLA_ROPE)] = 1.0

    wg = jnp.zeros((2, LANES, GLA_HEADS * GLA_DK), F32)
    for dr in range(2):
        lo = _MISC_GATE + dr * GLA_GATE_RANK
        wg = wg.at[dr, lo:lo + GLA_GATE_RANK, :].set(gla_w_gate_up[ai, dr])
    return dict(win=win, g_q=mla_g_q[ai][None, :], g_kv=mla_g_kv[ai][None, :], wq=wq, wk=wk, wv=wv,
                e=jnp.asarray(e, dtype=BF16), wg=wg.astype(BF16),
                bg=gla_b_gate[ai].reshape(2, 1, GLA_HEADS * GLA_DK))


def _diff_lambda_init(layer):
    return 0.8 - 0.6 * math.exp(-0.3 * layer)


def kernel(x_prompt, x_sample, cache_mla_ckv, cache_mla_krope, state_gla_fwd, state_gla_bwd,
           cache_diff_k, cache_diff_v, c, c_ctx, w_mod, b_mod, g_norm, ab_w_in, mla_g_q, mla_g_kv,
           mla_w_uq, mla_w_ukv, gla_w_gate_up, gla_b_gate, gla_g_out, ab_w_out, c_w_qkv, diff_lambda,
           diff_g_out, c_w_out, w_ffn_in, w_ffn_out):
    depth = w_mod.shape[0]
    bp, lp, d = x_prompt.shape
    bs, ls, _ = x_sample.shape
    n_ctx = cache_mla_ckv.shape[2]

    n_cond = 1 + bs
    rows = -(-n_cond // 8) * 8
    conds = jnp.concatenate([c_ctx[None, :], c, jnp.zeros((rows - n_cond, d), F32)], axis=0)
    mod3 = _modulation(conds, w_mod, b_mod).reshape(depth * rows, 1, 6 * d)

    per_s = ls // min(TOKEN_TILE, ls)
    rope_q = _rope_tables(ls, MLA_ROPE, MLA_NOPE, LANES)
    rope_misc = _rope_tables(ls, MLA_ROPE, 0, LANES)
    rope_diff = _rope_tables(ls, DIFF_HEAD_DIM, 0, DIFF_HEAD_DIM)

    groups = {
        "prompt": dict(x=x_prompt.reshape(bp * lp, d), b=bp, n=lp, rope=False,
                       modrow=lambda li: (lambda i: li * rows)),
        "sample": dict(x=x_sample.reshape(bs * ls, d), b=bs, n=ls, rope=True,
                       modrow=lambda li: (lambda i: li * rows + 1 + i // per_s)),
    }
    ab_states, c_states = [], []

    ab_w = [_prep_ab_weights(ai, ab_w_in, mla_g_q, mla_g_kv, mla_w_uq, mla_w_ukv, gla_w_gate_up, gla_b_gate)
            for ai in range(ab_w_in.shape[0])]
    ab_w_out_b = ab_w_out.astype(BF16)
    c_w_qkv_b = c_w_qkv.astype(BF16)
    c_w_out_b = c_w_out.astype(BF16)
    w_ffn_in_b = w_ffn_in.astype(BF16)
    w_ffn_out_b = w_ffn_out.astype(BF16)

    for name, grp in groups.items():
        x, b, n = grp["x"], grp["b"], grp["n"]
        is_sample = grp["rope"]
        tm = min(TOKEN_TILE, n)
        for li in range(depth):
            modrow = grp["modrow"](li)
            g = g_norm[li]
            if li % 2 == 0:
                ai = li // 2
                w = ab_w[ai]
                rope = rope_q + rope_misc if is_sample else None
                q, kcat, v, ckv, misc, gq, gk, gv, gout = _ab_in(x, mod3, modrow, g[0:1], w, rope, tm, n)
                sh = lambda a: a.reshape(b, n, a.shape[-1])
                if is_sample:
                    ctx = _ctx_kv(cache_mla_ckv[:, ai], cache_mla_krope[:, ai], w)
                    s0 = (jnp.swapaxes(state_gla_fwd[:, ai], -1, -2), jnp.swapaxes(state_gla_bwd[:, ai], -1, -2))
                else:
                    ctx, s0 = None, None
                mla_o = _mla_attn(q, kcat, v, ctx, min(ATTN_Q_TILE, n),
                                  max(1, min(MLA_HEADS // 2, ATTN_STEP_ROWS // n)))
                o_f, o_b, s_f, s_b = _gla(sh(gq), sh(gk), sh(gv), sh(misc), w, s0, min(GLA_BLOCK, n))
                if not is_sample:
                    ab_states.append((ckv.reshape(b, n, MLA_KV_LORA), misc[:, :MLA_ROPE].reshape(b, n, MLA_ROPE),
                                      jnp.swapaxes(s_f, -1, -2), jnp.swapaxes(s_b, -1, -2)))
                mix = mla_o.reshape(b * n, -1)
                gla = (o_f.reshape(b * n, -1), o_b.reshape(b * n, -1), gout, gla_g_out[ai][None, :])
                w_mix = ab_w_out_b[ai]
            else:
                ci = li // 2
                rope = rope_diff if is_sample else None
                outs = _c_in(x, mod3, modrow, g[0:1], c_w_qkv_b[ci], rope, not is_sample, tm, n)
                q, k, v = outs[:3]
                if is_sample:
                    head_major = lambda a: jnp.swapaxes(a.reshape(bs, n_ctx, DIFF_HEADS, LANES), 1, 2).astype(BF16)
                    ctx = (head_major(cache_diff_k[:, ci]), head_major(cache_diff_v[:, ci]))
                else:
                    ctx = None
                    c_states.append((outs[3].reshape(b, n, DIFF_HEADS, 2, DIFF_HEAD_DIM),
                                     outs[4].reshape(b, n, DIFF_HEADS, 2 * DIFF_HEAD_DIM)))
                o = _diff_attn(q, k, v, ctx, diff_lambda[ci], diff_g_out[ci][None, :],
                               _diff_lambda_init(li), min(ATTN_Q_TILE, n),
                               max(1, min(DIFF_HEADS, ATTN_STEP_ROWS // n)))
                mix, gla, w_mix = o.reshape(b * n, -1), None, c_w_out_b[ci]
            tm_tail = tm if is_sample else min(TOKEN_TILE, b * n)
            x = _layer_tail(mix, gla, w_mix, x, mod3, modrow, g, w_ffn_in_b[li], w_ffn_out_b[li], tm_tail)
        grp["y"] = x.reshape(b, n, d)

    stack = lambda states, i: jnp.stack([s[i] for s in states], axis=1)
    return (groups["prompt"]["y"], groups["sample"]["y"],
            stack(ab_states, 0), stack(ab_states, 1), stack(ab_states, 2), stack(ab_states, 3),
            stack(c_states, 0), stack(c_states, 1))
```

```python
import functools
import math

import numpy as np
import jax
import jax.numpy as jnp
from jax import lax
from jax.experimental import pallas as pl
from jax.experimental.pallas import tpu as pltpu

F32 = jnp.float32
BF16 = jnp.bfloat16

D_MODEL = 1024
GRID_W = 64
ROPE_BASE = 10000.0
NORM_EPS = 1e-6

MLA_HEADS = 8
MLA_NOPE = 64
MLA_ROPE = 32
MLA_V = 64
MLA_Q_LORA = 384
MLA_KV_LORA = 256
GLA_HEADS = 4
GLA_DK = 128
GLA_DV = 128
GLA_GATE_RANK = 16
GLA_GATE_NORM = 16.0
GLA_CHUNK = 64
DIFF_HEADS = 8
DIFF_HEAD_DIM = 64
FFN_HIDDEN = 2816

LANES = 128
VMEM_LIMIT = 56 * 1024 * 1024

_OFF_QLAT = 0
_OFF_MISC = _OFF_QLAT + MLA_Q_LORA
_OFF_KVLAT = _OFF_MISC + LANES
_OFF_GQ = _OFF_KVLAT + MLA_KV_LORA
_OFF_GK = _OFF_GQ + GLA_HEADS * GLA_DK
_OFF_GV = _OFF_GK + GLA_HEADS * GLA_DK
_OFF_GOUT = _OFF_GV + GLA_HEADS * GLA_DV
_AB_COLS = _OFF_GOUT + GLA_HEADS * GLA_DV
_MISC_GATE = MLA_ROPE

LOG2E = math.log2(math.e)

TOKEN_TILE = 512
ATTN_Q_TILE = 1024
ATTN_K_TILE = 256
ATTN_STEP_ROWS = 8192
GLA_BLOCK = 256
GLA_BATCH = 2
FFN_SPLIT = 11


def _rms(x, g):
    var = jnp.mean(x * x, axis=-1, keepdims=True)
    return x * lax.rsqrt(var + NORM_EPS) * g


def _silu(x):
    return x * (1.0 / (1.0 + jnp.exp(-x)))


def _log_sigmoid(x):
    return jnp.minimum(x, 0.0) - jnp.log(1.0 + jnp.exp(-jnp.abs(x)))


def _rope(x, c, se, so):
    n = x.shape[-1]
    return x * c + pltpu.roll(x, n - 1, 1) * se + pltpu.roll(x, 1, 1) * so


def _dot(a, b):
    return jnp.dot(a, b, preferred_element_type=F32)


def _dot_nt(a, b):
    return lax.dot_general(a, b, (((1,), (1,)), ((), ())), preferred_element_type=F32)


def _dot_tn(a, b):
    return lax.dot_general(a, b, (((0,), (0,)), ((), ())), preferred_element_type=F32)


def _params(*sem):
    return pltpu.CompilerParams(dimension_semantics=sem, vmem_limit_bytes=VMEM_LIMIT)


def _full(shape):
    zeros = (0,) * len(shape)
    return pl.BlockSpec(shape, lambda *_: zeros)


def _rope_tables(n_tokens, rot_dim, lane_lo, period):
    rows = n_tokens // GRID_W
    row = np.repeat(np.arange(rows, dtype=np.float64), GRID_W)
    col = np.tile(np.arange(GRID_W, dtype=np.float64), rows)
    n_freq = rot_dim // 4
    inv = ROPE_BASE ** (-np.arange(n_freq, dtype=np.float64) / n_freq)
    ang = np.concatenate([row[:, None] * inv, col[:, None] * inv], axis=-1)
    cos, sin = np.cos(ang), np.sin(ang)
    c = np.ones((n_tokens, period))
    se = np.zeros((n_tokens, period))
    so = np.zeros((n_tokens, period))
    c[:, lane_lo:lane_lo + rot_dim] = np.repeat(cos, 2, axis=-1)
    se[:, lane_lo:lane_lo + rot_dim:2] = -sin
    so[:, lane_lo + 1:lane_lo + rot_dim:2] = sin
    reps = LANES // period
    return tuple(jnp.asarray(np.tile(t, (1, reps)), dtype=F32) for t in (c, se, so))


def _mod_kernel(c_ref, w_ref, b_ref, o_ref):
    a = _silu(c_ref[...]).astype(BF16)
    o_ref[0] = _dot(a, w_ref[0].astype(BF16)) + b_ref[0]


def _modulation(conds, w_mod, b_mod):
    depth, d, n = w_mod.shape
    rows = conds.shape[0]
    tn = 1536
    return pl.pallas_call(
        _mod_kernel,
        grid=(depth, n // tn),
        in_specs=[_full((rows, d)),
                  pl.BlockSpec((1, d, tn), lambda l, j: (l, 0, j)),
                  pl.BlockSpec((1, 1, tn), lambda l, j: (l, 0, j))],
        out_specs=pl.BlockSpec((1, rows, tn), lambda l, j: (l, 0, j)),
        out_shape=jax.ShapeDtypeStruct((depth, rows, n), F32),
        compiler_params=_params("arbitrary", "arbitrary"),
        name="modulation",
    )(conds, w_mod, b_mod.reshape(depth, 1, n))


def _ab_in_kernel(use_rope, *refs):
    (x_ref, mod_ref, g0_ref, win_ref, gq_ref, gkv_ref, wq_ref, wk_ref, wv_ref) = refs[:9]
    refs = refs[9:]
    if use_rope:
        cq, seq, soq, cm, sem, som = refs[:6]
        refs = refs[6:]
    (q_out, kcat_out, v_out, ckv_out, misc_out, gq_out, gk_out, gv_out, gout_out) = refs

    mod = mod_ref[0]
    shift, scale = mod[:, 0:D_MODEL], mod[:, D_MODEL:2 * D_MODEL]
    hb = (_rms(x_ref[...], g0_ref[...]) * (1.0 + scale) + shift).astype(BF16)
    project = lambda lo, hi: _dot(hb, win_ref[:, lo:hi])

    q_misc = project(_OFF_QLAT, _OFF_KVLAT)
    misc = q_misc[:, _OFF_MISC:]
    if use_rope:
        misc = _rope(misc, cm[...], sem[...], som[...])
    misc_out[...] = misc

    ckv = _rms(project(_OFF_KVLAT, _OFF_GQ), gkv_ref[...])
    ckv_out[...] = ckv
    ckv_b = ckv.astype(BF16)
    key_rope = pltpu.roll(misc, MLA_NOPE, 1)
    lane = lax.broadcasted_iota(jnp.int32, misc.shape, 1)
    rope_lanes = (lane >= MLA_NOPE) & (lane < MLA_NOPE + MLA_ROPE)
    pair = 2 * LANES
    for hp in range(MLA_HEADS // 2):
        k_nope = _dot(ckv_b, wk_ref[:, hp * pair:(hp + 1) * pair])
        for e in range(2):
            kh = jnp.where(rope_lanes, key_rope, k_nope[:, e * LANES:(e + 1) * LANES])
            kcat_out[0, 2 * hp + e] = kh.astype(BF16)
    v = _dot(ckv_b, wv_ref[...]).astype(BF16)
    for hp in range(MLA_HEADS // 2):
        v_out[0, hp] = v[:, hp * LANES:(hp + 1) * LANES]

    qn = _rms(q_misc[:, :_OFF_MISC], gq_ref[...]).astype(BF16)
    sm_scale = (MLA_NOPE + MLA_ROPE) ** -0.5 * LOG2E
    for hp in range(MLA_HEADS // 2):
        q2 = _dot(qn, wq_ref[:, hp * pair:(hp + 1) * pair])
        for e in range(2):
            qh = q2[:, e * LANES:(e + 1) * LANES]
            if use_rope:
                qh = _rope(qh, cq[...], seq[...], soq[...])
            q_out[0, 2 * hp + e] = (qh * sm_scale).astype(BF16)

    gq_out[...] = (project(_OFF_GQ, _OFF_GK) * (GLA_DK ** -0.5)).astype(BF16)
    gk_out[...] = project(_OFF_GK, _OFF_GV).astype(BF16)
    gv_out[...] = project(_OFF_GV, _OFF_GOUT).astype(BF16)
    gout_out[...] = project(_OFF_GOUT, _AB_COLS).astype(BF16)


def _ab_in(x, mod3, modrow, g0, w, rope, tm, seq_len):
    t = x.shape[0]
    nt = t // tm
    tok = lambda n: pl.BlockSpec((tm, n), lambda i: (i, 0))
    in_specs = [tok(D_MODEL),
                pl.BlockSpec((1, 1, 6 * D_MODEL), lambda i: (modrow(i), 0, 0)),
                _full((1, D_MODEL)), _full((D_MODEL, _AB_COLS)),
                _full((1, MLA_Q_LORA)), _full((1, MLA_KV_LORA)),
                _full((MLA_Q_LORA, MLA_HEADS * LANES)), _full((MLA_KV_LORA, MLA_HEADS * LANES)),
                _full((MLA_KV_LORA, MLA_HEADS * MLA_V))]
    args = [x, mod3, g0, w["win"], w["g_q"], w["g_kv"], w["wq"], w["wk"], w["wv"]]
    per = seq_len // tm
    if rope is not None:
        in_specs += [pl.BlockSpec((tm, LANES), lambda i: (i % per, 0))] * 6
        args += list(rope)
    nb = t // seq_len
    heads = lambda nh: pl.BlockSpec((1, nh, tm, LANES), lambda i: (i // per, 0, i % per, 0))
    head_shape = lambda nh: jax.ShapeDtypeStruct((nb, nh, seq_len, LANES), BF16)
    widths = [(MLA_KV_LORA, F32), (LANES, F32)] + [(GLA_HEADS * GLA_DK, BF16)] * 4
    return pl.pallas_call(
        functools.partial(_ab_in_kernel, rope is not None),
        grid=(nt,),
        in_specs=in_specs,
        out_specs=[heads(MLA_HEADS), heads(MLA_HEADS), heads(MLA_HEADS // 2)] + [tok(n) for n, _ in widths],
        out_shape=[head_shape(MLA_HEADS), head_shape(MLA_HEADS), head_shape(MLA_HEADS // 2)]
        + [jax.ShapeDtypeStruct((t, n), dt) for n, dt in widths],
        compiler_params=_params("arbitrary"),
        name="ab_in",
    )(*args)


def _ctx_kv_kernel(ckv_ref, kr_ref, wk_ref, wv_ref, e_ref, kcat_out, v_out):
    c = ckv_ref[...].astype(BF16)
    kr = kr_ref[...].astype(BF16)
    kcat = (_dot(c, wk_ref[...]) + _dot(kr, e_ref[...])).astype(BF16)
    v = _dot(c, wv_ref[...]).astype(BF16)
    for hd in range(MLA_HEADS):
        kcat_out[0, hd] = kcat[:, hd * LANES:(hd + 1) * LANES]
    for hp in range(MLA_HEADS // 2):
        v_out[0, hp] = v[:, hp * LANES:(hp + 1) * LANES]


def _ctx_kv(ckv, krope, w):
    b, n_ctx, _ = ckv.shape
    tok = lambda n: pl.BlockSpec((None, n_ctx, n), lambda i: (i, 0, 0))
    heads = lambda nh: pl.BlockSpec((1, nh, n_ctx, LANES), lambda i: (i, 0, 0, 0))
    return pl.pallas_call(
        _ctx_kv_kernel,
        grid=(b,),
        in_specs=[tok(MLA_KV_LORA), tok(MLA_ROPE),
                  _full((MLA_KV_LORA, MLA_HEADS * LANES)), _full((MLA_KV_LORA, MLA_HEADS * MLA_V)),
                  _full((MLA_ROPE, MLA_HEADS * LANES))],
        out_specs=[heads(MLA_HEADS), heads(MLA_HEADS // 2)],
        out_shape=[jax.ShapeDtypeStruct((b, MLA_HEADS, n_ctx, LANES), BF16),
                   jax.ShapeDtypeStruct((b, MLA_HEADS // 2, n_ctx, LANES), BF16)],
        compiler_params=_params("arbitrary"),
        name="mla_ctx_kv",
    )(ckv, krope, w["wk"], w["wv"], w["e"])


def _two_map_flash(q0, q1, sources):
    state = [None, None]
    for k0_ref, k1_ref, v_ref in sources:
        rows = v_ref.shape[0]
        tk = min(ATTN_K_TILE, rows)
        ones = jnp.ones((tk, LANES), BF16)
        for c in range(rows // tk):
            r = slice(c * tk, (c + 1) * tk)
            v = jnp.concatenate([v_ref[r, :], ones], axis=1)
            for j, (q, k_ref) in enumerate(((q0, k0_ref), (q1, k1_ref))):
                s = _dot_nt(q, k_ref[r, :])
                m_blk = jnp.max(s, axis=-1, keepdims=True)
                if state[j] is None:
                    p = jnp.exp2(s - m_blk)
                    state[j] = (m_blk, _dot(p.astype(BF16), v))
                else:
                    m, acc = state[j]
                    m_new = jnp.maximum(m, m_blk)
                    p = jnp.exp2(s - m_new)
                    state[j] = (m_new, jnp.exp2(m - m_new) * acc + _dot(p.astype(BF16), v))
    return tuple((acc[:, LANES:], acc[:, :LANES]) for _, acc in state)


def _mla_attn_kernel(has_ctx, *refs):
    if has_ctx:
        q_ref, kn, vn, kc, vc, o_ref = refs
    else:
        q_ref, kn, vn, o_ref = refs
    for i in range(vn.shape[0]):
        e, o = 2 * i, 2 * i + 1
        sources = [(kn.at[e], kn.at[o], vn.at[i])]
        if has_ctx:
            sources.insert(0, (kc.at[e], kc.at[o], vc.at[i]))
        (l0, acc0), (l1, acc1) = _two_map_flash(q_ref[e], q_ref[o], sources)
        lane = lax.broadcasted_iota(jnp.int32, acc0.shape, 1)
        o_ref[:, i * LANES:(i + 1) * LANES] = jnp.where(
            lane < MLA_V, acc0 * (1.0 / l0), acc1 * (1.0 / l1)).astype(BF16)


def _mla_attn(q, k_new, v_new, ctx, tq, pairs_per_step):
    b, nh, n, _ = q.shape
    pp = pairs_per_step
    slab = lambda heads, rows, f: pl.BlockSpec((None, heads, rows, LANES), f)
    whole = lambda bi, p, qi: (bi, p, 0, 0)
    in_specs = [slab(2 * pp, tq, lambda bi, p, qi: (bi, p, qi, 0)), slab(2 * pp, n, whole), slab(pp, n, whole)]
    args = [q, k_new, v_new]
    if ctx is not None:
        nc = ctx[0].shape[2]
        in_specs += [slab(2 * pp, nc, whole), slab(pp, nc, whole)]
        args += list(ctx)
    return pl.pallas_call(
        functools.partial(_mla_attn_kernel, ctx is not None),
        grid=(b, nh // (2 * pp), n // tq),
        in_specs=in_specs,
        out_specs=pl.BlockSpec((None, tq, pp * LANES), lambda bi, p, qi: (bi, qi, p)),
        out_shape=jax.ShapeDtypeStruct((b, n, (nh // 2) * LANES), BF16),
        compiler_params=_params("arbitrary", "arbitrary", "arbitrary"),
        name="mla_attn",
    )(*args)


def _gla_kernel(has_s0, lookahead, *refs):
    fwd_in, bwd_in = refs[0:4], refs[4:8]
    refs = refs[8:]
    if lookahead:
        next_misc = refs[:2]
        refs = refs[2:]
    wg_ref, bg_ref, tf_ref, tb_ref = refs[:4]
    refs = refs[4:]
    if has_s0:
        s0f_ref, s0b_ref = refs[:2]
        refs = refs[2:]
    of_ref, ob_ref, sf_ref, sb_ref = refs[:4]
    tb_rows = of_ref.shape[1]
    nch = tb_rows // GLA_CHUNK
    n_bb = of_ref.shape[0]
    t_refs = (tf_ref, tb_ref)

    def log_decay(m_ref, d, bb):
        pre = _dot(m_ref[bb].astype(BF16), wg_ref[d]) + bg_ref[d]
        g = _log_sigmoid(pre) * (LOG2E / GLA_GATE_NORM)
        g_hi = g.astype(BF16)
        g_lo = (g - g_hi.astype(F32)).astype(BF16)
        t = t_refs[d][...]
        return _dot(t, g_hi) + _dot(t, g_lo)

    @pl.when(pl.program_id(1) == 0)
    def _():
        if has_s0:
            sf_ref[...] = s0f_ref[...]
            sb_ref[...] = s0b_ref[...]
        else:
            sf_ref[...] = jnp.zeros(sf_ref.shape, F32)
            sb_ref[...] = jnp.zeros(sb_ref.shape, F32)
        if lookahead:
            cum_scr = refs[4]
            for d, m_ref in enumerate((fwd_in[3], bwd_in[3])):
                for bb in range(n_bb):
                    cum_scr[d, bb] = log_decay(m_ref, d, bb)

    scans = []
    for d, ((q_ref, k_ref, v_ref, m_ref), o_ref, s_ref) in enumerate(
            ((fwd_in, of_ref, sf_ref), (bwd_in, ob_ref, sb_ref))):
        row = lax.broadcasted_iota(jnp.int32, (tb_rows, tb_rows), 0)
        col = lax.broadcasted_iota(jnp.int32, (tb_rows, tb_rows), 1)
        chunk_lo = (row // GLA_CHUNK) * GLA_CHUNK
        if d == 0:
            tmask = (col <= row) & (col >= chunk_lo)
        else:
            tmask = (col >= row) & (col < chunk_lo + GLA_CHUNK)
        for bb in range(n_bb):
            cum = refs[4][d, bb] if lookahead else log_decay(m_ref, d, bb)
            for hd in range(GLA_HEADS):
                sl = slice(hd * LANES, (hd + 1) * LANES)
                scans.append(dict(d=d, bb=bb, hd=hd, sl=sl, cs=cum[:, sl], tmask=tmask,
                                  q_ref=q_ref, k_ref=k_ref, v_ref=v_ref, o_ref=o_ref, s_ref=s_ref))

    for sc in scans:
        cs, bb, sl = sc["cs"], sc["bb"], sc["sl"]
        k = sc["k_ref"][bb, :, sl].astype(F32)
        v = sc["v_ref"][bb, :, sl]
        sc["q_in"] = (sc["q_ref"][bb, :, sl].astype(F32) * jnp.exp2(cs)).astype(BF16)
        k_neg = k * jnp.exp2(-cs)
        a = jnp.where(sc["tmask"], _dot_nt(sc["q_in"], k_neg.astype(BF16)), 0.0)
        sc["o_intra"] = _dot(a.astype(BF16), v)
        sc["decay"], sc["u_t"] = [], []
        for n in range(nch):
            r = slice(n * GLA_CHUNK, (n + 1) * GLA_CHUNK)
            edge = (n + 1) * GLA_CHUNK - 1 if sc["d"] == 0 else n * GLA_CHUNK
            decay = jnp.exp2(cs[edge:edge + 1, :])
            k_dec = (k_neg[r] * decay).astype(BF16)
            sc["decay"].append(decay)
            sc["u_t"].append(_dot_tn(v[r], k_dec))
        sc["s_t"] = sc["s_ref"][bb, sc["hd"]]

    for step in range(nch):
        for sc in scans:
            n = step if sc["d"] == 0 else nch - 1 - step
            r = slice(n * GLA_CHUNK, (n + 1) * GLA_CHUNK)
            sc["o_ref"][sc["bb"], r, sc["sl"]] = (
                sc["o_intra"][r] + _dot_nt(sc["q_in"][r], sc["s_t"].astype(BF16)))
            sc["s_t"] = sc["s_t"] * sc["decay"][n] + sc["u_t"][n]

    for sc in scans:
        sc["s_ref"][sc["bb"], sc["hd"]] = sc["s_t"]

    if lookahead:
        for d, m_ref in enumerate(next_misc):
            for bb in range(n_bb):
                refs[4][d, bb] = log_decay(m_ref, d, bb)


def _gla_masks(tb_rows):
    idx = np.arange(tb_rows)
    same = (idx[:, None] // GLA_CHUNK) == (idx[None, :] // GLA_CHUNK)
    fwd = same & (idx[None, :] <= idx[:, None])
    bwd = same & (idx[None, :] >= idx[:, None])
    return jnp.asarray(fwd, dtype=BF16), jnp.asarray(bwd, dtype=BF16)


def _gla(gq, gk, gv, misc, w, s0, tb_rows):
    b, n, width = gq.shape
    nb = n // tb_rows
    gb = GLA_BATCH
    fw = lambda cols: pl.BlockSpec((gb, tb_rows, cols), lambda bi, i: (bi, i, 0))
    bw = lambda cols: pl.BlockSpec((gb, tb_rows, cols), lambda bi, i: (bi, nb - 1 - i, 0))
    st = pl.BlockSpec((gb, GLA_HEADS, GLA_DV, GLA_DK), lambda bi, i: (bi, 0, 0, 0))
    tf, tb = _gla_masks(tb_rows)
    lookahead = nb > 1
    in_specs = [fw(width)] * 3 + [fw(LANES)] + [bw(width)] * 3 + [bw(LANES)]
    args = [gq, gk, gv, misc, gq, gk, gv, misc]
    if lookahead:
        in_specs += [pl.BlockSpec((gb, tb_rows, LANES), lambda bi, i: (bi, jnp.minimum(i + 1, nb - 1), 0)),
                     pl.BlockSpec((gb, tb_rows, LANES), lambda bi, i: (bi, jnp.maximum(nb - 2 - i, 0), 0))]
        args += [misc, misc]
    in_specs += [_full((2, LANES, width)), _full((2, 1, width)),
                 _full((tb_rows, tb_rows)), _full((tb_rows, tb_rows))]
    args += [w["wg"], w["bg"], tf, tb]
    if s0 is not None:
        in_specs += [st, st]
        args += list(s0)
    scratch = [pltpu.VMEM((2, gb, tb_rows, width), F32)] if lookahead else []
    return pl.pallas_call(
        functools.partial(_gla_kernel, s0 is not None, lookahead),
        grid=(b // gb, nb),
        in_specs=in_specs,
        out_specs=[fw(width), bw(width), st, st],
        out_shape=[jax.ShapeDtypeStruct((b, n, width), F32)] * 2
        + [jax.ShapeDtypeStruct((b, GLA_HEADS, GLA_DV, GLA_DK), F32)] * 2,
        scratch_shapes=scratch,
        compiler_params=_params("arbitrary", "arbitrary"),
        name="gla",
    )(*args)


def _layer_tail_kernel(has_gla, *refs):
    if has_gla:
        mix_ref, of_ref, ob_ref, gout_ref, ggo_ref = refs[:5]
        refs = refs[5:]
    else:
        mix_ref = refs[0]
        refs = refs[1:]
    wmix_ref, x_ref, mod_ref, g_ref, win_ref, wo_ref, o_ref = refs
    mod = mod_ref[0]
    if has_gla:
        o = of_ref[...] + ob_ref[...]
        parts = []
        for hd in range(GLA_HEADS):
            sl = slice(hd * LANES, (hd + 1) * LANES)
            gate = gout_ref[:, sl].astype(F32)
            parts.append((_rms(o[:, sl], ggo_ref[...]) * _silu(gate)).astype(BF16))
        n_mla = mix_ref.shape[1]
        out = (_dot(mix_ref[...], wmix_ref[0:n_mla, :])
               + _dot(jnp.concatenate(parts, axis=-1), wmix_ref[n_mla:, :]))
    else:
        out = _dot(mix_ref[...], wmix_ref[...])
    gate1 = mod[:, 2 * D_MODEL:3 * D_MODEL]
    x1 = x_ref[...] + gate1 * _rms(out, g_ref[1:2, :])
    shift, scale = mod[:, 3 * D_MODEL:4 * D_MODEL], mod[:, 4 * D_MODEL:5 * D_MODEL]
    hb = (_rms(x1, g_ref[2:3, :]) * (1.0 + scale) + shift).astype(BF16)

    th = FFN_HIDDEN // FFN_SPLIT
    f = None
    for s in range(FFN_SPLIT):
        gate = _dot(hb, win_ref[:, s * th:(s + 1) * th])
        up = _dot(hb, win_ref[:, FFN_HIDDEN + s * th:FFN_HIDDEN + (s + 1) * th])
        part = _dot((_silu(gate) * up).astype(BF16), wo_ref[s * th:(s + 1) * th, :])
        f = part if f is None else f + part

    gate2 = mod[:, 5 * D_MODEL:6 * D_MODEL]
    o_ref[...] = x1 + gate2 * _rms(f, g_ref[3:4, :])


def _layer_tail(mix, gla, w_mix, x, mod3, modrow, g, w_in, w_out, tm):
    t = x.shape[0]
    tok = lambda n: pl.BlockSpec((tm, n), lambda i: (i, 0))
    resident = lambda shape: pl.BlockSpec(shape, lambda i: (0,) * len(shape), pipeline_mode=pl.Buffered(1))
    in_specs = [tok(mix.shape[1])]
    args = [mix]
    if gla is not None:
        o_f, o_b, gout, g_gla = gla
        in_specs += [tok(o_f.shape[1])] * 3 + [resident((1, GLA_DV))]
        args += [o_f, o_b, gout, g_gla]
    in_specs += [resident(w_mix.shape), tok(D_MODEL),
                 pl.BlockSpec((1, 1, 6 * D_MODEL), lambda i: (modrow(i), 0, 0)), resident(g.shape),
                 resident(w_in.shape), resident(w_out.shape)]
    args += [w_mix, x, mod3, g, w_in, w_out]
    return pl.pallas_call(
        functools.partial(_layer_tail_kernel, gla is not None),
        grid=(t // tm,),
        in_specs=in_specs,
        out_specs=tok(D_MODEL),
        out_shape=jax.ShapeDtypeStruct((t, D_MODEL), F32),
        compiler_params=_params("arbitrary"),
        name="layer_tail",
    )(*args)


def _c_in_kernel(use_rope, want_f32, *refs):
    x_ref, mod_ref, g0_ref, w_ref = refs[:4]
    refs = refs[4:]
    if use_rope:
        c_ref, se_ref, so_ref = refs[:3]
        refs = refs[3:]
    q_out, k_out, v_out = refs[:3]
    mod = mod_ref[0]
    shift, scale = mod[:, 0:D_MODEL], mod[:, D_MODEL:2 * D_MODEL]
    h = _rms(x_ref[...], g0_ref[...]) * (1.0 + scale) + shift
    proj = _dot(h.astype(BF16), w_ref[...])
    width = DIFF_HEADS * LANES
    sm_scale = DIFF_HEAD_DIM ** -0.5 * LOG2E
    for hd in range(DIFF_HEADS):
        sl = slice(hd * LANES, (hd + 1) * LANES)
        qh = proj[:, hd * LANES:(hd + 1) * LANES]
        kh = proj[:, width + hd * LANES:width + (hd + 1) * LANES]
        vh = proj[:, 2 * width + hd * LANES:2 * width + (hd + 1) * LANES]
        if use_rope:
            qh = _rope(qh, c_ref[...], se_ref[...], so_ref[...])
            kh = _rope(kh, c_ref[...], se_ref[...], so_ref[...])
        q_out[0, hd] = (qh * sm_scale).astype(BF16)
        k_out[0, hd] = kh.astype(BF16)
        v_out[0, hd] = vh.astype(BF16)
        if want_f32:
            refs[3][:, sl] = kh
            refs[4][:, sl] = vh


def _c_in(x, mod3, modrow, g0, w_qkv, rope, want_f32, tm, seq_len):
    t = x.shape[0]
    width = DIFF_HEADS * LANES
    tok = lambda n: pl.BlockSpec((tm, n), lambda i: (i, 0))
    in_specs = [tok(D_MODEL), pl.BlockSpec((1, 1, 6 * D_MODEL), lambda i: (modrow(i), 0, 0)),
                _full((1, D_MODEL)), _full((D_MODEL, 3 * width))]
    args = [x, mod3, g0, w_qkv]
    per = seq_len // tm
    nb = t // seq_len
    if rope is not None:
        in_specs += [pl.BlockSpec((tm, LANES), lambda i: (i % per, 0))] * 3
        args += list(rope)
    heads = pl.BlockSpec((1, DIFF_HEADS, tm, LANES), lambda i: (i // per, 0, i % per, 0))
    head_shape = jax.ShapeDtypeStruct((nb, DIFF_HEADS, seq_len, LANES), BF16)
    n_f32 = 2 if want_f32 else 0
    return pl.pallas_call(
        functools.partial(_c_in_kernel, rope is not None, want_f32),
        grid=(t // tm,),
        in_specs=in_specs,
        out_specs=[heads] * 3 + [tok(width)] * n_f32,
        out_shape=[head_shape] * 3 + [jax.ShapeDtypeStruct((t, width), F32)] * n_f32,
        compiler_params=_params("arbitrary"),
        name="c_in",
    )(*args)


def _diff_attn_kernel(has_ctx, lam_init, *refs):
    if has_ctx:
        q_ref, kn, vn, kc, vc, lam_ref, g_ref, o_ref = refs
    else:
        q_ref, kn, vn, lam_ref, g_ref, o_ref = refs
    lp = lam_ref[...]
    lam = (jnp.exp(jnp.sum(lp[0:1] * lp[1:2], axis=-1, keepdims=True))
           - jnp.exp(jnp.sum(lp[2:3] * lp[3:4], axis=-1, keepdims=True)) + lam_init)
    for i in range(q_ref.shape[0]):
        k_new, k_ctx = kn.at[i], (kc.at[i] if has_ctx else None)
        sources = [(k_new, k_new, vn.at[i])]
        if has_ctx:
            sources.insert(0, (k_ctx, k_ctx, vc.at[i]))
        qh = q_ref[i]
        lane = lax.broadcasted_iota(jnp.int32, qh.shape, 1)
        zero = jnp.zeros_like(qh)
        q0 = jnp.where(lane < DIFF_HEAD_DIM, qh, zero)
        q1 = jnp.where(lane < DIFF_HEAD_DIM, zero, qh)
        (l0, acc0), (l1, acc1) = _two_map_flash(q0, q1, sources)
        o = acc0 * (1.0 / l0) - lam * (acc1 * (1.0 / l1))
        o_ref[:, i * LANES:(i + 1) * LANES] = (_rms(o, g_ref[...]) * (1.0 - lam_init)).astype(BF16)


def _diff_attn(q, k_new, v_new, ctx, lam_p, g_out, lam_init, tq, heads_per_step):
    b, nh, n, _ = q.shape
    hps = heads_per_step
    slab = lambda rows, f: pl.BlockSpec((None, hps, rows, LANES), f)
    whole = lambda bi, h, qi: (bi, h, 0, 0)
    in_specs = [slab(tq, lambda bi, h, qi: (bi, h, qi, 0)), slab(n, whole), slab(n, whole)]
    args = [q, k_new, v_new]
    if ctx is not None:
        in_specs += [slab(ctx[0].shape[2], whole)] * 2
        args += list(ctx)
    in_specs += [_full(lam_p.shape), _full(g_out.shape)]
    args += [lam_p, g_out]
    return pl.pallas_call(
        functools.partial(_diff_attn_kernel, ctx is not None, lam_init),
        grid=(b, nh // hps, n // tq),
        in_specs=in_specs,
        out_specs=pl.BlockSpec((None, tq, hps * LANES), lambda bi, h, qi: (bi, qi, h)),
        out_shape=jax.ShapeDtypeStruct((b, n, nh * LANES), BF16),
        compiler_params=_params("arbitrary", "arbitrary", "arbitrary"),
        name="diff_attn",
    )(*args)


def _prep_ab_weights(ai, ab_w_in, mla_g_q, mla_g_kv, mla_w_uq, mla_w_ukv, gla_w_gate_up, gla_b_gate):
    w = ab_w_in[ai]
    d = w.shape[0]
    o = np.cumsum([0, MLA_Q_LORA, MLA_KV_LORA, MLA_ROPE, GLA_HEADS * GLA_DK, GLA_HEADS * GLA_DK,
                   GLA_HEADS * GLA_DV, 2 * GLA_GATE_RANK, GLA_HEADS * GLA_DV])
    q_lat, kv_lat, k_rope, gq, gk, gv, ggate, gout = (w[:, o[i]:o[i + 1]] for i in range(8))
    pad = jnp.zeros((d, LANES - MLA_ROPE - 2 * GLA_GATE_RANK), w.dtype)
    win = jnp.concatenate([q_lat, k_rope, ggate, pad, kv_lat, gq, gk, gv, gout], axis=1).astype(BF16)

    head_pad = LANES - MLA_NOPE - MLA_ROPE
    wq = mla_w_uq[ai].reshape(MLA_Q_LORA, MLA_HEADS, MLA_NOPE + MLA_ROPE)
    wq = jnp.pad(wq, ((0, 0), (0, 0), (0, head_pad))).reshape(MLA_Q_LORA, MLA_HEADS * LANES).astype(BF16)
    wkv = mla_w_ukv[ai].reshape(MLA_KV_LORA, MLA_HEADS, MLA_NOPE + MLA_V)
    wk = jnp.pad(wkv[:, :, :MLA_NOPE], ((0, 0), (0, 0), (0, LANES - MLA_NOPE)))
    wk = wk.reshape(MLA_KV_LORA, MLA_HEADS * LANES).astype(BF16)
    wv = wkv[:, :, MLA_NOPE:].reshape(MLA_KV_LORA, MLA_HEADS * MLA_V).astype(BF16)

    e = np.zeros((MLA_ROPE, MLA_HEADS * LANES), np.float32)
    for hd in range(MLA_HEADS):
        e[np.arange(MLA_ROPE), hd * LANES + MLA_NOPE + np.arange(MLA_ROPE)] = 1.0

    wg = jnp.zeros((2, LANES, GLA_HEADS * GLA_DK), F32)
    for dr in range(2):
        lo = _MISC_GATE + dr * GLA_GATE_RANK
        wg = wg.at[dr, lo:lo + GLA_GATE_RANK, :].set(gla_w_gate_up[ai, dr])
    return dict(win=win, g_q=mla_g_q[ai][None, :], g_kv=mla_g_kv[ai][None, :], wq=wq, wk=wk, wv=wv,
                e=jnp.asarray(e, dtype=BF16), wg=wg.astype(BF16),
                bg=gla_b_gate[ai].reshape(2, 1, GLA_HEADS * GLA_DK))


def _diff_lambda_init(layer):
    return 0.8 - 0.6 * math.exp(-0.3 * layer)


def kernel(x_prompt, x_sample, cache_mla_ckv, cache_mla_krope, state_gla_fwd, state_gla_bwd,
           cache_diff_k, cache_diff_v, c, c_ctx, w_mod, b_mod, g_norm, ab_w_in, mla_g_q, mla_g_kv,
           mla_w_uq, mla_w_ukv, gla_w_gate_up, gla_b_gate, gla_g_out, ab_w_out, c_w_qkv, diff_lambda,
           diff_g_out, c_w_out, w_ffn_in, w_ffn_out):
    depth = w_mod.shape[0]
    bp, lp, d = x_prompt.shape
    bs, ls, _ = x_sample.shape
    n_ctx = cache_mla_ckv.shape[2]

    n_cond = 1 + bs
    rows = -(-n_cond // 8) * 8
    conds = jnp.concatenate([c_ctx[None, :], c, jnp.zeros((rows - n_cond, d), F32)], axis=0)
    mod3 = _modulation(conds, w_mod, b_mod).reshape(depth * rows, 1, 6 * d)

    per_s = ls // min(TOKEN_TILE, ls)
    rope_q = _rope_tables(ls, MLA_ROPE, MLA_NOPE, LANES)
    rope_misc = _rope_tables(ls, MLA_ROPE, 0, LANES)
    rope_diff = _rope_tables(ls, DIFF_HEAD_DIM, 0, DIFF_HEAD_DIM)

    groups = {
        "prompt": dict(x=x_prompt.reshape(bp * lp, d), b=bp, n=lp, rope=False,
                       modrow=lambda li: (lambda i: li * rows)),
        "sample": dict(x=x_sample.reshape(bs * ls, d), b=bs, n=ls, rope=True,
                       modrow=lambda li: (lambda i: li * rows + 1 + i // per_s)),
    }
    ab_states, c_states = [], []

    ab_w = [_prep_ab_weights(ai, ab_w_in, mla_g_q, mla_g_kv, mla_w_uq, mla_w_ukv, gla_w_gate_up, gla_b_gate)
            for ai in range(ab_w_in.shape[0])]
    ab_w_out_b = ab_w_out.astype(BF16)
    c_w_qkv_b = c_w_qkv.astype(BF16)
    c_w_out_b = c_w_out.astype(BF16)
    w_ffn_in_b = w_ffn_in.astype(BF16)
    w_ffn_out_b = w_ffn_out.astype(BF16)

    for name, grp in groups.items():
        x, b, n = grp["x"], grp["b"], grp["n"]
        is_sample = grp["rope"]
        tm = min(TOKEN_TILE, n)
        for li in range(depth):
            modrow = grp["modrow"](li)
            g = g_norm[li]
            if li % 2 == 0:
                ai = li // 2
                w = ab_w[ai]
                rope = rope_q + rope_misc if is_sample else None
                q, kcat, v, ckv, misc, gq, gk, gv, gout = _ab_in(x, mod3, modrow, g[0:1], w, rope, tm, n)
                sh = lambda a: a.reshape(b, n, a.shape[-1])
                if is_sample:
                    ctx = _ctx_kv(cache_mla_ckv[:, ai], cache_mla_krope[:, ai], w)
                    s0 = (jnp.swapaxes(state_gla_fwd[:, ai], -1, -2), jnp.swapaxes(state_gla_bwd[:, ai], -1, -2))
                else:
                    ctx, s0 = None, None
                mla_o = _mla_attn(q, kcat, v, ctx, min(ATTN_Q_TILE, n),
                                  max(1, min(MLA_HEADS // 2, ATTN_STEP_ROWS // n)))
                o_f, o_b, s_f, s_b = _gla(sh(gq), sh(gk), sh(gv), sh(misc), w, s0, min(GLA_BLOCK, n))
                if not is_sample:
                    ab_states.append((ckv.reshape(b, n, MLA_KV_LORA), misc[:, :MLA_ROPE].reshape(b, n, MLA_ROPE),
                                      jnp.swapaxes(s_f, -1, -2), jnp.swapaxes(s_b, -1, -2)))
                mix = mla_o.reshape(b * n, -1)
                gla = (o_f.reshape(b * n, -1), o_b.reshape(b * n, -1), gout, gla_g_out[ai][None, :])
                w_mix = ab_w_out_b[ai]
            else:
                ci = li // 2
                rope = rope_diff if is_sample else None
                outs = _c_in(x, mod3, modrow, g[0:1], c_w_qkv_b[ci], rope, not is_sample, tm, n)
                q, k, v = outs[:3]
                if is_sample:
                    head_major = lambda a: jnp.swapaxes(a.reshape(bs, n_ctx, DIFF_HEADS, LANES), 1, 2).astype(BF16)
                    ctx = (head_major(cache_diff_k[:, ci]), head_major(cache_diff_v[:, ci]))
                else:
                    ctx = None
                    c_states.append((outs[3].reshape(b, n, DIFF_HEADS, 2, DIFF_HEAD_DIM),
                                     outs[4].reshape(b, n, DIFF_HEADS, 2 * DIFF_HEAD_DIM)))
                o = _diff_attn(q, k, v, ctx, diff_lambda[ci], diff_g_out[ci][None, :],
                               _diff_lambda_init(li), min(ATTN_Q_TILE, n),
                               max(1, min(DIFF_HEADS, ATTN_STEP_ROWS // n)))
                mix, gla, w_mix = o.reshape(b * n, -1), None, c_w_out_b[ci]
            tm_tail = tm if is_sample else min(TOKEN_TILE, b * n)
            x = _layer_tail(mix, gla, w_mix, x, mod3, modrow, g, w_ffn_in_b[li], w_ffn_out_b[li], tm_tail)
        grp["y"] = x.reshape(b, n, d)

    stack = lambda states, i: jnp.stack([s[i] for s in states], axis=1)
    return (groups["prompt"]["y"], groups["sample"]["y"],
            stack(ab_states, 0), stack(ab_states, 1), stack(ab_states, 2), stack(ab_states, 3),
            stack(c_states, 0), stack(c_states, 1))
```

```python
import functools
import math

import numpy as np
import jax
import jax.numpy as jnp
from jax import lax
from jax.experimental import pallas as pl
from jax.experimental.pallas import tpu as pltpu

F32 = jnp.float32
BF16 = jnp.bfloat16

D_MODEL = 1024
GRID_W = 64
ROPE_BASE = 10000.0
NORM_EPS = 1e-6

MLA_HEADS = 8
MLA_NOPE = 64
MLA_ROPE = 32
MLA_V = 64
MLA_Q_LORA = 384
MLA_KV_LORA = 256
GLA_HEADS = 4
GLA_DK = 128
GLA_DV = 128
GLA_GATE_RANK = 16
GLA_GATE_NORM = 16.0
GLA_CHUNK = 64
DIFF_HEADS = 8
DIFF_HEAD_DIM = 64
FFN_HIDDEN = 2816

LANES = 128
VMEM_LIMIT = 56 * 1024 * 1024

_OFF_QLAT = 0
_OFF_MISC = _OFF_QLAT + MLA_Q_LORA
_OFF_KVLAT = _OFF_MISC + LANES
_OFF_GQ = _OFF_KVLAT + MLA_KV_LORA
_OFF_GK = _OFF_GQ + GLA_HEADS * GLA_DK
_OFF_GV = _OFF_GK + GLA_HEADS * GLA_DK
_OFF_GOUT = _OFF_GV + GLA_HEADS * GLA_DV
_AB_COLS = _OFF_GOUT + GLA_HEADS * GLA_DV
_MISC_GATE = MLA_ROPE

LOG2E = math.log2(math.e)

TOKEN_TILE = 512
ATTN_Q_TILE = 1024
ATTN_K_TILE = 256
ATTN_STEP_ROWS = 8192
GLA_BLOCK = 256
GLA_BATCH = 2
FFN_SPLIT = 11


def _rms(x, g):
    var = jnp.mean(x * x, axis=-1, keepdims=True)
    return x * lax.rsqrt(var + NORM_EPS) * g


def _silu(x):
    return x * (1.0 / (1.0 + jnp.exp(-x)))


def _log_sigmoid(x):
    return jnp.minimum(x, 0.0) - jnp.log(1.0 + jnp.exp(-jnp.abs(x)))


def _rope(x, c, se, so):
    n = x.shape[-1]
    return x * c + pltpu.roll(x, n - 1, 1) * se + pltpu.roll(x, 1, 1) * so


def _dot(a, b):
    return jnp.dot(a, b, preferred_element_type=F32)


def _dot_nt(a, b):
    return lax.dot_general(a, b, (((1,), (1,)), ((), ())), preferred_element_type=F32)


def _dot_tn(a, b):
    return lax.dot_general(a, b, (((0,), (0,)), ((), ())), preferred_element_type=F32)


def _params(*sem):
    return pltpu.CompilerParams(dimension_semantics=sem, vmem_limit_bytes=VMEM_LIMIT)


def _full(shape):
    zeros = (0,) * len(shape)
    return pl.BlockSpec(shape, lambda *_: zeros)


def _rope_tables(n_tokens, rot_dim, lane_lo, period):
    rows = n_tokens // GRID_W
    row = np.repeat(np.arange(rows, dtype=np.float64), GRID_W)
    col = np.tile(np.arange(GRID_W, dtype=np.float64), rows)
    n_freq = rot_dim // 4
    inv = ROPE_BASE ** (-np.arange(n_freq, dtype=np.float64) / n_freq)
    ang = np.concatenate([row[:, None] * inv, col[:, None] * inv], axis=-1)
    cos, sin = np.cos(ang), np.sin(ang)
    c = np.ones((n_tokens, period))
    se = np.zeros((n_tokens, period))
    so = np.zeros((n_tokens, period))
    c[:, lane_lo:lane_lo + rot_dim] = np.repeat(cos, 2, axis=-1)
    se[:, lane_lo:lane_lo + rot_dim:2] = -sin
    so[:, lane_lo + 1:lane_lo + rot_dim:2] = sin
    reps = LANES // period
    return tuple(jnp.asarray(np.tile(t, (1, reps)), dtype=F32) for t in (c, se, so))


def _mod_kernel(c_ref, w_ref, b_ref, o_ref):
    a = _silu(c_ref[...]).astype(BF16)
    o_ref[0] = _dot(a, w_ref[0].astype(BF16)) + b_ref[0]


def _modulation(conds, w_mod, b_mod):
    depth, d, n = w_mod.shape
    rows = conds.shape[0]
    tn = 1536
    return pl.pallas_call(
        _mod_kernel,
        grid=(depth, n // tn),
        in_specs=[_full((rows, d)),
                  pl.BlockSpec((1, d, tn), lambda l, j: (l, 0, j)),
                  pl.BlockSpec((1, 1, tn), lambda l, j: (l, 0, j))],
        out_specs=pl.BlockSpec((1, rows, tn), lambda l, j: (l, 0, j)),
        out_shape=jax.ShapeDtypeStruct((depth, rows, n), F32),
        compiler_params=_params("arbitrary", "arbitrary"),
        name="modulation",
    )(conds, w_mod, b_mod.reshape(depth, 1, n))


def _ab_in_kernel(use_rope, *refs):
    (x_ref, mod_ref, g0_ref, win_ref, gq_ref, gkv_ref, wq_ref, wk_ref, wv_ref) = refs[:9]
    refs = refs[9:]
    if use_rope:
        cq, seq, soq, cm, sem, som = refs[:6]
        refs = refs[6:]
    (q_out, kcat_out, v_out, ckv_out, misc_out, gq_out, gk_out, gv_out, gout_out) = refs

    mod = mod_ref[0]
    shift, scale = mod[:, 0:D_MODEL], mod[:, D_MODEL:2 * D_MODEL]
    hb = (_rms(x_ref[...], g0_ref[...]) * (1.0 + scale) + shift).astype(BF16)
    project = lambda lo, hi: _dot(hb, win_ref[:, lo:hi])

    q_misc = project(_OFF_QLAT, _OFF_KVLAT)
    misc = q_misc[:, _OFF_MISC:]
    if use_rope:
        misc = _rope(misc, cm[...], sem[...], som[...])
    misc_out[...] = misc

    ckv = _rms(project(_OFF_KVLAT, _OFF_GQ), gkv_ref[...])
    ckv_out[...] = ckv
    ckv_b = ckv.astype(BF16)
    key_rope = pltpu.roll(misc, MLA_NOPE, 1)
    lane = lax.broadcasted_iota(jnp.int32, misc.shape, 1)
    rope_lanes = (lane >= MLA_NOPE) & (lane < MLA_NOPE + MLA_ROPE)
    pair = 2 * LANES
    for hp in range(MLA_HEADS // 2):
        k_nope = _dot(ckv_b, wk_ref[:, hp * pair:(hp + 1) * pair])
        for e in range(2):
            kh = jnp.where(rope_lanes, key_rope, k_nope[:, e * LANES:(e + 1) * LANES])
            kcat_out[0, 2 * hp + e] = kh.astype(BF16)
    v = _dot(ckv_b, wv_ref[...]).astype(BF16)
    for hp in range(MLA_HEADS // 2):
        v_out[0, hp] = v[:, hp * LANES:(hp + 1) * LANES]

    qn = _rms(q_misc[:, :_OFF_MISC], gq_ref[...]).astype(BF16)
    sm_scale = (MLA_NOPE + MLA_ROPE) ** -0.5 * LOG2E
    for hp in range(MLA_HEADS // 2):
        q2 = _dot(qn, wq_ref[:, hp * pair:(hp + 1) * pair])
        for e in range(2):
            qh = q2[:, e * LANES:(e + 1) * LANES]
            if use_rope:
                qh = _rope(qh, cq[...], seq[...], soq[...])
            q_out[0, 2 * hp + e] = (qh * sm_scale).astype(BF16)

    gq_out[...] = (project(_OFF_GQ, _OFF_GK) * (GLA_DK ** -0.5)).astype(BF16)
    gk_out[...] = project(_OFF_GK, _OFF_GV).astype(BF16)
    gv_out[...] = project(_OFF_GV, _OFF_GOUT).astype(BF16)
    gout_out[...] = project(_OFF_GOUT, _AB_COLS).astype(BF16)


def _ab_in(x, mod3, modrow, g0, w, rope, tm, seq_len):
    t = x.shape[0]
    nt = t // tm
    tok = lambda n: pl.BlockSpec((tm, n), lambda i: (i, 0))
    in_specs = [tok(D_MODEL),
                pl.BlockSpec((1, 1, 6 * D_MODEL), lambda i: (modrow(i), 0, 0)),
                _full((1, D_MODEL)), _full((D_MODEL, _AB_COLS)),
                _full((1, MLA_Q_LORA)), _full((1, MLA_KV_LORA)),
                _full((MLA_Q_LORA, MLA_HEADS * LANES)), _full((MLA_KV_LORA, MLA_HEADS * LANES)),
                _full((MLA_KV_LORA, MLA_HEADS * MLA_V))]
    args = [x, mod3, g0, w["win"], w["g_q"], w["g_kv"], w["wq"], w["wk"], w["wv"]]
    per = seq_len // tm
    if rope is not None:
        in_specs += [pl.BlockSpec((tm, LANES), lambda i: (i % per, 0))] * 6
        args += list(rope)
    nb = t // seq_len
    heads = lambda nh: pl.BlockSpec((1, nh, tm, LANES), lambda i: (i // per, 0, i % per, 0))
    head_shape = lambda nh: jax.ShapeDtypeStruct((nb, nh, seq_len, LANES), BF16)
    widths = [(MLA_KV_LORA, F32), (LANES, F32)] + [(GLA_HEADS * GLA_DK, BF16)] * 4
    return pl.pallas_call(
        functools.partial(_ab_in_kernel, rope is not None),
        grid=(nt,),
        in_specs=in_specs,
        out_specs=[heads(MLA_HEADS), heads(MLA_HEADS), heads(MLA_HEADS // 2)] + [tok(n) for n, _ in widths],
        out_shape=[head_shape(MLA_HEADS), head_shape(MLA_HEADS), head_shape(MLA_HEADS // 2)]
        + [jax.ShapeDtypeStruct((t, n), dt) for n, dt in widths],
        compiler_params=_params("arbitrary"),
        name="ab_in",
    )(*args)


def _ctx_kv_kernel(ckv_ref, kr_ref, wk_ref, wv_ref, e_ref, kcat_out, v_out):
    c = ckv_ref[...].astype(BF16)
    kr = kr_ref[...].astype(BF16)
    kcat = (_dot(c, wk_ref[...]) + _dot(kr, e_ref[...])).astype(BF16)
    v = _dot(c, wv_ref[...]).astype(BF16)
    for hd in range(MLA_HEADS):
        kcat_out[0, hd] = kcat[:, hd * LANES:(hd + 1) * LANES]
    for hp in range(MLA_HEADS // 2):
        v_out[0, hp] = v[:, hp * LANES:(hp + 1) * LANES]


def _ctx_kv(ckv, krope, w):
    b, n_ctx, _ = ckv.shape
    tok = lambda n: pl.BlockSpec((None, n_ctx, n), lambda i: (i, 0, 0))
    heads = lambda nh: pl.BlockSpec((1, nh, n_ctx, LANES), lambda i: (i, 0, 0, 0))
    return pl.pallas_call(
        _ctx_kv_kernel,
        grid=(b,),
        in_specs=[tok(MLA_KV_LORA), tok(MLA_ROPE),
                  _full((MLA_KV_LORA, MLA_HEADS * LANES)), _full((MLA_KV_LORA, MLA_HEADS * MLA_V)),
                  _full((MLA_ROPE, MLA_HEADS * LANES))],
        out_specs=[heads(MLA_HEADS), heads(MLA_HEADS // 2)],
        out_shape=[jax.ShapeDtypeStruct((b, MLA_HEADS, n_ctx, LANES), BF16),
                   jax.ShapeDtypeStruct((b, MLA_HEADS // 2, n_ctx, LANES), BF16)],
        compiler_params=_params("arbitrary"),
        name="mla_ctx_kv",
    )(ckv, krope, w["wk"], w["wv"], w["e"])


def _two_map_flash(q0, q1, sources):
    state = [None, None]
    for k0_ref, k1_ref, v_ref in sources:
        rows = v_ref.shape[0]
        tk = min(ATTN_K_TILE, rows)
        ones = jnp.ones((tk, LANES), BF16)
        for c in range(rows // tk):
            r = slice(c * tk, (c + 1) * tk)
            v = jnp.concatenate([v_ref[r, :], ones], axis=1)
            for j, (q, k_ref) in enumerate(((q0, k0_ref), (q1, k1_ref))):
                s = _dot_nt(q, k_ref[r, :])
                m_blk = jnp.max(s, axis=-1, keepdims=True)
                if state[j] is None:
                    p = jnp.exp2(s - m_blk)
                    state[j] = (m_blk, _dot(p.astype(BF16), v))
                else:
                    m, acc = state[j]
                    m_new = jnp.maximum(m, m_blk)
                    p = jnp.exp2(s - m_new)
                    state[j] = (m_new, jnp.exp2(m - m_new) * acc + _dot(p.astype(BF16), v))
    return tuple((acc[:, LANES:], acc[:, :LANES]) for _, acc in state)


def _mla_attn_kernel(has_ctx, *refs):
    if has_ctx:
        q_ref, kn, vn, kc, vc, o_ref = refs
    else:
        q_ref, kn, vn, o_ref = refs
    for i in range(vn.shape[0]):
        e, o = 2 * i, 2 * i + 1
        sources = [(kn.at[e], kn.at[o], vn.at[i])]
        if has_ctx:
            sources.insert(0, (kc.at[e], kc.at[o], vc.at[i]))
        (l0, acc0), (l1, acc1) = _two_map_flash(q_ref[e], q_ref[o], sources)
        lane = lax.broadcasted_iota(jnp.int32, acc0.shape, 1)
        o_ref[:, i * LANES:(i + 1) * LANES] = jnp.where(
            lane < MLA_V, acc0 * (1.0 / l0), acc1 * (1.0 / l1)).astype(BF16)


def _mla_attn(q, k_new, v_new, ctx, tq, pairs_per_step):
    b, nh, n, _ = q.shape
    pp = pairs_per_step
    slab = lambda heads, rows, f: pl.BlockSpec((None, heads, rows, LANES), f)
    whole = lambda bi, p, qi: (bi, p, 0, 0)
    in_specs = [slab(2 * pp, tq, lambda bi, p, qi: (bi, p, qi, 0)), slab(2 * pp, n, whole), slab(pp, n, whole)]
    args = [q, k_new, v_new]
    if ctx is not None:
        nc = ctx[0].shape[2]
        in_specs += [slab(2 * pp, nc, whole), slab(pp, nc, whole)]
        args += list(ctx)
    return pl.pallas_call(
        functools.partial(_mla_attn_kernel, ctx is not None),
        grid=(b, nh // (2 * pp), n // tq),
        in_specs=in_specs,
        out_specs=pl.BlockSpec((None, tq, pp * LANES), lambda bi, p, qi: (bi, qi, p)),
        out_shape=jax.ShapeDtypeStruct((b, n, (nh // 2) * LANES), BF16),
        compiler_params=_params("arbitrary", "arbitrary", "arbitrary"),
        name="mla_attn",
    )(*args)


def _gla_kernel(has_s0, lookahead, *refs):
    fwd_in, bwd_in = refs[0:4], refs[4:8]
    refs = refs[8:]
    if lookahead:
        next_misc = refs[:2]
        refs = refs[2:]
    wg_ref, bg_ref, tf_ref, tb_ref = refs[:4]
    refs = refs[4:]
    if has_s0:
        s0f_ref, s0b_ref = refs[:2]
        refs = refs[2:]
    of_ref, ob_ref, sf_ref, sb_ref = refs[:4]
    tb_rows = of_ref.shape[1]
    nch = tb_rows // GLA_CHUNK
    n_bb = of_ref.shape[0]
    t_refs = (tf_ref, tb_ref)

    def log_decay(m_ref, d, bb):
        pre = _dot(m_ref[bb].astype(BF16), wg_ref[d]) + bg_ref[d]
        g = _log_sigmoid(pre) * (LOG2E / GLA_GATE_NORM)
        g_hi = g.astype(BF16)
        g_lo = (g - g_hi.astype(F32)).astype(BF16)
        t = t_refs[d][...]
        return _dot(t, g_hi) + _dot(t, g_lo)

    @pl.when(pl.program_id(1) == 0)
    def _():
        if has_s0:
            sf_ref[...] = s0f_ref[...]
            sb_ref[...] = s0b_ref[...]
        else:
            sf_ref[...] = jnp.zeros(sf_ref.shape, F32)
            sb_ref[...] = jnp.zeros(sb_ref.shape, F32)
        if lookahead:
            cum_scr = refs[4]
            for d, m_ref in enumerate((fwd_in[3], bwd_in[3])):
                for bb in range(n_bb):
                    cum_scr[d, bb] = log_decay(m_ref, d, bb)

    scans = []
    for d, ((q_ref, k_ref, v_ref, m_ref), o_ref, s_ref) in enumerate(
            ((fwd_in, of_ref, sf_ref), (bwd_in, ob_ref, sb_ref))):
        row = lax.broadcasted_iota(jnp.int32, (tb_rows, tb_rows), 0)
        col = lax.broadcasted_iota(jnp.int32, (tb_rows, tb_rows), 1)
        chunk_lo = (row // GLA_CHUNK) * GLA_CHUNK
        if d == 0:
            tmask = (col <= row) & (col >= chunk_lo)
        else:
            tmask = (col >= row) & (col < chunk_lo + GLA_CHUNK)
        edges = [(n + 1) * GLA_CHUNK - 1 if d == 0 else n * GLA_CHUNK for n in range(nch)]
        for bb in range(n_bb):
            cum = refs[4][d, bb] if lookahead else log_decay(m_ref, d, bb)
            decay_rows = jnp.exp2(jnp.concatenate([cum[e:e + 1, :] for e in edges], axis=0))
            decay_cols = jnp.concatenate(
                [decay_rows, jnp.zeros((LANES - nch, decay_rows.shape[1]), F32)], axis=0).T
            for hd in range(GLA_HEADS):
                sl = slice(hd * LANES, (hd + 1) * LANES)
                scans.append(dict(d=d, bb=bb, hd=hd, sl=sl, cs=cum[:, sl], tmask=tmask,
                                  decay_row=decay_rows[:, sl], decay_col=decay_cols[sl, :],
                                  q_ref=q_ref, k_ref=k_ref, v_ref=v_ref, o_ref=o_ref, s_ref=s_ref))

    for sc in scans:
        cs, bb, sl = sc["cs"], sc["bb"], sc["sl"]
        k = sc["k_ref"][bb, :, sl].astype(F32)
        v = sc["v_ref"][bb, :, sl]
        sc["q_in"] = (sc["q_ref"][bb, :, sl].astype(F32) * jnp.exp2(cs)).astype(BF16)
        k_neg = k * jnp.exp2(-cs)
        a = jnp.where(sc["tmask"], _dot_nt(sc["q_in"], k_neg.astype(BF16)), 0.0)
        sc["o_intra"] = _dot(a.astype(BF16), v)
        sc["u"] = []
        for n in range(nch):
            r = slice(n * GLA_CHUNK, (n + 1) * GLA_CHUNK)
            k_dec = (k_neg[r] * sc["decay_row"][n:n + 1, :]).astype(BF16)
            sc["u"].append(_dot_tn(k_dec, v[r]))
        sc["s"] = sc["s_ref"][bb, sc["hd"]]

    for step in range(nch):
        for sc in scans:
            n = step if sc["d"] == 0 else nch - 1 - step
            r = slice(n * GLA_CHUNK, (n + 1) * GLA_CHUNK)
            sc["o_ref"][sc["bb"], r, sc["sl"]] = sc["o_intra"][r] + _dot(sc["q_in"][r], sc["s"].astype(BF16))
            sc["s"] = sc["s"] * sc["decay_col"][:, n:n + 1] + sc["u"][n]

    for sc in scans:
        sc["s_ref"][sc["bb"], sc["hd"]] = sc["s"]

    if lookahead:
        for d, m_ref in enumerate(next_misc):
            for bb in range(n_bb):
                refs[4][d, bb] = log_decay(m_ref, d, bb)


def _gla_masks(tb_rows):
    idx = np.arange(tb_rows)
    same = (idx[:, None] // GLA_CHUNK) == (idx[None, :] // GLA_CHUNK)
    fwd = same & (idx[None, :] <= idx[:, None])
    bwd = same & (idx[None, :] >= idx[:, None])
    return jnp.asarray(fwd, dtype=BF16), jnp.asarray(bwd, dtype=BF16)


def _gla(gq, gk, gv, misc, w, s0, tb_rows):
    b, n, width = gq.shape
    nb = n // tb_rows
    gb = GLA_BATCH
    fw = lambda cols: pl.BlockSpec((gb, tb_rows, cols), lambda bi, i: (bi, i, 0))
    bw = lambda cols: pl.BlockSpec((gb, tb_rows, cols), lambda bi, i: (bi, nb - 1 - i, 0))
    st = pl.BlockSpec((gb, GLA_HEADS, GLA_DK, GLA_DV), lambda bi, i: (bi, 0, 0, 0))
    tf, tb = _gla_masks(tb_rows)
    lookahead = nb > 1
    in_specs = [fw(width)] * 3 + [fw(LANES)] + [bw(width)] * 3 + [bw(LANES)]
    args = [gq, gk, gv, misc, gq, gk, gv, misc]
    if lookahead:
        in_specs += [pl.BlockSpec((gb, tb_rows, LANES), lambda bi, i: (bi, jnp.minimum(i + 1, nb - 1), 0)),
                     pl.BlockSpec((gb, tb_rows, LANES), lambda bi, i: (bi, jnp.maximum(nb - 2 - i, 0), 0))]
        args += [misc, misc]
    in_specs += [_full((2, LANES, width)), _full((2, 1, width)),
                 _full((tb_rows, tb_rows)), _full((tb_rows, tb_rows))]
    args += [w["wg"], w["bg"], tf, tb]
    if s0 is not None:
        in_specs += [st, st]
        args += list(s0)
    scratch = [pltpu.VMEM((2, gb, tb_rows, width), F32)] if lookahead else []
    return pl.pallas_call(
        functools.partial(_gla_kernel, s0 is not None, lookahead),
        grid=(b // gb, nb),
        in_specs=in_specs,
        out_specs=[fw(width), bw(width), st, st],
        out_shape=[jax.ShapeDtypeStruct((b, n, width), F32)] * 2
        + [jax.ShapeDtypeStruct((b, GLA_HEADS, GLA_DK, GLA_DV), F32)] * 2,
        scratch_shapes=scratch,
        compiler_params=_params("arbitrary", "arbitrary"),
        name="gla",
    )(*args)


def _layer_tail_kernel(has_gla, *refs):
    if has_gla:
        mix_ref, of_ref, ob_ref, gout_ref, ggo_ref = refs[:5]
        refs = refs[5:]
    else:
        mix_ref = refs[0]
        refs = refs[1:]
    wmix_ref, x_ref, mod_ref, g_ref, win_ref, wo_ref, o_ref = refs
    mod = mod_ref[0]
    if has_gla:
        o = of_ref[...] + ob_ref[...]
        parts = []
        for hd in range(GLA_HEADS):
            sl = slice(hd * LANES, (hd + 1) * LANES)
            gate = gout_ref[:, sl].astype(F32)
            parts.append((_rms(o[:, sl], ggo_ref[...]) * _silu(gate)).astype(BF16))
        n_mla = mix_ref.shape[1]
        out = (_dot(mix_ref[...], wmix_ref[0:n_mla, :])
               + _dot(jnp.concatenate(parts, axis=-1), wmix_ref[n_mla:, :]))
    else:
        out = _dot(mix_ref[...], wmix_ref[...])
    gate1 = mod[:, 2 * D_MODEL:3 * D_MODEL]
    x1 = x_ref[...] + gate1 * _rms(out, g_ref[1:2, :])
    shift, scale = mod[:, 3 * D_MODEL:4 * D_MODEL], mod[:, 4 * D_MODEL:5 * D_MODEL]
    hb = (_rms(x1, g_ref[2:3, :]) * (1.0 + scale) + shift).astype(BF16)

    th = FFN_HIDDEN // FFN_SPLIT
    f = None
    for s in range(FFN_SPLIT):
        gate = _dot(hb, win_ref[:, s * th:(s + 1) * th])
        up = _dot(hb, win_ref[:, FFN_HIDDEN + s * th:FFN_HIDDEN + (s + 1) * th])
        part = _dot((_silu(gate) * up).astype(BF16), wo_ref[s * th:(s + 1) * th, :])
        f = part if f is None else f + part

    gate2 = mod[:, 5 * D_MODEL:6 * D_MODEL]
    o_ref[...] = x1 + gate2 * _rms(f, g_ref[3:4, :])


def _layer_tail(mix, gla, w_mix, x, mod3, modrow, g, w_in, w_out, tm):
    t = x.shape[0]
    tok = lambda n: pl.BlockSpec((tm, n), lambda i: (i, 0))
    resident = lambda shape: pl.BlockSpec(shape, lambda i: (0,) * len(shape), pipeline_mode=pl.Buffered(1))
    in_specs = [tok(mix.shape[1])]
    args = [mix]
    if gla is not None:
        o_f, o_b, gout, g_gla = gla
        in_specs += [tok(o_f.shape[1])] * 3 + [resident((1, GLA_DV))]
        args += [o_f, o_b, gout, g_gla]
    in_specs += [resident(w_mix.shape), tok(D_MODEL),
                 pl.BlockSpec((1, 1, 6 * D_MODEL), lambda i: (modrow(i), 0, 0)), resident(g.shape),
                 resident(w_in.shape), resident(w_out.shape)]
    args += [w_mix, x, mod3, g, w_in, w_out]
    return pl.pallas_call(
        functools.partial(_layer_tail_kernel, gla is not None),
        grid=(t // tm,),
        in_specs=in_specs,
        out_specs=tok(D_MODEL),
        out_shape=jax.ShapeDtypeStruct((t, D_MODEL), F32),
        compiler_params=_params("arbitrary"),
        name="layer_tail",
    )(*args)


def _c_in_kernel(use_rope, want_f32, *refs):
    x_ref, mod_ref, g0_ref, w_ref = refs[:4]
    refs = refs[4:]
    if use_rope:
        c_ref, se_ref, so_ref = refs[:3]
        refs = refs[3:]
    q_out, k_out, v_out = refs[:3]
    mod = mod_ref[0]
    shift, scale = mod[:, 0:D_MODEL], mod[:, D_MODEL:2 * D_MODEL]
    h = _rms(x_ref[...], g0_ref[...]) * (1.0 + scale) + shift
    proj = _dot(h.astype(BF16), w_ref[...])
    width = DIFF_HEADS * LANES
    sm_scale = DIFF_HEAD_DIM ** -0.5 * LOG2E
    for hd in range(DIFF_HEADS):
        sl = slice(hd * LANES, (hd + 1) * LANES)
        qh = proj[:, hd * LANES:(hd + 1) * LANES]
        kh = proj[:, width + hd * LANES:width + (hd + 1) * LANES]
        vh = proj[:, 2 * width + hd * LANES:2 * width + (hd + 1) * LANES]
        if use_rope:
            qh = _rope(qh, c_ref[...], se_ref[...], so_ref[...])
            kh = _rope(kh, c_ref[...], se_ref[...], so_ref[...])
        q_out[0, hd] = (qh * sm_scale).astype(BF16)
        k_out[0, hd] = kh.astype(BF16)
        v_out[0, hd] = vh.astype(BF16)
        if want_f32:
            refs[3][:, sl] = kh
            refs[4][:, sl] = vh


def _c_in(x, mod3, modrow, g0, w_qkv, rope, want_f32, tm, seq_len):
    t = x.shape[0]
    width = DIFF_HEADS * LANES
    tok = lambda n: pl.BlockSpec((tm, n), lambda i: (i, 0))
    in_specs = [tok(D_MODEL), pl.BlockSpec((1, 1, 6 * D_MODEL), lambda i: (modrow(i), 0, 0)),
                _full((1, D_MODEL)), _full((D_MODEL, 3 * width))]
    args = [x, mod3, g0, w_qkv]
    per = seq_len // tm
    nb = t // seq_len
    if rope is not None:
        in_specs += [pl.BlockSpec((tm, LANES), lambda i: (i % per, 0))] * 3
        args += list(rope)
    heads = pl.BlockSpec((1, DIFF_HEADS, tm, LANES), lambda i: (i // per, 0, i % per, 0))
    head_shape = jax.ShapeDtypeStruct((nb, DIFF_HEADS, seq_len, LANES), BF16)
    n_f32 = 2 if want_f32 else 0
    return pl.pallas_call(
        functools.partial(_c_in_kernel, rope is not None, want_f32),
        grid=(t // tm,),
        in_specs=in_specs,
        out_specs=[heads] * 3 + [tok(width)] * n_f32,
        out_shape=[head_shape] * 3 + [jax.ShapeDtypeStruct((t, width), F32)] * n_f32,
        compiler_params=_params("arbitrary"),
        name="c_in",
    )(*args)


def _diff_attn_kernel(has_ctx, lam_init, *refs):
    if has_ctx:
        q_ref, kn, vn, kc, vc, lam_ref, g_ref, o_ref = refs
    else:
        q_ref, kn, vn, lam_ref, g_ref, o_ref = refs
    lp = lam_ref[...]
    lam = (jnp.exp(jnp.sum(lp[0:1] * lp[1:2], axis=-1, keepdims=True))
           - jnp.exp(jnp.sum(lp[2:3] * lp[3:4], axis=-1, keepdims=True)) + lam_init)
    for i in range(q_ref.shape[0]):
        k_new, k_ctx = kn.at[i], (kc.at[i] if has_ctx else None)
        sources = [(k_new, k_new, vn.at[i])]
        if has_ctx:
            sources.insert(0, (k_ctx, k_ctx, vc.at[i]))
        qh = q_ref[i]
        lane = lax.broadcasted_iota(jnp.int32, qh.shape, 1)
        zero = jnp.zeros_like(qh)
        q0 = jnp.where(lane < DIFF_HEAD_DIM, qh, zero)
        q1 = jnp.where(lane < DIFF_HEAD_DIM, zero, qh)
        (l0, acc0), (l1, acc1) = _two_map_flash(q0, q1, sources)
        o = acc0 * (1.0 / l0) - lam * (acc1 * (1.0 / l1))
        o_ref[:, i * LANES:(i + 1) * LANES] = (_rms(o, g_ref[...]) * (1.0 - lam_init)).astype(BF16)


def _diff_attn(q, k_new, v_new, ctx, lam_p, g_out, lam_init, tq, heads_per_step):
    b, nh, n, _ = q.shape
    hps = heads_per_step
    slab = lambda rows, f: pl.BlockSpec((None, hps, rows, LANES), f)
    whole = lambda bi, h, qi: (bi, h, 0, 0)
    in_specs = [slab(tq, lambda bi, h, qi: (bi, h, qi, 0)), slab(n, whole), slab(n, whole)]
    args = [q, k_new, v_new]
    if ctx is not None:
        in_specs += [slab(ctx[0].shape[2], whole)] * 2
        args += list(ctx)
    in_specs += [_full(lam_p.shape), _full(g_out.shape)]
    args += [lam_p, g_out]
    return pl.pallas_call(
        functools.partial(_diff_attn_kernel, ctx is not None, lam_init),
        grid=(b, nh // hps, n // tq),
        in_specs=in_specs,
        out_specs=pl.BlockSpec((None, tq, hps * LANES), lambda bi, h, qi: (bi, qi, h)),
        out_shape=jax.ShapeDtypeStruct((b, n, nh * LANES), BF16),
        compiler_params=_params("arbitrary", "arbitrary", "arbitrary"),
        name="diff_attn",
    )(*args)


def _prep_ab_weights(ai, ab_w_in, mla_g_q, mla_g_kv, mla_w_uq, mla_w_ukv, gla_w_gate_up, gla_b_gate):
    w = ab_w_in[ai]
    d = w.shape[0]
    o = np.cumsum([0, MLA_Q_LORA, MLA_KV_LORA, MLA_ROPE, GLA_HEADS * GLA_DK, GLA_HEADS * GLA_DK,
                   GLA_HEADS * GLA_DV, 2 * GLA_GATE_RANK, GLA_HEADS * GLA_DV])
    q_lat, kv_lat, k_rope, gq, gk, gv, ggate, gout = (w[:, o[i]:o[i + 1]] for i in range(8))
    pad = jnp.zeros((d, LANES - MLA_ROPE - 2 * GLA_GATE_RANK), w.dtype)
    win = jnp.concatenate([q_lat, k_rope, ggate, pad, kv_lat, gq, gk, gv, gout], axis=1).astype(BF16)

    head_pad = LANES - MLA_NOPE - MLA_ROPE
    wq = mla_w_uq[ai].reshape(MLA_Q_LORA, MLA_HEADS, MLA_NOPE + MLA_ROPE)
    wq = jnp.pad(wq, ((0, 0), (0, 0), (0, head_pad))).reshape(MLA_Q_LORA, MLA_HEADS * LANES).astype(BF16)
    wkv = mla_w_ukv[ai].reshape(MLA_KV_LORA, MLA_HEADS, MLA_NOPE + MLA_V)
    wk = jnp.pad(wkv[:, :, :MLA_NOPE], ((0, 0), (0, 0), (0, LANES - MLA_NOPE)))
    wk = wk.reshape(MLA_KV_LORA, MLA_HEADS * LANES).astype(BF16)
    wv = wkv[:, :, MLA_NOPE:].reshape(MLA_KV_LORA, MLA_HEADS * MLA_V).astype(BF16)

    e = np.zeros((MLA_ROPE, MLA_HEADS * LANES), np.float32)
    for hd in range(MLA_HEADS):
        e[np.arange(MLA_ROPE), hd * LANES + MLA_NOPE + np.arange(MLA_ROPE)] = 1.0

    wg = jnp.zeros((2, LANES, GLA_HEADS * GLA_DK), F32)
    for dr in range(2):
        lo = _MISC_GATE + dr * GLA_GATE_RANK
        wg = wg.at[dr, lo:lo + GLA_GATE_RANK, :].set(gla_w_gate_up[ai, dr])
    return dict(win=win, g_q=mla_g_q[ai][None, :], g_kv=mla_g_kv[ai][None, :], wq=wq, wk=wk, wv=wv,
                e=jnp.asarray(e, dtype=BF16), wg=wg.astype(BF16),
                bg=gla_b_gate[ai].reshape(2, 1, GLA_HEADS * GLA_DK))


def _diff_lambda_init(layer):
    return 0.8 - 0.6 * math.exp(-0.3 * layer)


def kernel(x_prompt, x_sample, cache_mla_ckv, cache_mla_krope, state_gla_fwd, state_gla_bwd,
           cache_diff_k, cache_diff_v, c, c_ctx, w_mod, b_mod, g_norm, ab_w_in, mla_g_q, mla_g_kv,
           mla_w_uq, mla_w_ukv, gla_w_gate_up, gla_b_gate, gla_g_out, ab_w_out, c_w_qkv, diff_lambda,
           diff_g_out, c_w_out, w_ffn_in, w_ffn_out):
    depth = w_mod.shape[0]
    bp, lp, d = x_prompt.shape
    bs, ls, _ = x_sample.shape
    n_ctx = cache_mla_ckv.shape[2]

    n_cond = 1 + bs
    rows = -(-n_cond // 8) * 8
    conds = jnp.concatenate([c_ctx[None, :], c, jnp.zeros((rows - n_cond, d), F32)], axis=0)
    mod3 = _modulation(conds, w_mod, b_mod).reshape(depth * rows, 1, 6 * d)

    per_s = ls // min(TOKEN_TILE, ls)
    rope_q = _rope_tables(ls, MLA_ROPE, MLA_NOPE, LANES)
    rope_misc = _rope_tables(ls, MLA_ROPE, 0, LANES)
    rope_diff = _rope_tables(ls, DIFF_HEAD_DIM, 0, DIFF_HEAD_DIM)

    groups = {
        "prompt": dict(x=x_prompt.reshape(bp * lp, d), b=bp, n=lp, rope=False,
                       modrow=lambda li: (lambda i: li * rows)),
        "sample": dict(x=x_sample.reshape(bs * ls, d), b=bs, n=ls, rope=True,
                       modrow=lambda li: (lambda i: li * rows + 1 + i // per_s)),
    }
    ab_states, c_states = [], []

    ab_w = [_prep_ab_weights(ai, ab_w_in, mla_g_q, mla_g_kv, mla_w_uq, mla_w_ukv, gla_w_gate_up, gla_b_gate)
            for ai in range(ab_w_in.shape[0])]
    ab_w_out_b = ab_w_out.astype(BF16)
    c_w_qkv_b = c_w_qkv.astype(BF16)
    c_w_out_b = c_w_out.astype(BF16)
    w_ffn_in_b = w_ffn_in.astype(BF16)
    w_ffn_out_b = w_ffn_out.astype(BF16)

    for name, grp in groups.items():
        x, b, n = grp["x"], grp["b"], grp["n"]
        is_sample = grp["rope"]
        tm = min(TOKEN_TILE, n)
        for li in range(depth):
            modrow = grp["modrow"](li)
            g = g_norm[li]
            if li % 2 == 0:
                ai = li // 2
                w = ab_w[ai]
                rope = rope_q + rope_misc if is_sample else None
                q, kcat, v, ckv, misc, gq, gk, gv, gout = _ab_in(x, mod3, modrow, g[0:1], w, rope, tm, n)
                sh = lambda a: a.reshape(b, n, a.shape[-1])
                if is_sample:
                    ctx = _ctx_kv(cache_mla_ckv[:, ai], cache_mla_krope[:, ai], w)
                    s0 = (state_gla_fwd[:, ai], state_gla_bwd[:, ai])
                else:
                    ctx, s0 = None, None
                mla_o = _mla_attn(q, kcat, v, ctx, min(ATTN_Q_TILE, n),
                                  max(1, min(MLA_HEADS // 2, ATTN_STEP_ROWS // n)))
                o_f, o_b, s_f, s_b = _gla(sh(gq), sh(gk), sh(gv), sh(misc), w, s0, min(GLA_BLOCK, n))
                if not is_sample:
                    ab_states.append((ckv.reshape(b, n, MLA_KV_LORA), misc[:, :MLA_ROPE].reshape(b, n, MLA_ROPE),
                                      s_f, s_b))
                mix = mla_o.reshape(b * n, -1)
                gla = (o_f.reshape(b * n, -1), o_b.reshape(b * n, -1), gout, gla_g_out[ai][None, :])
                w_mix = ab_w_out_b[ai]
            else:
                ci = li // 2
                rope = rope_diff if is_sample else None
                outs = _c_in(x, mod3, modrow, g[0:1], c_w_qkv_b[ci], rope, not is_sample, tm, n)
                q, k, v = outs[:3]
                if is_sample:
                    head_major = lambda a: jnp.swapaxes(a.reshape(bs, n_ctx, DIFF_HEADS, LANES), 1, 2).astype(BF16)
                    ctx = (head_major(cache_diff_k[:, ci]), head_major(cache_diff_v[:, ci]))
                else:
                    ctx = None
                    c_states.append((outs[3].reshape(b, n, DIFF_HEADS, 2, DIFF_HEAD_DIM),
                                     outs[4].reshape(b, n, DIFF_HEADS, 2 * DIFF_HEAD_DIM)))
                o = _diff_attn(q, k, v, ctx, diff_lambda[ci], diff_g_out[ci][None, :],
                               _diff_lambda_init(li), min(ATTN_Q_TILE, n),
                               max(1, min(DIFF_HEADS, ATTN_STEP_ROWS // n)))
                mix, gla, w_mix = o.reshape(b * n, -1), None, c_w_out_b[ci]
            tm_tail = tm if is_sample else min(TOKEN_TILE, b * n)
            x = _layer_tail(mix, gla, w_mix, x, mod3, modrow, g, w_ffn_in_b[li], w_ffn_out_b[li], tm_tail)
        grp["y"] = x.reshape(b, n, d)

    stack = lambda states, i: jnp.stack([s[i] for s in states], axis=1)
    return (groups["prompt"]["y"], groups["sample"]["y"],
            stack(ab_states, 0), stack(ab_states, 1), stack(ab_states, 2), stack(ab_states, 3),
            stack(c_states, 0), stack(c_states, 1))
```

```python
import functools
import math

import numpy as np
import jax
import jax.numpy as jnp
from jax import lax
from jax.experimental import pallas as pl
from jax.experimental.pallas import tpu as pltpu

F32 = jnp.float32
BF16 = jnp.bfloat16

D_MODEL = 1024
GRID_W = 64
ROPE_BASE = 10000.0
NORM_EPS = 1e-6

MLA_HEADS = 8
MLA_NOPE = 64
MLA_ROPE = 32
MLA_V = 64
MLA_Q_LORA = 384
MLA_KV_LORA = 256
GLA_HEADS = 4
GLA_DK = 128
GLA_DV = 128
GLA_GATE_RANK = 16
GLA_GATE_NORM = 16.0
GLA_CHUNK = 64
DIFF_HEADS = 8
DIFF_HEAD_DIM = 64
FFN_HIDDEN = 2816

LANES = 128
VMEM_LIMIT = 56 * 1024 * 1024

_OFF_QLAT = 0
_OFF_MISC = _OFF_QLAT + MLA_Q_LORA
_OFF_KVLAT = _OFF_MISC + LANES
_OFF_GQ = _OFF_KVLAT + MLA_KV_LORA
_OFF_GK = _OFF_GQ + GLA_HEADS * GLA_DK
_OFF_GV = _OFF_GK + GLA_HEADS * GLA_DK
_OFF_GOUT = _OFF_GV + GLA_HEADS * GLA_DV
_AB_COLS = _OFF_GOUT + GLA_HEADS * GLA_DV
_MISC_GATE = MLA_ROPE

LOG2E = math.log2(math.e)

TOKEN_TILE = 512
ATTN_Q_TILE = 1024
ATTN_K_TILE = 256
ATTN_STEP_ROWS = 8192
GLA_BLOCK = 256
GLA_BATCH = 2
FFN_SPLIT = 11


def _rms(x, g):
    var = jnp.mean(x * x, axis=-1, keepdims=True)
    return x * lax.rsqrt(var + NORM_EPS) * g


def _silu(x):
    return x * (1.0 / (1.0 + jnp.exp(-x)))


def _log_sigmoid(x):
    return jnp.minimum(x, 0.0) - jnp.log(1.0 + jnp.exp(-jnp.abs(x)))


def _rope(x, c, se, so):
    n = x.shape[-1]
    return x * c + pltpu.roll(x, n - 1, 1) * se + pltpu.roll(x, 1, 1) * so


def _dot(a, b):
    return jnp.dot(a, b, preferred_element_type=F32)


def _dot_nt(a, b):
    return lax.dot_general(a, b, (((1,), (1,)), ((), ())), preferred_element_type=F32)


def _dot_tn(a, b):
    return lax.dot_general(a, b, (((0,), (0,)), ((), ())), preferred_element_type=F32)


def _params(*sem):
    return pltpu.CompilerParams(dimension_semantics=sem, vmem_limit_bytes=VMEM_LIMIT)


def _full(shape):
    zeros = (0,) * len(shape)
    return pl.BlockSpec(shape, lambda *_: zeros)


def _rope_tables(n_tokens, rot_dim, lane_lo, period):
    rows = n_tokens // GRID_W
    row = np.repeat(np.arange(rows, dtype=np.float64), GRID_W)
    col = np.tile(np.arange(GRID_W, dtype=np.float64), rows)
    n_freq = rot_dim // 4
    inv = ROPE_BASE ** (-np.arange(n_freq, dtype=np.float64) / n_freq)
    ang = np.concatenate([row[:, None] * inv, col[:, None] * inv], axis=-1)
    cos, sin = np.cos(ang), np.sin(ang)
    c = np.ones((n_tokens, period))
    se = np.zeros((n_tokens, period))
    so = np.zeros((n_tokens, period))
    c[:, lane_lo:lane_lo + rot_dim] = np.repeat(cos, 2, axis=-1)
    se[:, lane_lo:lane_lo + rot_dim:2] = -sin
    so[:, lane_lo + 1:lane_lo + rot_dim:2] = sin
    reps = LANES // period
    return tuple(jnp.asarray(np.tile(t, (1, reps)), dtype=F32) for t in (c, se, so))


def _mod_kernel(c_ref, w_ref, b_ref, o_ref):
    a = _silu(c_ref[...]).astype(BF16)
    o_ref[0] = _dot(a, w_ref[0].astype(BF16)) + b_ref[0]


def _modulation(conds, w_mod, b_mod):
    depth, d, n = w_mod.shape
    rows = conds.shape[0]
    tn = 1536
    return pl.pallas_call(
        _mod_kernel,
        grid=(depth, n // tn),
        in_specs=[_full((rows, d)),
                  pl.BlockSpec((1, d, tn), lambda l, j: (l, 0, j)),
                  pl.BlockSpec((1, 1, tn), lambda l, j: (l, 0, j))],
        out_specs=pl.BlockSpec((1, rows, tn), lambda l, j: (l, 0, j)),
        out_shape=jax.ShapeDtypeStruct((depth, rows, n), F32),
        compiler_params=_params("arbitrary", "arbitrary"),
        name="modulation",
    )(conds, w_mod, b_mod.reshape(depth, 1, n))


def _ab_in_kernel(use_rope, *refs):
    (x_ref, mod_ref, g0_ref, win_ref, gq_ref, gkv_ref, wq_ref, wk_ref, wv_ref) = refs[:9]
    refs = refs[9:]
    if use_rope:
        cq, seq, soq, cm, sem, som = refs[:6]
        refs = refs[6:]
    (q_out, kcat_out, v_out, ckv_out, misc_out, gq_out, gk_out, gv_out, gout_out) = refs

    mod = mod_ref[0]
    shift, scale = mod[:, 0:D_MODEL], mod[:, D_MODEL:2 * D_MODEL]
    hb = (_rms(x_ref[...], g0_ref[...]) * (1.0 + scale) + shift).astype(BF16)
    project = lambda lo, hi: _dot(hb, win_ref[:, lo:hi])

    q_misc = project(_OFF_QLAT, _OFF_KVLAT)
    misc = q_misc[:, _OFF_MISC:]
    if use_rope:
        misc = _rope(misc, cm[...], sem[...], som[...])
    misc_out[...] = misc

    ckv = _rms(project(_OFF_KVLAT, _OFF_GQ), gkv_ref[...])
    ckv_out[...] = ckv
    ckv_b = ckv.astype(BF16)
    key_rope = pltpu.roll(misc, MLA_NOPE, 1)
    lane = lax.broadcasted_iota(jnp.int32, misc.shape, 1)
    rope_lanes = (lane >= MLA_NOPE) & (lane < MLA_NOPE + MLA_ROPE)
    pair = 2 * LANES
    for hp in range(MLA_HEADS // 2):
        k_nope = _dot(ckv_b, wk_ref[:, hp * pair:(hp + 1) * pair])
        for e in range(2):
            kh = jnp.where(rope_lanes, key_rope, k_nope[:, e * LANES:(e + 1) * LANES])
            kcat_out[0, 2 * hp + e] = kh.astype(BF16)
    v = _dot(ckv_b, wv_ref[...]).astype(BF16)
    for hp in range(MLA_HEADS // 2):
        v_out[0, hp] = v[:, hp * LANES:(hp + 1) * LANES]

    qn = _rms(q_misc[:, :_OFF_MISC], gq_ref[...]).astype(BF16)
    sm_scale = (MLA_NOPE + MLA_ROPE) ** -0.5 * LOG2E
    for hp in range(MLA_HEADS // 2):
        q2 = _dot(qn, wq_ref[:, hp * pair:(hp + 1) * pair])
        for e in range(2):
            qh = q2[:, e * LANES:(e + 1) * LANES]
            if use_rope:
                qh = _rope(qh, cq[...], seq[...], soq[...])
            q_out[0, 2 * hp + e] = (qh * sm_scale).astype(BF16)

    gq_out[...] = (project(_OFF_GQ, _OFF_GK) * (GLA_DK ** -0.5)).astype(BF16)
    gk_out[...] = project(_OFF_GK, _OFF_GV).astype(BF16)
    gv_out[...] = project(_OFF_GV, _OFF_GOUT).astype(BF16)
    gout_out[...] = project(_OFF_GOUT, _AB_COLS).astype(BF16)


def _ab_in(x, mod3, modrow, g0, w, rope, tm, seq_len):
    t = x.shape[0]
    nt = t // tm
    tok = lambda n: pl.BlockSpec((tm, n), lambda i: (i, 0))
    in_specs = [tok(D_MODEL),
                pl.BlockSpec((1, 1, 6 * D_MODEL), lambda i: (modrow(i), 0, 0)),
                _full((1, D_MODEL)), _full((D_MODEL, _AB_COLS)),
                _full((1, MLA_Q_LORA)), _full((1, MLA_KV_LORA)),
                _full((MLA_Q_LORA, MLA_HEADS * LANES)), _full((MLA_KV_LORA, MLA_HEADS * LANES)),
                _full((MLA_KV_LORA, MLA_HEADS * MLA_V))]
    args = [x, mod3, g0, w["win"], w["g_q"], w["g_kv"], w["wq"], w["wk"], w["wv"]]
    per = seq_len // tm
    if rope is not None:
        in_specs += [pl.BlockSpec((tm, LANES), lambda i: (i % per, 0))] * 6
        args += list(rope)
    nb = t // seq_len
    heads = lambda nh: pl.BlockSpec((1, nh, tm, LANES), lambda i: (i // per, 0, i % per, 0))
    head_shape = lambda nh: jax.ShapeDtypeStruct((nb, nh, seq_len, LANES), BF16)
    widths = [(MLA_KV_LORA, F32), (LANES, F32)] + [(GLA_HEADS * GLA_DK, BF16)] * 4
    return pl.pallas_call(
        functools.partial(_ab_in_kernel, rope is not None),
        grid=(nt,),
        in_specs=in_specs,
        out_specs=[heads(MLA_HEADS), heads(MLA_HEADS), heads(MLA_HEADS // 2)] + [tok(n) for n, _ in widths],
        out_shape=[head_shape(MLA_HEADS), head_shape(MLA_HEADS), head_shape(MLA_HEADS // 2)]
        + [jax.ShapeDtypeStruct((t, n), dt) for n, dt in widths],
        compiler_params=_params("arbitrary"),
        name="ab_in",
    )(*args)


def _ctx_kv_kernel(ckv_ref, kr_ref, wk_ref, wv_ref, e_ref, kcat_out, v_out):
    c = ckv_ref[...].astype(BF16)
    kr = kr_ref[...].astype(BF16)
    kcat = (_dot(c, wk_ref[...]) + _dot(kr, e_ref[...])).astype(BF16)
    v = _dot(c, wv_ref[...]).astype(BF16)
    for hd in range(MLA_HEADS):
        kcat_out[0, hd] = kcat[:, hd * LANES:(hd + 1) * LANES]
    for hp in range(MLA_HEADS // 2):
        v_out[0, hp] = v[:, hp * LANES:(hp + 1) * LANES]


def _ctx_kv(ckv, krope, w):
    b, n_ctx, _ = ckv.shape
    tok = lambda n: pl.BlockSpec((None, n_ctx, n), lambda i: (i, 0, 0))
    heads = lambda nh: pl.BlockSpec((1, nh, n_ctx, LANES), lambda i: (i, 0, 0, 0))
    return pl.pallas_call(
        _ctx_kv_kernel,
        grid=(b,),
        in_specs=[tok(MLA_KV_LORA), tok(MLA_ROPE),
                  _full((MLA_KV_LORA, MLA_HEADS * LANES)), _full((MLA_KV_LORA, MLA_HEADS * MLA_V)),
                  _full((MLA_ROPE, MLA_HEADS * LANES))],
        out_specs=[heads(MLA_HEADS), heads(MLA_HEADS // 2)],
        out_shape=[jax.ShapeDtypeStruct((b, MLA_HEADS, n_ctx, LANES), BF16),
                   jax.ShapeDtypeStruct((b, MLA_HEADS // 2, n_ctx, LANES), BF16)],
        compiler_params=_params("arbitrary"),
        name="mla_ctx_kv",
    )(ckv, krope, w["wk"], w["wv"], w["e"])


def _two_map_flash(q0, q1, sources):
    state = [None, None]
    for k0_ref, k1_ref, v_ref in sources:
        rows = v_ref.shape[0]
        tk = min(ATTN_K_TILE, rows)
        ones = jnp.ones((tk, LANES), BF16)
        for c in range(rows // tk):
            r = slice(c * tk, (c + 1) * tk)
            v = jnp.concatenate([v_ref[r, :], ones], axis=1)
            for j, (q, k_ref) in enumerate(((q0, k0_ref), (q1, k1_ref))):
                s = _dot_nt(q, k_ref[r, :])
                m_blk = jnp.max(s, axis=-1, keepdims=True)
                if state[j] is None:
                    p = jnp.exp2(s - m_blk)
                    state[j] = (m_blk, _dot(p.astype(BF16), v))
                else:
                    m, acc = state[j]
                    m_new = jnp.maximum(m, m_blk)
                    p = jnp.exp2(s - m_new)
                    state[j] = (m_new, jnp.exp2(m - m_new) * acc + _dot(p.astype(BF16), v))
    return tuple((acc[:, LANES:], acc[:, :LANES]) for _, acc in state)


def _mla_attn_kernel(has_ctx, *refs):
    if has_ctx:
        q_ref, kn, vn, kc, vc, o_ref = refs
    else:
        q_ref, kn, vn, o_ref = refs
    for i in range(vn.shape[0]):
        e, o = 2 * i, 2 * i + 1
        sources = [(kn.at[e], kn.at[o], vn.at[i])]
        if has_ctx:
            sources.insert(0, (kc.at[e], kc.at[o], vc.at[i]))
        (l0, acc0), (l1, acc1) = _two_map_flash(q_ref[e], q_ref[o], sources)
        lane = lax.broadcasted_iota(jnp.int32, acc0.shape, 1)
        o_ref[:, i * LANES:(i + 1) * LANES] = jnp.where(
            lane < MLA_V, acc0 * (1.0 / l0), acc1 * (1.0 / l1)).astype(BF16)


def _mla_attn(q, k_new, v_new, ctx, tq, pairs_per_step):
    b, nh, n, _ = q.shape
    pp = pairs_per_step
    slab = lambda heads, rows, f: pl.BlockSpec((None, heads, rows, LANES), f)
    whole = lambda bi, p, qi: (bi, p, 0, 0)
    in_specs = [slab(2 * pp, tq, lambda bi, p, qi: (bi, p, qi, 0)), slab(2 * pp, n, whole), slab(pp, n, whole)]
    args = [q, k_new, v_new]
    if ctx is not None:
        nc = ctx[0].shape[2]
        in_specs += [slab(2 * pp, nc, whole), slab(pp, nc, whole)]
        args += list(ctx)
    return pl.pallas_call(
        functools.partial(_mla_attn_kernel, ctx is not None),
        grid=(b, nh // (2 * pp), n // tq),
        in_specs=in_specs,
        out_specs=pl.BlockSpec((None, tq, pp * LANES), lambda bi, p, qi: (bi, qi, p)),
        out_shape=jax.ShapeDtypeStruct((b, n, (nh // 2) * LANES), BF16),
        compiler_params=_params("arbitrary", "arbitrary", "arbitrary"),
        name="mla_attn",
    )(*args)


def _gla_kernel(has_s0, lookahead, *refs):
    fwd_in, bwd_in = refs[0:4], refs[4:8]
    refs = refs[8:]
    if lookahead:
        next_misc = refs[:2]
        refs = refs[2:]
    wg_ref, bg_ref, tf_ref, tb_ref = refs[:4]
    refs = refs[4:]
    if has_s0:
        s0f_ref, s0b_ref = refs[:2]
        refs = refs[2:]
    of_ref, ob_ref, sf_ref, sb_ref = refs[:4]
    tb_rows = of_ref.shape[1]
    nch = tb_rows // GLA_CHUNK
    n_bb = of_ref.shape[0]
    t_refs = (tf_ref, tb_ref)

    def log_decay(m_ref, d, bb):
        pre = _dot(m_ref[bb].astype(BF16), wg_ref[d]) + bg_ref[d]
        g = _log_sigmoid(pre) * (LOG2E / GLA_GATE_NORM)
        g_hi = g.astype(BF16)
        g_lo = (g - g_hi.astype(F32)).astype(BF16)
        t = t_refs[d][...]
        return _dot(t, g_hi) + _dot(t, g_lo)

    @pl.when(pl.program_id(1) == 0)
    def _():
        if has_s0:
            sf_ref[...] = s0f_ref[...]
            sb_ref[...] = s0b_ref[...]
        else:
            sf_ref[...] = jnp.zeros(sf_ref.shape, F32)
            sb_ref[...] = jnp.zeros(sb_ref.shape, F32)
        if lookahead:
            cum_scr = refs[4]
            for d, m_ref in enumerate((fwd_in[3], bwd_in[3])):
                for bb in range(n_bb):
                    cum_scr[d, bb] = log_decay(m_ref, d, bb)

    scans = []
    for d, ((q_ref, k_ref, v_ref, m_ref), o_ref, s_ref) in enumerate(
            ((fwd_in, of_ref, sf_ref), (bwd_in, ob_ref, sb_ref))):
        row = lax.broadcasted_iota(jnp.int32, (tb_rows, tb_rows), 0)
        col = lax.broadcasted_iota(jnp.int32, (tb_rows, tb_rows), 1)
        chunk_lo = (row // GLA_CHUNK) * GLA_CHUNK
        if d == 0:
            tmask = (col <= row) & (col >= chunk_lo)
        else:
            tmask = (col >= row) & (col < chunk_lo + GLA_CHUNK)
        edges = [(n + 1) * GLA_CHUNK - 1 if d == 0 else n * GLA_CHUNK for n in range(nch)]
        for bb in range(n_bb):
            cum = refs[4][d, bb] if lookahead else log_decay(m_ref, d, bb)
            decay_rows = jnp.exp2(jnp.concatenate([cum[e:e + 1, :] for e in edges], axis=0))
            decay_cols = jnp.concatenate(
                [decay_rows, jnp.zeros((LANES - nch, decay_rows.shape[1]), F32)], axis=0).T
            for hd in range(GLA_HEADS):
                sl = slice(hd * LANES, (hd + 1) * LANES)
                scans.append(dict(d=d, bb=bb, hd=hd, sl=sl, cs=cum[:, sl], tmask=tmask,
                                  decay_row=decay_rows[:, sl], decay_col=decay_cols[sl, :],
                                  q_ref=q_ref, k_ref=k_ref, v_ref=v_ref, o_ref=o_ref, s_ref=s_ref))

    for sc in scans:
        cs, bb, sl = sc["cs"], sc["bb"], sc["sl"]
        k = sc["k_ref"][bb, :, sl].astype(F32)
        v = sc["v_ref"][bb, :, sl]
        sc["q_in"] = (sc["q_ref"][bb, :, sl].astype(F32) * jnp.exp2(cs)).astype(BF16)
        k_neg = k * jnp.exp2(-cs)
        a = jnp.where(sc["tmask"], _dot(sc["q_in"], k_neg.T.astype(BF16)), 0.0)
        sc["o_intra"] = _dot(a.astype(BF16), v)
        sc["u"] = []
        for n in range(nch):
            r = slice(n * GLA_CHUNK, (n + 1) * GLA_CHUNK)
            k_dec = (k_neg[r] * sc["decay_row"][n:n + 1, :]).astype(BF16)
            sc["u"].append(_dot_tn(k_dec, v[r]))
        sc["s"] = sc["s_ref"][bb, sc["hd"]]

    for step in range(nch):
        for sc in scans:
            n = step if sc["d"] == 0 else nch - 1 - step
            r = slice(n * GLA_CHUNK, (n + 1) * GLA_CHUNK)
            sc["o_ref"][sc["bb"], r, sc["sl"]] = sc["o_intra"][r] + _dot(sc["q_in"][r], sc["s"].astype(BF16))
            sc["s"] = sc["s"] * sc["decay_col"][:, n:n + 1] + sc["u"][n]

    for sc in scans:
        sc["s_ref"][sc["bb"], sc["hd"]] = sc["s"]

    if lookahead:
        for d, m_ref in enumerate(next_misc):
            for bb in range(n_bb):
                refs[4][d, bb] = log_decay(m_ref, d, bb)


def _gla_masks(tb_rows):
    idx = np.arange(tb_rows)
    same = (idx[:, None] // GLA_CHUNK) == (idx[None, :] // GLA_CHUNK)
    fwd = same & (idx[None, :] <= idx[:, None])
    bwd = same & (idx[None, :] >= idx[:, None])
    return jnp.asarray(fwd, dtype=BF16), jnp.asarray(bwd, dtype=BF16)


def _gla(gq, gk, gv, misc, w, s0, tb_rows):
    b, n, width = gq.shape
    nb = n // tb_rows
    gb = GLA_BATCH
    fw = lambda cols: pl.BlockSpec((gb, tb_rows, cols), lambda bi, i: (bi, i, 0))
    bw = lambda cols: pl.BlockSpec((gb, tb_rows, cols), lambda bi, i: (bi, nb - 1 - i, 0))
    st = pl.BlockSpec((gb, GLA_HEADS, GLA_DK, GLA_DV), lambda bi, i: (bi, 0, 0, 0))
    tf, tb = _gla_masks(tb_rows)
    lookahead = nb > 1
    in_specs = [fw(width)] * 3 + [fw(LANES)] + [bw(width)] * 3 + [bw(LANES)]
    args = [gq, gk, gv, misc, gq, gk, gv, misc]
    if lookahead:
        in_specs += [pl.BlockSpec((gb, tb_rows, LANES), lambda bi, i: (bi, jnp.minimum(i + 1, nb - 1), 0)),
                     pl.BlockSpec((gb, tb_rows, LANES), lambda bi, i: (bi, jnp.maximum(nb - 2 - i, 0), 0))]
        args += [misc, misc]
    in_specs += [_full((2, LANES, width)), _full((2, 1, width)),
                 _full((tb_rows, tb_rows)), _full((tb_rows, tb_rows))]
    args += [w["wg"], w["bg"], tf, tb]
    if s0 is not None:
        in_specs += [st, st]
        args += list(s0)
    scratch = [pltpu.VMEM((2, gb, tb_rows, width), F32)] if lookahead else []
    return pl.pallas_call(
        functools.partial(_gla_kernel, s0 is not None, lookahead),
        grid=(b // gb, nb),
        in_specs=in_specs,
        out_specs=[fw(width), bw(width), st, st],
        out_shape=[jax.ShapeDtypeStruct((b, n, width), F32)] * 2
        + [jax.ShapeDtypeStruct((b, GLA_HEADS, GLA_DK, GLA_DV), F32)] * 2,
        scratch_shapes=scratch,
        compiler_params=_params("arbitrary", "arbitrary"),
        name="gla",
    )(*args)


def _layer_tail_kernel(has_gla, *refs):
    if has_gla:
        mix_ref, of_ref, ob_ref, gout_ref, ggo_ref = refs[:5]
        refs = refs[5:]
    else:
        mix_ref = refs[0]
        refs = refs[1:]
    wmix_ref, x_ref, mod_ref, g_ref, win_ref, wo_ref, o_ref = refs
    mod = mod_ref[0]
    if has_gla:
        o = of_ref[...] + ob_ref[...]
        parts = []
        for hd in range(GLA_HEADS):
            sl = slice(hd * LANES, (hd + 1) * LANES)
            gate = gout_ref[:, sl].astype(F32)
            parts.append((_rms(o[:, sl], ggo_ref[...]) * _silu(gate)).astype(BF16))
        n_mla = mix_ref.shape[1]
        out = (_dot(mix_ref[...], wmix_ref[0:n_mla, :])
               + _dot(jnp.concatenate(parts, axis=-1), wmix_ref[n_mla:, :]))
    else:
        out = _dot(mix_ref[...], wmix_ref[...])
    gate1 = mod[:, 2 * D_MODEL:3 * D_MODEL]
    x1 = x_ref[...] + gate1 * _rms(out, g_ref[1:2, :])
    shift, scale = mod[:, 3 * D_MODEL:4 * D_MODEL], mod[:, 4 * D_MODEL:5 * D_MODEL]
    hb = (_rms(x1, g_ref[2:3, :]) * (1.0 + scale) + shift).astype(BF16)

    th = FFN_HIDDEN // FFN_SPLIT
    f = None
    for s in range(FFN_SPLIT):
        gate = _dot(hb, win_ref[:, s * th:(s + 1) * th])
        up = _dot(hb, win_ref[:, FFN_HIDDEN + s * th:FFN_HIDDEN + (s + 1) * th])
        part = _dot((_silu(gate) * up).astype(BF16), wo_ref[s * th:(s + 1) * th, :])
        f = part if f is None else f + part

    gate2 = mod[:, 5 * D_MODEL:6 * D_MODEL]
    o_ref[...] = x1 + gate2 * _rms(f, g_ref[3:4, :])


def _layer_tail(mix, gla, w_mix, x, mod3, modrow, g, w_in, w_out, tm):
    t = x.shape[0]
    tok = lambda n: pl.BlockSpec((tm, n), lambda i: (i, 0))
    resident = lambda shape: pl.BlockSpec(shape, lambda i: (0,) * len(shape), pipeline_mode=pl.Buffered(1))
    in_specs = [tok(mix.shape[1])]
    args = [mix]
    if gla is not None:
        o_f, o_b, gout, g_gla = gla
        in_specs += [tok(o_f.shape[1])] * 3 + [resident((1, GLA_DV))]
        args += [o_f, o_b, gout, g_gla]
    in_specs += [resident(w_mix.shape), tok(D_MODEL),
                 pl.BlockSpec((1, 1, 6 * D_MODEL), lambda i: (modrow(i), 0, 0)), resident(g.shape),
                 resident(w_in.shape), resident(w_out.shape)]
    args += [w_mix, x, mod3, g, w_in, w_out]
    return pl.pallas_call(
        functools.partial(_layer_tail_kernel, gla is not None),
        grid=(t // tm,),
        in_specs=in_specs,
        out_specs=tok(D_MODEL),
        out_shape=jax.ShapeDtypeStruct((t, D_MODEL), F32),
        compiler_params=_params("arbitrary"),
        name="layer_tail",
    )(*args)


def _c_in_kernel(use_rope, want_f32, *refs):
    x_ref, mod_ref, g0_ref, w_ref = refs[:4]
    refs = refs[4:]
    if use_rope:
        c_ref, se_ref, so_ref = refs[:3]
        refs = refs[3:]
    q_out, k_out, v_out = refs[:3]
    mod = mod_ref[0]
    shift, scale = mod[:, 0:D_MODEL], mod[:, D_MODEL:2 * D_MODEL]
    h = _rms(x_ref[...], g0_ref[...]) * (1.0 + scale) + shift
    proj = _dot(h.astype(BF16), w_ref[...])
    width = DIFF_HEADS * LANES
    sm_scale = DIFF_HEAD_DIM ** -0.5 * LOG2E
    for hd in range(DIFF_HEADS):
        sl = slice(hd * LANES, (hd + 1) * LANES)
        qh = proj[:, hd * LANES:(hd + 1) * LANES]
        kh = proj[:, width + hd * LANES:width + (hd + 1) * LANES]
        vh = proj[:, 2 * width + hd * LANES:2 * width + (hd + 1) * LANES]
        if use_rope:
            qh = _rope(qh, c_ref[...], se_ref[...], so_ref[...])
            kh = _rope(kh, c_ref[...], se_ref[...], so_ref[...])
        q_out[0, hd] = (qh * sm_scale).astype(BF16)
        k_out[0, hd] = kh.astype(BF16)
        v_out[0, hd] = vh.astype(BF16)
        if want_f32:
            refs[3][:, sl] = kh
            refs[4][:, sl] = vh


def _c_in(x, mod3, modrow, g0, w_qkv, rope, want_f32, tm, seq_len):
    t = x.shape[0]
    width = DIFF_HEADS * LANES
    tok = lambda n: pl.BlockSpec((tm, n), lambda i: (i, 0))
    in_specs = [tok(D_MODEL), pl.BlockSpec((1, 1, 6 * D_MODEL), lambda i: (modrow(i), 0, 0)),
                _full((1, D_MODEL)), _full((D_MODEL, 3 * width))]
    args = [x, mod3, g0, w_qkv]
    per = seq_len // tm
    nb = t // seq_len
    if rope is not None:
        in_specs += [pl.BlockSpec((tm, LANES), lambda i: (i % per, 0))] * 3
        args += list(rope)
    heads = pl.BlockSpec((1, DIFF_HEADS, tm, LANES), lambda i: (i // per, 0, i % per, 0))
    head_shape = jax.ShapeDtypeStruct((nb, DIFF_HEADS, seq_len, LANES), BF16)
    n_f32 = 2 if want_f32 else 0
    return pl.pallas_call(
        functools.partial(_c_in_kernel, rope is not None, want_f32),
        grid=(t // tm,),
        in_specs=in_specs,
        out_specs=[heads] * 3 + [tok(width)] * n_f32,
        out_shape=[head_shape] * 3 + [jax.ShapeDtypeStruct((t, width), F32)] * n_f32,
        compiler_params=_params("arbitrary"),
        name="c_in",
    )(*args)


def _diff_attn_kernel(has_ctx, lam_init, *refs):
    if has_ctx:
        q_ref, kn, vn, kc, vc, lam_ref, g_ref, o_ref = refs
    else:
        q_ref, kn, vn, lam_ref, g_ref, o_ref = refs
    lp = lam_ref[...]
    lam = (jnp.exp(jnp.sum(lp[0:1] * lp[1:2], axis=-1, keepdims=True))
           - jnp.exp(jnp.sum(lp[2:3] * lp[3:4], axis=-1, keepdims=True)) + lam_init)
    for i in range(q_ref.shape[0]):
        k_new, k_ctx = kn.at[i], (kc.at[i] if has_ctx else None)
        sources = [(k_new, k_new, vn.at[i])]
        if has_ctx:
            sources.insert(0, (k_ctx, k_ctx, vc.at[i]))
        qh = q_ref[i]
        lane = lax.broadcasted_iota(jnp.int32, qh.shape, 1)
        zero = jnp.zeros_like(qh)
        q0 = jnp.where(lane < DIFF_HEAD_DIM, qh, zero)
        q1 = jnp.where(lane < DIFF_HEAD_DIM, zero, qh)
        (l0, acc0), (l1, acc1) = _two_map_flash(q0, q1, sources)
        o = acc0 * (1.0 / l0) - lam * (acc1 * (1.0 / l1))
        o_ref[:, i * LANES:(i + 1) * LANES] = (_rms(o, g_ref[...]) * (1.0 - lam_init)).astype(BF16)


def _diff_attn(q, k_new, v_new, ctx, lam_p, g_out, lam_init, tq, heads_per_step):
    b, nh, n, _ = q.shape
    hps = heads_per_step
    slab = lambda rows, f: pl.BlockSpec((None, hps, rows, LANES), f)
    whole = lambda bi, h, qi: (bi, h, 0, 0)
    in_specs = [slab(tq, lambda bi, h, qi: (bi, h, qi, 0)), slab(n, whole), slab(n, whole)]
    args = [q, k_new, v_new]
    if ctx is not None:
        in_specs += [slab(ctx[0].shape[2], whole)] * 2
        args += list(ctx)
    in_specs += [_full(lam_p.shape), _full(g_out.shape)]
    args += [lam_p, g_out]
    return pl.pallas_call(
        functools.partial(_diff_attn_kernel, ctx is not None, lam_init),
        grid=(b, nh // hps, n // tq),
        in_specs=in_specs,
        out_specs=pl.BlockSpec((None, tq, hps * LANES), lambda bi, h, qi: (bi, qi, h)),
        out_shape=jax.ShapeDtypeStruct((b, n, nh * LANES), BF16),
        compiler_params=_params("arbitrary", "arbitrary", "arbitrary"),
        name="diff_attn",
    )(*args)


def _prep_ab_weights(ai, ab_w_in, mla_g_q, mla_g_kv, mla_w_uq, mla_w_ukv, gla_w_gate_up, gla_b_gate):
    w = ab_w_in[ai]
    d = w.shape[0]
    o = np.cumsum([0, MLA_Q_LORA, MLA_KV_LORA, MLA_ROPE, GLA_HEADS * GLA_DK, GLA_HEADS * GLA_DK,
                   GLA_HEADS * GLA_DV, 2 * GLA_GATE_RANK, GLA_HEADS * GLA_DV])
    q_lat, kv_lat, k_rope, gq, gk, gv, ggate, gout = (w[:, o[i]:o[i + 1]] for i in range(8))
    pad = jnp.zeros((d, LANES - MLA_ROPE - 2 * GLA_GATE_RANK), w.dtype)
    win = jnp.concatenate([q_lat, k_rope, ggate, pad, kv_lat, gq, gk, gv, gout], axis=1).astype(BF16)

    head_pad = LANES - MLA_NOPE - MLA_ROPE
    wq = mla_w_uq[ai].reshape(MLA_Q_LORA, MLA_HEADS, MLA_NOPE + MLA_ROPE)
    wq = jnp.pad(wq, ((0, 0), (0, 0), (0, head_pad))).reshape(MLA_Q_LORA, MLA_HEADS * LANES).astype(BF16)
    wkv = mla_w_ukv[ai].reshape(MLA_KV_LORA, MLA_HEADS, MLA_NOPE + MLA_V)
    wk = jnp.pad(wkv[:, :, :MLA_NOPE], ((0, 0), (0, 0), (0, LANES - MLA_NOPE)))
    wk = wk.reshape(MLA_KV_LORA, MLA_HEADS * LANES).astype(BF16)
    wv = wkv[:, :, MLA_NOPE:].reshape(MLA_KV_LORA, MLA_HEADS * MLA_V).astype(BF16)

    e = np.zeros((MLA_ROPE, MLA_HEADS * LANES), np.float32)
    for hd in range(MLA_HEADS):
        e[np.arange(MLA_ROPE), hd * LANES + MLA_NOPE + np.arange(MLA_ROPE)] = 1.0

    wg = jnp.zeros((2, LANES, GLA_HEADS * GLA_DK), F32)
    for dr in range(2):
        lo = _MISC_GATE + dr * GLA_GATE_RANK
        wg = wg.at[dr, lo:lo + GLA_GATE_RANK, :].set(gla_w_gate_up[ai, dr])
    return dict(win=win, g_q=mla_g_q[ai][None, :], g_kv=mla_g_kv[ai][None, :], wq=wq, wk=wk, wv=wv,
                e=jnp.asarray(e, dtype=BF16), wg=wg.astype(BF16),
                bg=gla_b_gate[ai].reshape(2, 1, GLA_HEADS * GLA_DK))


def _diff_lambda_init(layer):
    return 0.8 - 0.6 * math.exp(-0.3 * layer)


def kernel(x_prompt, x_sample, cache_mla_ckv, cache_mla_krope, state_gla_fwd, state_gla_bwd,
           cache_diff_k, cache_diff_v, c, c_ctx, w_mod, b_mod, g_norm, ab_w_in, mla_g_q, mla_g_kv,
           mla_w_uq, mla_w_ukv, gla_w_gate_up, gla_b_gate, gla_g_out, ab_w_out, c_w_qkv, diff_lambda,
           diff_g_out, c_w_out, w_ffn_in, w_ffn_out):
    depth = w_mod.shape[0]
    bp, lp, d = x_prompt.shape
    bs, ls, _ = x_sample.shape
    n_ctx = cache_mla_ckv.shape[2]

    n_cond = 1 + bs
    rows = -(-n_cond // 8) * 8
    conds = jnp.concatenate([c_ctx[None, :], c, jnp.zeros((rows - n_cond, d), F32)], axis=0)
    mod3 = _modulation(conds, w_mod, b_mod).reshape(depth * rows, 1, 6 * d)

    per_s = ls // min(TOKEN_TILE, ls)
    rope_q = _rope_tables(ls, MLA_ROPE, MLA_NOPE, LANES)
    rope_misc = _rope_tables(ls, MLA_ROPE, 0, LANES)
    rope_diff = _rope_tables(ls, DIFF_HEAD_DIM, 0, DIFF_HEAD_DIM)

    groups = {
        "prompt": dict(x=x_prompt.reshape(bp * lp, d), b=bp, n=lp, rope=False,
                       modrow=lambda li: (lambda i: li * rows)),
        "sample": dict(x=x_sample.reshape(bs * ls, d), b=bs, n=ls, rope=True,
                       modrow=lambda li: (lambda i: li * rows + 1 + i // per_s)),
    }
    ab_states, c_states = [], []

    ab_w = [_prep_ab_weights(ai, ab_w_in, mla_g_q, mla_g_kv, mla_w_uq, mla_w_ukv, gla_w_gate_up, gla_b_gate)
            for ai in range(ab_w_in.shape[0])]
    ab_w_out_b = ab_w_out.astype(BF16)
    c_w_qkv_b = c_w_qkv.astype(BF16)
    c_w_out_b = c_w_out.astype(BF16)
    w_ffn_in_b = w_ffn_in.astype(BF16)
    w_ffn_out_b = w_ffn_out.astype(BF16)

    for name, grp in groups.items():
        x, b, n = grp["x"], grp["b"], grp["n"]
        is_sample = grp["rope"]
        tm = min(TOKEN_TILE, n)
        for li in range(depth):
            modrow = grp["modrow"](li)
            g = g_norm[li]
            if li % 2 == 0:
                ai = li // 2
                w = ab_w[ai]
                rope = rope_q + rope_misc if is_sample else None
                q, kcat, v, ckv, misc, gq, gk, gv, gout = _ab_in(x, mod3, modrow, g[0:1], w, rope, tm, n)
                sh = lambda a: a.reshape(b, n, a.shape[-1])
                if is_sample:
                    ctx = _ctx_kv(cache_mla_ckv[:, ai], cache_mla_krope[:, ai], w)
                    s0 = (state_gla_fwd[:, ai], state_gla_bwd[:, ai])
                else:
                    ctx, s0 = None, None
                mla_o = _mla_attn(q, kcat, v, ctx, min(ATTN_Q_TILE, n),
                                  max(1, min(MLA_HEADS // 2, ATTN_STEP_ROWS // n)))
                o_f, o_b, s_f, s_b = _gla(sh(gq), sh(gk), sh(gv), sh(misc), w, s0, min(GLA_BLOCK, n))
                if not is_sample:
                    ab_states.append((ckv.reshape(b, n, MLA_KV_LORA), misc[:, :MLA_ROPE].reshape(b, n, MLA_ROPE),
                                      s_f, s_b))
                mix = mla_o.reshape(b * n, -1)
                gla = (o_f.reshape(b * n, -1), o_b.reshape(b * n, -1), gout, gla_g_out[ai][None, :])
                w_mix = ab_w_out_b[ai]
            else:
                ci = li // 2
                rope = rope_diff if is_sample else None
                outs = _c_in(x, mod3, modrow, g[0:1], c_w_qkv_b[ci], rope, not is_sample, tm, n)
                q, k, v = outs[:3]
                if is_sample:
                    head_major = lambda a: jnp.swapaxes(a.reshape(bs, n_ctx, DIFF_HEADS, LANES), 1, 2).astype(BF16)
                    ctx = (head_major(cache_diff_k[:, ci]), head_major(cache_diff_v[:, ci]))
                else:
                    ctx = None
                    c_states.append((outs[3].reshape(b, n, DIFF_HEADS, 2, DIFF_HEAD_DIM),
                                     outs[4].reshape(b, n, DIFF_HEADS, 2 * DIFF_HEAD_DIM)))
                o = _diff_attn(q, k, v, ctx, diff_lambda[ci], diff_g_out[ci][None, :],
                               _diff_lambda_init(li), min(ATTN_Q_TILE, n),
                               max(1, min(DIFF_HEADS, ATTN_STEP_ROWS // n)))
                mix, gla, w_mix = o.reshape(b * n, -1), None, c_w_out_b[ci]
            tm_tail = tm if is_sample else min(TOKEN_TILE, b * n)
            x = _layer_tail(mix, gla, w_mix, x, mod3, modrow, g, w_ffn_in_b[li], w_ffn_out_b[li], tm_tail)
        grp["y"] = x.reshape(b, n, d)

    stack = lambda states, i: jnp.stack([s[i] for s in states], axis=1)
    return (groups["prompt"]["y"], groups["sample"]["y"],
            stack(ab_states, 0), stack(ab_states, 1), stack(ab_states, 2), stack(ab_states, 3),
            stack(c_states, 0), stack(c_states, 1))
```

```python
import functools
import math

import numpy as np
import jax
import jax.numpy as jnp
from jax import lax
from jax.experimental import pallas as pl
from jax.experimental.pallas import tpu as pltpu

F32 = jnp.float32
BF16 = jnp.bfloat16

D_MODEL = 1024
GRID_W = 64
ROPE_BASE = 10000.0
NORM_EPS = 1e-6

MLA_HEADS = 8
MLA_NOPE = 64
MLA_ROPE = 32
MLA_V = 64
MLA_Q_LORA = 384
MLA_KV_LORA = 256
GLA_HEADS = 4
GLA_DK = 128
GLA_DV = 128
GLA_GATE_RANK = 16
GLA_GATE_NORM = 16.0
GLA_CHUNK = 64
DIFF_HEADS = 8
DIFF_HEAD_DIM = 64
FFN_HIDDEN = 2816

LANES = 128
VMEM_LIMIT = 56 * 1024 * 1024

_OFF_QLAT = 0
_OFF_MISC = _OFF_QLAT + MLA_Q_LORA
_OFF_KVLAT = _OFF_MISC + LANES
_OFF_GQ = _OFF_KVLAT + MLA_KV_LORA
_OFF_GK = _OFF_GQ + GLA_HEADS * GLA_DK
_OFF_GV = _OFF_GK + GLA_HEADS * GLA_DK
_OFF_GOUT = _OFF_GV + GLA_HEADS * GLA_DV
_AB_COLS = _OFF_GOUT + GLA_HEADS * GLA_DV
_MISC_GATE = MLA_ROPE

LOG2E = math.log2(math.e)

TOKEN_TILE = 512
ATTN_Q_TILE = 1024
ATTN_K_TILE = 256
ATTN_STEP_ROWS = 8192
GLA_BLOCK = 256
GLA_BATCH = 4
FFN_SPLIT = 11


def _rms(x, g):
    var = jnp.mean(x * x, axis=-1, keepdims=True)
    return x * lax.rsqrt(var + NORM_EPS) * g


def _silu(x):
    return x * (1.0 / (1.0 + jnp.exp(-x)))


def _log_sigmoid(x):
    return jnp.minimum(x, 0.0) - jnp.log(1.0 + jnp.exp(-jnp.abs(x)))


def _rope(x, c, se, so):
    n = x.shape[-1]
    return x * c + pltpu.roll(x, n - 1, 1) * se + pltpu.roll(x, 1, 1) * so


def _dot(a, b):
    return jnp.dot(a, b, preferred_element_type=F32)


def _dot_nt(a, b):
    return lax.dot_general(a, b, (((1,), (1,)), ((), ())), preferred_element_type=F32)


def _dot_tn(a, b):
    return lax.dot_general(a, b, (((0,), (0,)), ((), ())), preferred_element_type=F32)


def _params(*sem):
    return pltpu.CompilerParams(dimension_semantics=sem, vmem_limit_bytes=VMEM_LIMIT)


def _full(shape):
    zeros = (0,) * len(shape)
    return pl.BlockSpec(shape, lambda *_: zeros)


def _rope_tables(n_tokens, rot_dim, lane_lo, period):
    rows = n_tokens // GRID_W
    row = np.repeat(np.arange(rows, dtype=np.float64), GRID_W)
    col = np.tile(np.arange(GRID_W, dtype=np.float64), rows)
    n_freq = rot_dim // 4
    inv = ROPE_BASE ** (-np.arange(n_freq, dtype=np.float64) / n_freq)
    ang = np.concatenate([row[:, None] * inv, col[:, None] * inv], axis=-1)
    cos, sin = np.cos(ang), np.sin(ang)
    c = np.ones((n_tokens, period))
    se = np.zeros((n_tokens, period))
    so = np.zeros((n_tokens, period))
    c[:, lane_lo:lane_lo + rot_dim] = np.repeat(cos, 2, axis=-1)
    se[:, lane_lo:lane_lo + rot_dim:2] = -sin
    so[:, lane_lo + 1:lane_lo + rot_dim:2] = sin
    reps = LANES // period
    return tuple(jnp.asarray(np.tile(t, (1, reps)), dtype=F32) for t in (c, se, so))


def _mod_kernel(c_ref, w_ref, b_ref, o_ref):
    a = _silu(c_ref[...]).astype(BF16)
    o_ref[0] = _dot(a, w_ref[0].astype(BF16)) + b_ref[0]


def _modulation(conds, w_mod, b_mod):
    depth, d, n = w_mod.shape
    rows = conds.shape[0]
    tn = 1536
    return pl.pallas_call(
        _mod_kernel,
        grid=(depth, n // tn),
        in_specs=[_full((rows, d)),
                  pl.BlockSpec((1, d, tn), lambda l, j: (l, 0, j)),
                  pl.BlockSpec((1, 1, tn), lambda l, j: (l, 0, j))],
        out_specs=pl.BlockSpec((1, rows, tn), lambda l, j: (l, 0, j)),
        out_shape=jax.ShapeDtypeStruct((depth, rows, n), F32),
        compiler_params=_params("arbitrary", "arbitrary"),
        name="modulation",
    )(conds, w_mod, b_mod.reshape(depth, 1, n))


def _ab_in_kernel(use_rope, *refs):
    (x_ref, mod_ref, g0_ref, win_ref, gq_ref, gkv_ref, wq_ref, wk_ref, wv_ref) = refs[:9]
    refs = refs[9:]
    if use_rope:
        cq, seq, soq, cm, sem, som = refs[:6]
        refs = refs[6:]
    (q_out, kcat_out, v_out, ckv_out, misc_out, gq_out, gk_out, gv_out, gout_out) = refs

    mod = mod_ref[0]
    shift, scale = mod[:, 0:D_MODEL], mod[:, D_MODEL:2 * D_MODEL]
    hb = (_rms(x_ref[...], g0_ref[...]) * (1.0 + scale) + shift).astype(BF16)
    project = lambda lo, hi: _dot(hb, win_ref[:, lo:hi])

    q_misc = project(_OFF_QLAT, _OFF_KVLAT)
    misc = q_misc[:, _OFF_MISC:]
    if use_rope:
        misc = _rope(misc, cm[...], sem[...], som[...])
    misc_out[...] = misc

    ckv = _rms(project(_OFF_KVLAT, _OFF_GQ), gkv_ref[...])
    ckv_out[...] = ckv
    ckv_b = ckv.astype(BF16)
    key_rope = pltpu.roll(misc, MLA_NOPE, 1)
    lane = lax.broadcasted_iota(jnp.int32, misc.shape, 1)
    rope_lanes = (lane >= MLA_NOPE) & (lane < MLA_NOPE + MLA_ROPE)
    pair = 2 * LANES
    for hp in range(MLA_HEADS // 2):
        k_nope = _dot(ckv_b, wk_ref[:, hp * pair:(hp + 1) * pair])
        for e in range(2):
            kh = jnp.where(rope_lanes, key_rope, k_nope[:, e * LANES:(e + 1) * LANES])
            kcat_out[0, 2 * hp + e] = kh.astype(BF16)
    v = _dot(ckv_b, wv_ref[...]).astype(BF16)
    for hp in range(MLA_HEADS // 2):
        v_out[0, hp] = v[:, hp * LANES:(hp + 1) * LANES]

    qn = _rms(q_misc[:, :_OFF_MISC], gq_ref[...]).astype(BF16)
    sm_scale = (MLA_NOPE + MLA_ROPE) ** -0.5 * LOG2E
    for hp in range(MLA_HEADS // 2):
        q2 = _dot(qn, wq_ref[:, hp * pair:(hp + 1) * pair])
        for e in range(2):
            qh = q2[:, e * LANES:(e + 1) * LANES]
            if use_rope:
                qh = _rope(qh, cq[...], seq[...], soq[...])
            q_out[0, 2 * hp + e] = (qh * sm_scale).astype(BF16)

    gq_out[...] = (project(_OFF_GQ, _OFF_GK) * (GLA_DK ** -0.5)).astype(BF16)
    gk_out[...] = project(_OFF_GK, _OFF_GV).astype(BF16)
    gv_out[...] = project(_OFF_GV, _OFF_GOUT).astype(BF16)
    gout_out[...] = project(_OFF_GOUT, _AB_COLS).astype(BF16)


def _ab_in(x, mod3, modrow, g0, w, rope, tm, seq_len):
    t = x.shape[0]
    nt = t // tm
    tok = lambda n: pl.BlockSpec((tm, n), lambda i: (i, 0))
    in_specs = [tok(D_MODEL),
                pl.BlockSpec((1, 1, 6 * D_MODEL), lambda i: (modrow(i), 0, 0)),
                _full((1, D_MODEL)), _full((D_MODEL, _AB_COLS)),
                _full((1, MLA_Q_LORA)), _full((1, MLA_KV_LORA)),
                _full((MLA_Q_LORA, MLA_HEADS * LANES)), _full((MLA_KV_LORA, MLA_HEADS * LANES)),
                _full((MLA_KV_LORA, MLA_HEADS * MLA_V))]
    args = [x, mod3, g0, w["win"], w["g_q"], w["g_kv"], w["wq"], w["wk"], w["wv"]]
    per = seq_len // tm
    if rope is not None:
        in_specs += [pl.BlockSpec((tm, LANES), lambda i: (i % per, 0))] * 6
        args += list(rope)
    nb = t // seq_len
    heads = lambda nh: pl.BlockSpec((1, nh, tm, LANES), lambda i: (i // per, 0, i % per, 0))
    head_shape = lambda nh: jax.ShapeDtypeStruct((nb, nh, seq_len, LANES), BF16)
    widths = [(MLA_KV_LORA, F32), (LANES, F32)] + [(GLA_HEADS * GLA_DK, BF16)] * 4
    return pl.pallas_call(
        functools.partial(_ab_in_kernel, rope is not None),
        grid=(nt,),
        in_specs=in_specs,
        out_specs=[heads(MLA_HEADS), heads(MLA_HEADS), heads(MLA_HEADS // 2)] + [tok(n) for n, _ in widths],
        out_shape=[head_shape(MLA_HEADS), head_shape(MLA_HEADS), head_shape(MLA_HEADS // 2)]
        + [jax.ShapeDtypeStruct((t, n), dt) for n, dt in widths],
        compiler_params=_params("arbitrary"),
        name="ab_in",
    )(*args)


def _ctx_kv_kernel(ckv_ref, kr_ref, wk_ref, wv_ref, e_ref, kcat_out, v_out):
    c = ckv_ref[...].astype(BF16)
    kr = kr_ref[...].astype(BF16)
    kcat = (_dot(c, wk_ref[...]) + _dot(kr, e_ref[...])).astype(BF16)
    v = _dot(c, wv_ref[...]).astype(BF16)
    for hd in range(MLA_HEADS):
        kcat_out[0, hd] = kcat[:, hd * LANES:(hd + 1) * LANES]
    for hp in range(MLA_HEADS // 2):
        v_out[0, hp] = v[:, hp * LANES:(hp + 1) * LANES]


def _ctx_kv(ckv, krope, w):
    b, n_ctx, _ = ckv.shape
    tok = lambda n: pl.BlockSpec((None, n_ctx, n), lambda i: (i, 0, 0))
    heads = lambda nh: pl.BlockSpec((1, nh, n_ctx, LANES), lambda i: (i, 0, 0, 0))
    return pl.pallas_call(
        _ctx_kv_kernel,
        grid=(b,),
        in_specs=[tok(MLA_KV_LORA), tok(MLA_ROPE),
                  _full((MLA_KV_LORA, MLA_HEADS * LANES)), _full((MLA_KV_LORA, MLA_HEADS * MLA_V)),
                  _full((MLA_ROPE, MLA_HEADS * LANES))],
        out_specs=[heads(MLA_HEADS), heads(MLA_HEADS // 2)],
        out_shape=[jax.ShapeDtypeStruct((b, MLA_HEADS, n_ctx, LANES), BF16),
                   jax.ShapeDtypeStruct((b, MLA_HEADS // 2, n_ctx, LANES), BF16)],
        compiler_params=_params("arbitrary"),
        name="mla_ctx_kv",
    )(ckv, krope, w["wk"], w["wv"], w["e"])


def _two_map_flash(q0, q1, sources):
    state = [None, None]
    for k0_ref, k1_ref, v_ref in sources:
        rows = v_ref.shape[0]
        tk = min(ATTN_K_TILE, rows)
        ones = jnp.ones((tk, LANES), BF16)
        for c in range(rows // tk):
            r = slice(c * tk, (c + 1) * tk)
            v = jnp.concatenate([v_ref[r, :], ones], axis=1)
            for j, (q, k_ref) in enumerate(((q0, k0_ref), (q1, k1_ref))):
                s = _dot_nt(q, k_ref[r, :])
                m_blk = jnp.max(s, axis=-1, keepdims=True)
                if state[j] is None:
                    p = jnp.exp2(s - m_blk)
                    state[j] = (m_blk, _dot(p.astype(BF16), v))
                else:
                    m, acc = state[j]
                    m_new = jnp.maximum(m, m_blk)
                    p = jnp.exp2(s - m_new)
                    state[j] = (m_new, jnp.exp2(m - m_new) * acc + _dot(p.astype(BF16), v))
    return tuple((acc[:, LANES:], acc[:, :LANES]) for _, acc in state)


def _mla_attn_kernel(has_ctx, *refs):
    if has_ctx:
        q_ref, kn, vn, kc, vc, o_ref = refs
    else:
        q_ref, kn, vn, o_ref = refs
    for i in range(vn.shape[0]):
        e, o = 2 * i, 2 * i + 1
        sources = [(kn.at[e], kn.at[o], vn.at[i])]
        if has_ctx:
            sources.insert(0, (kc.at[e], kc.at[o], vc.at[i]))
        (l0, acc0), (l1, acc1) = _two_map_flash(q_ref[e], q_ref[o], sources)
        lane = lax.broadcasted_iota(jnp.int32, acc0.shape, 1)
        o_ref[:, i * LANES:(i + 1) * LANES] = jnp.where(
            lane < MLA_V, acc0 * (1.0 / l0), acc1 * (1.0 / l1)).astype(BF16)


def _mla_attn(q, k_new, v_new, ctx, tq, pairs_per_step):
    b, nh, n, _ = q.shape
    pp = pairs_per_step
    slab = lambda heads, rows, f: pl.BlockSpec((None, heads, rows, LANES), f)
    whole = lambda bi, p, qi: (bi, p, 0, 0)
    in_specs = [slab(2 * pp, tq, lambda bi, p, qi: (bi, p, qi, 0)), slab(2 * pp, n, whole), slab(pp, n, whole)]
    args = [q, k_new, v_new]
    if ctx is not None:
        nc = ctx[0].shape[2]
        in_specs += [slab(2 * pp, nc, whole), slab(pp, nc, whole)]
        args += list(ctx)
    return pl.pallas_call(
        functools.partial(_mla_attn_kernel, ctx is not None),
        grid=(b, nh // (2 * pp), n // tq),
        in_specs=in_specs,
        out_specs=pl.BlockSpec((None, tq, pp * LANES), lambda bi, p, qi: (bi, qi, p)),
        out_shape=jax.ShapeDtypeStruct((b, n, (nh // 2) * LANES), BF16),
        compiler_params=_params("arbitrary", "arbitrary", "arbitrary"),
        name="mla_attn",
    )(*args)


def _gla_kernel(has_s0, lookahead, *refs):
    fwd_in, bwd_in = refs[0:4], refs[4:8]
    refs = refs[8:]
    if lookahead:
        next_misc = refs[:2]
        refs = refs[2:]
    wg_ref, bg_ref, tf_ref, tb_ref = refs[:4]
    refs = refs[4:]
    if has_s0:
        s0f_ref, s0b_ref = refs[:2]
        refs = refs[2:]
    of_ref, ob_ref, sf_ref, sb_ref = refs[:4]
    tb_rows = of_ref.shape[1]
    nch = tb_rows // GLA_CHUNK
    n_bb = of_ref.shape[0]
    t_refs = (tf_ref, tb_ref)

    def log_decay(m_ref, d, bb):
        pre = _dot(m_ref[bb].astype(BF16), wg_ref[d]) + bg_ref[d]
        g = _log_sigmoid(pre) * (LOG2E / GLA_GATE_NORM)
        g_hi = g.astype(BF16)
        g_lo = (g - g_hi.astype(F32)).astype(BF16)
        t = t_refs[d][...]
        return _dot(t, g_hi) + _dot(t, g_lo)

    @pl.when(pl.program_id(1) == 0)
    def _():
        if has_s0:
            sf_ref[...] = s0f_ref[...]
            sb_ref[...] = s0b_ref[...]
        else:
            sf_ref[...] = jnp.zeros(sf_ref.shape, F32)
            sb_ref[...] = jnp.zeros(sb_ref.shape, F32)
        if lookahead:
            cum_scr = refs[4]
            for d, m_ref in enumerate((fwd_in[3], bwd_in[3])):
                for bb in range(n_bb):
                    cum_scr[d, bb] = log_decay(m_ref, d, bb)

    scans = []
    for d, ((q_ref, k_ref, v_ref, m_ref), o_ref, s_ref) in enumerate(
            ((fwd_in, of_ref, sf_ref), (bwd_in, ob_ref, sb_ref))):
        row = lax.broadcasted_iota(jnp.int32, (tb_rows, tb_rows), 0)
        col = lax.broadcasted_iota(jnp.int32, (tb_rows, tb_rows), 1)
        chunk_lo = (row // GLA_CHUNK) * GLA_CHUNK
        if d == 0:
            tmask = (col <= row) & (col >= chunk_lo)
        else:
            tmask = (col >= row) & (col < chunk_lo + GLA_CHUNK)
        edges = [(n + 1) * GLA_CHUNK - 1 if d == 0 else n * GLA_CHUNK for n in range(nch)]
        for bb in range(n_bb):
            cum = refs[4][d, bb] if lookahead else log_decay(m_ref, d, bb)
            decay_rows = jnp.exp2(jnp.concatenate([cum[e:e + 1, :] for e in edges], axis=0))
            decay_cols = jnp.concatenate(
                [decay_rows, jnp.zeros((LANES - nch, decay_rows.shape[1]), F32)], axis=0).T
            for hd in range(GLA_HEADS):
                sl = slice(hd * LANES, (hd + 1) * LANES)
                scans.append(dict(d=d, bb=bb, hd=hd, sl=sl, cs=cum[:, sl], tmask=tmask,
                                  decay_row=decay_rows[:, sl], decay_col=decay_cols[sl, :],
                                  q_ref=q_ref, k_ref=k_ref, v_ref=v_ref, o_ref=o_ref, s_ref=s_ref))

    for sc in scans:
        cs, bb, sl = sc["cs"], sc["bb"], sc["sl"]
        k = sc["k_ref"][bb, :, sl].astype(F32)
        v = sc["v_ref"][bb, :, sl]
        sc["q_in"] = (sc["q_ref"][bb, :, sl].astype(F32) * jnp.exp2(cs)).astype(BF16)
        k_neg = k * jnp.exp2(-cs)
        a = jnp.where(sc["tmask"], _dot(sc["q_in"], k_neg.T.astype(BF16)), 0.0)
        sc["o_intra"] = _dot(a.astype(BF16), v)
        sc["u"] = []
        for n in range(nch):
            r = slice(n * GLA_CHUNK, (n + 1) * GLA_CHUNK)
            k_dec = (k_neg[r] * sc["decay_row"][n:n + 1, :]).astype(BF16)
            sc["u"].append(_dot_tn(k_dec, v[r]))
        sc["s"] = sc["s_ref"][bb, sc["hd"]]

    for step in range(nch):
        for sc in scans:
            n = step if sc["d"] == 0 else nch - 1 - step
            r = slice(n * GLA_CHUNK, (n + 1) * GLA_CHUNK)
            sc["o_ref"][sc["bb"], r, sc["sl"]] = sc["o_intra"][r] + _dot(sc["q_in"][r], sc["s"].astype(BF16))
            sc["s"] = sc["s"] * sc["decay_col"][:, n:n + 1] + sc["u"][n]

    for sc in scans:
        sc["s_ref"][sc["bb"], sc["hd"]] = sc["s"]

    if lookahead:
        for d, m_ref in enumerate(next_misc):
            for bb in range(n_bb):
                refs[4][d, bb] = log_decay(m_ref, d, bb)


def _gla_masks(tb_rows):
    idx = np.arange(tb_rows)
    same = (idx[:, None] // GLA_CHUNK) == (idx[None, :] // GLA_CHUNK)
    fwd = same & (idx[None, :] <= idx[:, None])
    bwd = same & (idx[None, :] >= idx[:, None])
    return jnp.asarray(fwd, dtype=BF16), jnp.asarray(bwd, dtype=BF16)


def _gla(gq, gk, gv, misc, w, s0, tb_rows):
    b, n, width = gq.shape
    nb = n // tb_rows
    gb = GLA_BATCH
    fw = lambda cols: pl.BlockSpec((gb, tb_rows, cols), lambda bi, i: (bi, i, 0))
    bw = lambda cols: pl.BlockSpec((gb, tb_rows, cols), lambda bi, i: (bi, nb - 1 - i, 0))
    st = pl.BlockSpec((gb, GLA_HEADS, GLA_DK, GLA_DV), lambda bi, i: (bi, 0, 0, 0))
    tf, tb = _gla_masks(tb_rows)
    lookahead = nb > 1
    in_specs = [fw(width)] * 3 + [fw(LANES)] + [bw(width)] * 3 + [bw(LANES)]
    args = [gq, gk, gv, misc, gq, gk, gv, misc]
    if lookahead:
        in_specs += [pl.BlockSpec((gb, tb_rows, LANES), lambda bi, i: (bi, jnp.minimum(i + 1, nb - 1), 0)),
                     pl.BlockSpec((gb, tb_rows, LANES), lambda bi, i: (bi, jnp.maximum(nb - 2 - i, 0), 0))]
        args += [misc, misc]
    in_specs += [_full((2, LANES, width)), _full((2, 1, width)),
                 _full((tb_rows, tb_rows)), _full((tb_rows, tb_rows))]
    args += [w["wg"], w["bg"], tf, tb]
    if s0 is not None:
        in_specs += [st, st]
        args += list(s0)
    scratch = [pltpu.VMEM((2, gb, tb_rows, width), F32)] if lookahead else []
    return pl.pallas_call(
        functools.partial(_gla_kernel, s0 is not None, lookahead),
        grid=(b // gb, nb),
        in_specs=in_specs,
        out_specs=[fw(width), bw(width), st, st],
        out_shape=[jax.ShapeDtypeStruct((b, n, width), F32)] * 2
        + [jax.ShapeDtypeStruct((b, GLA_HEADS, GLA_DK, GLA_DV), F32)] * 2,
        scratch_shapes=scratch,
        compiler_params=_params("arbitrary", "arbitrary"),
        name="gla",
    )(*args)


def _layer_tail_kernel(has_gla, *refs):
    if has_gla:
        mix_ref, of_ref, ob_ref, gout_ref, ggo_ref = refs[:5]
        refs = refs[5:]
    else:
        mix_ref = refs[0]
        refs = refs[1:]
    wmix_ref, x_ref, mod_ref, g_ref, win_ref, wo_ref, o_ref = refs
    mod = mod_ref[0]
    if has_gla:
        o = of_ref[...] + ob_ref[...]
        parts = []
        for hd in range(GLA_HEADS):
            sl = slice(hd * LANES, (hd + 1) * LANES)
            gate = gout_ref[:, sl].astype(F32)
            parts.append((_rms(o[:, sl], ggo_ref[...]) * _silu(gate)).astype(BF16))
        n_mla = mix_ref.shape[1]
        out = (_dot(mix_ref[...], wmix_ref[0:n_mla, :])
               + _dot(jnp.concatenate(parts, axis=-1), wmix_ref[n_mla:, :]))
    else:
        out = _dot(mix_ref[...], wmix_ref[...])
    gate1 = mod[:, 2 * D_MODEL:3 * D_MODEL]
    x1 = x_ref[...] + gate1 * _rms(out, g_ref[1:2, :])
    shift, scale = mod[:, 3 * D_MODEL:4 * D_MODEL], mod[:, 4 * D_MODEL:5 * D_MODEL]
    hb = (_rms(x1, g_ref[2:3, :]) * (1.0 + scale) + shift).astype(BF16)

    th = FFN_HIDDEN // FFN_SPLIT
    f = None
    for s in range(FFN_SPLIT):
        gate = _dot(hb, win_ref[:, s * th:(s + 1) * th])
        up = _dot(hb, win_ref[:, FFN_HIDDEN + s * th:FFN_HIDDEN + (s + 1) * th])
        part = _dot((_silu(gate) * up).astype(BF16), wo_ref[s * th:(s + 1) * th, :])
        f = part if f is None else f + part

    gate2 = mod[:, 5 * D_MODEL:6 * D_MODEL]
    o_ref[...] = x1 + gate2 * _rms(f, g_ref[3:4, :])


def _layer_tail(mix, gla, w_mix, x, mod3, modrow, g, w_in, w_out, tm):
    t = x.shape[0]
    tok = lambda n: pl.BlockSpec((tm, n), lambda i: (i, 0))
    resident = lambda shape: pl.BlockSpec(shape, lambda i: (0,) * len(shape), pipeline_mode=pl.Buffered(1))
    in_specs = [tok(mix.shape[1])]
    args = [mix]
    if gla is not None:
        o_f, o_b, gout, g_gla = gla
        in_specs += [tok(o_f.shape[1])] * 3 + [resident((1, GLA_DV))]
        args += [o_f, o_b, gout, g_gla]
    in_specs += [resident(w_mix.shape), tok(D_MODEL),
                 pl.BlockSpec((1, 1, 6 * D_MODEL), lambda i: (modrow(i), 0, 0)), resident(g.shape),
                 resident(w_in.shape), resident(w_out.shape)]
    args += [w_mix, x, mod3, g, w_in, w_out]
    return pl.pallas_call(
        functools.partial(_layer_tail_kernel, gla is not None),
        grid=(t // tm,),
        in_specs=in_specs,
        out_specs=tok(D_MODEL),
        out_shape=jax.ShapeDtypeStruct((t, D_MODEL), F32),
        compiler_params=_params("arbitrary"),
        name="layer_tail",
    )(*args)


def _c_in_kernel(use_rope, want_f32, *refs):
    x_ref, mod_ref, g0_ref, w_ref = refs[:4]
    refs = refs[4:]
    if use_rope:
        c_ref, se_ref, so_ref = refs[:3]
        refs = refs[3:]
    q_out, k_out, v_out = refs[:3]
    mod = mod_ref[0]
    shift, scale = mod[:, 0:D_MODEL], mod[:, D_MODEL:2 * D_MODEL]
    h = _rms(x_ref[...], g0_ref[...]) * (1.0 + scale) + shift
    proj = _dot(h.astype(BF16), w_ref[...])
    width = DIFF_HEADS * LANES
    sm_scale = DIFF_HEAD_DIM ** -0.5 * LOG2E
    for hd in range(DIFF_HEADS):
        sl = slice(hd * LANES, (hd + 1) * LANES)
        qh = proj[:, hd * LANES:(hd + 1) * LANES]
        kh = proj[:, width + hd * LANES:width + (hd + 1) * LANES]
        vh = proj[:, 2 * width + hd * LANES:2 * width + (hd + 1) * LANES]
        if use_rope:
            qh = _rope(qh, c_ref[...], se_ref[...], so_ref[...])
            kh = _rope(kh, c_ref[...], se_ref[...], so_ref[...])
        q_out[0, hd] = (qh * sm_scale).astype(BF16)
        k_out[0, hd] = kh.astype(BF16)
        v_out[0, hd] = vh.astype(BF16)
        if want_f32:
            refs[3][:, sl] = kh
            refs[4][:, sl] = vh


def _c_in(x, mod3, modrow, g0, w_qkv, rope, want_f32, tm, seq_len):
    t = x.shape[0]
    width = DIFF_HEADS * LANES
    tok = lambda n: pl.BlockSpec((tm, n), lambda i: (i, 0))
    in_specs = [tok(D_MODEL), pl.BlockSpec((1, 1, 6 * D_MODEL), lambda i: (modrow(i), 0, 0)),
                _full((1, D_MODEL)), _full((D_MODEL, 3 * width))]
    args = [x, mod3, g0, w_qkv]
    per = seq_len // tm
    nb = t // seq_len
    if rope is not None:
        in_specs += [pl.BlockSpec((tm, LANES), lambda i: (i % per, 0))] * 3
        args += list(rope)
    heads = pl.BlockSpec((1, DIFF_HEADS, tm, LANES), lambda i: (i // per, 0, i % per, 0))
    head_shape = jax.ShapeDtypeStruct((nb, DIFF_HEADS, seq_len, LANES), BF16)
    n_f32 = 2 if want_f32 else 0
    return pl.pallas_call(
        functools.partial(_c_in_kernel, rope is not None, want_f32),
        grid=(t // tm,),
        in_specs=in_specs,
        out_specs=[heads] * 3 + [tok(width)] * n_f32,
        out_shape=[head_shape] * 3 + [jax.ShapeDtypeStruct((t, width), F32)] * n_f32,
        compiler_params=_params("arbitrary"),
        name="c_in",
    )(*args)


def _diff_attn_kernel(has_ctx, lam_init, *refs):
    if has_ctx:
        q_ref, kn, vn, kc, vc, lam_ref, g_ref, o_ref = refs
    else:
        q_ref, kn, vn, lam_ref, g_ref, o_ref = refs
    lp = lam_ref[...]
    lam = (jnp.exp(jnp.sum(lp[0:1] * lp[1:2], axis=-1, keepdims=True))
           - jnp.exp(jnp.sum(lp[2:3] * lp[3:4], axis=-1, keepdims=True)) + lam_init)
    for i in range(q_ref.shape[0]):
        k_new, k_ctx = kn.at[i], (kc.at[i] if has_ctx else None)
        sources = [(k_new, k_new, vn.at[i])]
        if has_ctx:
            sources.insert(0, (k_ctx, k_ctx, vc.at[i]))
        qh = q_ref[i]
        lane = lax.broadcasted_iota(jnp.int32, qh.shape, 1)
        zero = jnp.zeros_like(qh)
        q0 = jnp.where(lane < DIFF_HEAD_DIM, qh, zero)
        q1 = jnp.where(lane < DIFF_HEAD_DIM, zero, qh)
        (l0, acc0), (l1, acc1) = _two_map_flash(q0, q1, sources)
        o = acc0 * (1.0 / l0) - lam * (acc1 * (1.0 / l1))
        o_ref[:, i * LANES:(i + 1) * LANES] = (_rms(o, g_ref[...]) * (1.0 - lam_init)).astype(BF16)


def _diff_attn(q, k_new, v_new, ctx, lam_p, g_out, lam_init, tq, heads_per_step):
    b, nh, n, _ = q.shape
    hps = heads_per_step
    slab = lambda rows, f: pl.BlockSpec((None, hps, rows, LANES), f)
    whole = lambda bi, h, qi: (bi, h, 0, 0)
    in_specs = [slab(tq, lambda bi, h, qi: (bi, h, qi, 0)), slab(n, whole), slab(n, whole)]
    args = [q, k_new, v_new]
    if ctx is not None:
        in_specs += [slab(ctx[0].shape[2], whole)] * 2
        args += list(ctx)
    in_specs += [_full(lam_p.shape), _full(g_out.shape)]
    args += [lam_p, g_out]
    return pl.pallas_call(
        functools.partial(_diff_attn_kernel, ctx is not None, lam_init),
        grid=(b, nh // hps, n // tq),
        in_specs=in_specs,
        out_specs=pl.BlockSpec((None, tq, hps * LANES), lambda bi, h, qi: (bi, qi, h)),
        out_shape=jax.ShapeDtypeStruct((b, n, nh * LANES), BF16),
        compiler_params=_params("arbitrary", "arbitrary", "arbitrary"),
        name="diff_attn",
    )(*args)


def _prep_ab_weights(ai, ab_w_in, mla_g_q, mla_g_kv, mla_w_uq, mla_w_ukv, gla_w_gate_up, gla_b_gate):
    w = ab_w_in[ai]
    d = w.shape[0]
    o = np.cumsum([0, MLA_Q_LORA, MLA_KV_LORA, MLA_ROPE, GLA_HEADS * GLA_DK, GLA_HEADS * GLA_DK,
                   GLA_HEADS * GLA_DV, 2 * GLA_GATE_RANK, GLA_HEADS * GLA_DV])
    q_lat, kv_lat, k_rope, gq, gk, gv, ggate, gout = (w[:, o[i]:o[i + 1]] for i in range(8))
    pad = jnp.zeros((d, LANES - MLA_ROPE - 2 * GLA_GATE_RANK), w.dtype)
    win = jnp.concatenate([q_lat, k_rope, ggate, pad, kv_lat, gq, gk, gv, gout], axis=1).astype(BF16)

    head_pad = LANES - MLA_NOPE - MLA_ROPE
    wq = mla_w_uq[ai].reshape(MLA_Q_LORA, MLA_HEADS, MLA_NOPE + MLA_ROPE)
    wq = jnp.pad(wq, ((0, 0), (0, 0), (0, head_pad))).reshape(MLA_Q_LORA, MLA_HEADS * LANES).astype(BF16)
    wkv = mla_w_ukv[ai].reshape(MLA_KV_LORA, MLA_HEADS, MLA_NOPE + MLA_V)
    wk = jnp.pad(wkv[:, :, :MLA_NOPE], ((0, 0), (0, 0), (0, LANES - MLA_NOPE)))
    wk = wk.reshape(MLA_KV_LORA, MLA_HEADS * LANES).astype(BF16)
    wv = wkv[:, :, MLA_NOPE:].reshape(MLA_KV_LORA, MLA_HEADS * MLA_V).astype(BF16)

    e = np.zeros((MLA_ROPE, MLA_HEADS * LANES), np.float32)
    for hd in range(MLA_HEADS):
        e[np.arange(MLA_ROPE), hd * LANES + MLA_NOPE + np.arange(MLA_ROPE)] = 1.0

    wg = jnp.zeros((2, LANES, GLA_HEADS * GLA_DK), F32)
    for dr in range(2):
        lo = _MISC_GATE + dr * GLA_GATE_RANK
        wg = wg.at[dr, lo:lo + GLA_GATE_RANK, :].set(gla_w_gate_up[ai, dr])
    return dict(win=win, g_q=mla_g_q[ai][None, :], g_kv=mla_g_kv[ai][None, :], wq=wq, wk=wk, wv=wv,
                e=jnp.asarray(e, dtype=BF16), wg=wg.astype(BF16),
                bg=gla_b_gate[ai].reshape(2, 1, GLA_HEADS * GLA_DK))


def _diff_lambda_init(layer):
    return 0.8 - 0.6 * math.exp(-0.3 * layer)


def kernel(x_prompt, x_sample, cache_mla_ckv, cache_mla_krope, state_gla_fwd, state_gla_bwd,
           cache_diff_k, cache_diff_v, c, c_ctx, w_mod, b_mod, g_norm, ab_w_in, mla_g_q, mla_g_kv,
           mla_w_uq, mla_w_ukv, gla_w_gate_up, gla_b_gate, gla_g_out, ab_w_out, c_w_qkv, diff_lambda,
           diff_g_out, c_w_out, w_ffn_in, w_ffn_out):
    depth = w_mod.shape[0]
    bp, lp, d = x_prompt.shape
    bs, ls, _ = x_sample.shape
    n_ctx = cache_mla_ckv.shape[2]

    n_cond = 1 + bs
    rows = -(-n_cond // 8) * 8
    conds = jnp.concatenate([c_ctx[None, :], c, jnp.zeros((rows - n_cond, d), F32)], axis=0)
    mod3 = _modulation(conds, w_mod, b_mod).reshape(depth * rows, 1, 6 * d)

    per_s = ls // min(TOKEN_TILE, ls)
    rope_q = _rope_tables(ls, MLA_ROPE, MLA_NOPE, LANES)
    rope_misc = _rope_tables(ls, MLA_ROPE, 0, LANES)
    rope_diff = _rope_tables(ls, DIFF_HEAD_DIM, 0, DIFF_HEAD_DIM)

    groups = {
        "prompt": dict(x=x_prompt.reshape(bp * lp, d), b=bp, n=lp, rope=False,
                       modrow=lambda li: (lambda i: li * rows)),
        "sample": dict(x=x_sample.reshape(bs * ls, d), b=bs, n=ls, rope=True,
                       modrow=lambda li: (lambda i: li * rows + 1 + i // per_s)),
    }
    ab_states, c_states = [], []

    ab_w = [_prep_ab_weights(ai, ab_w_in, mla_g_q, mla_g_kv, mla_w_uq, mla_w_ukv, gla_w_gate_up, gla_b_gate)
            for ai in range(ab_w_in.shape[0])]
    ab_w_out_b = ab_w_out.astype(BF16)
    c_w_qkv_b = c_w_qkv.astype(BF16)
    c_w_out_b = c_w_out.astype(BF16)
    w_ffn_in_b = w_ffn_in.astype(BF16)
    w_ffn_out_b = w_ffn_out.astype(BF16)

    for name, grp in groups.items():
        x, b, n = grp["x"], grp["b"], grp["n"]
        is_sample = grp["rope"]
        tm = min(TOKEN_TILE, n)
        for li in range(depth):
            modrow = grp["modrow"](li)
            g = g_norm[li]
            if li % 2 == 0:
                ai = li // 2
                w = ab_w[ai]
                rope = rope_q + rope_misc if is_sample else None
                q, kcat, v, ckv, misc, gq, gk, gv, gout = _ab_in(x, mod3, modrow, g[0:1], w, rope, tm, n)
                sh = lambda a: a.reshape(b, n, a.shape[-1])
                if is_sample:
                    ctx = _ctx_kv(cache_mla_ckv[:, ai], cache_mla_krope[:, ai], w)
                    s0 = (state_gla_fwd[:, ai], state_gla_bwd[:, ai])
                else:
                    ctx, s0 = None, None
                mla_o = _mla_attn(q, kcat, v, ctx, min(ATTN_Q_TILE, n),
                                  max(1, min(MLA_HEADS // 2, ATTN_STEP_ROWS // n)))
                o_f, o_b, s_f, s_b = _gla(sh(gq), sh(gk), sh(gv), sh(misc), w, s0, min(GLA_BLOCK, n))
                if not is_sample:
                    ab_states.append((ckv.reshape(b, n, MLA_KV_LORA), misc[:, :MLA_ROPE].reshape(b, n, MLA_ROPE),
                                      s_f, s_b))
                mix = mla_o.reshape(b * n, -1)
                gla = (o_f.reshape(b * n, -1), o_b.reshape(b * n, -1), gout, gla_g_out[ai][None, :])
                w_mix = ab_w_out_b[ai]
            else:
                ci = li // 2
                rope = rope_diff if is_sample else None
                outs = _c_in(x, mod3, modrow, g[0:1], c_w_qkv_b[ci], rope, not is_sample, tm, n)
                q, k, v = outs[:3]
                if is_sample:
                    head_major = lambda a: jnp.swapaxes(a.reshape(bs, n_ctx, DIFF_HEADS, LANES), 1, 2).astype(BF16)
                    ctx = (head_major(cache_diff_k[:, ci]), head_major(cache_diff_v[:, ci]))
                else:
                    ctx = None
                    c_states.append((outs[3].reshape(b, n, DIFF_HEADS, 2, DIFF_HEAD_DIM),
                                     outs[4].reshape(b, n, DIFF_HEADS, 2 * DIFF_HEAD_DIM)))
                o = _diff_attn(q, k, v, ctx, diff_lambda[ci], diff_g_out[ci][None, :],
                               _diff_lambda_init(li), min(ATTN_Q_TILE, n),
                               max(1, min(DIFF_HEADS, ATTN_STEP_ROWS // n)))
                mix, gla, w_mix = o.reshape(b * n, -1), None, c_w_out_b[ci]
            tm_tail = tm if is_sample else min(TOKEN_TILE, b * n)
            x = _layer_tail(mix, gla, w_mix, x, mod3, modrow, g, w_ffn_in_b[li], w_ffn_out_b[li], tm_tail)
        grp["y"] = x.reshape(b, n, d)

    stack = lambda states, i: jnp.stack([s[i] for s in states], axis=1)
    return (groups["prompt"]["y"], groups["sample"]["y"],
            stack(ab_states, 0), stack(ab_states, 1), stack(ab_states, 2), stack(ab_states, 3),
            stack(c_states, 0), stack(c_states, 1))
```

```python
import functools
import math

import numpy as np
import jax
import jax.numpy as jnp
from jax import lax
from jax.experimental import pallas as pl
from jax.experimental.pallas import tpu as pltpu

F32 = jnp.float32
BF16 = jnp.bfloat16

D_MODEL = 1024
GRID_W = 64
ROPE_BASE = 10000.0
NORM_EPS = 1e-6

MLA_HEADS = 8
MLA_NOPE = 64
MLA_ROPE = 32
MLA_V = 64
MLA_Q_LORA = 384
MLA_KV_LORA = 256
GLA_HEADS = 4
GLA_DK = 128
GLA_DV = 128
GLA_GATE_RANK = 16
GLA_GATE_NORM = 16.0
GLA_CHUNK = 64
DIFF_HEADS = 8
DIFF_HEAD_DIM = 64
FFN_HIDDEN = 2816

LANES = 128
VMEM_LIMIT = 56 * 1024 * 1024

_OFF_QLAT = 0
_OFF_MISC = _OFF_QLAT + MLA_Q_LORA
_OFF_KVLAT = _OFF_MISC + LANES
_OFF_GQ = _OFF_KVLAT + MLA_KV_LORA
_OFF_GK = _OFF_GQ + GLA_HEADS * GLA_DK
_OFF_GV = _OFF_GK + GLA_HEADS * GLA_DK
_OFF_GOUT = _OFF_GV + GLA_HEADS * GLA_DV
_AB_COLS = _OFF_GOUT + GLA_HEADS * GLA_DV
_MISC_GATE = MLA_ROPE

LOG2E = math.log2(math.e)

TOKEN_TILE = 512
ATTN_Q_TILE = 1024
ATTN_K_TILE = 256
ATTN_STEP_ROWS = 8192
GLA_BLOCK = 256
GLA_BATCH = 4
FFN_SPLIT = 11


def _rms(x, g):
    var = jnp.mean(x * x, axis=-1, keepdims=True)
    return x * lax.rsqrt(var + NORM_EPS) * g


def _silu(x):
    return x * (1.0 / (1.0 + jnp.exp(-x)))


def _log_sigmoid(x):
    return jnp.minimum(x, 0.0) - jnp.log(1.0 + jnp.exp(-jnp.abs(x)))


def _rope(x, c, se, so):
    n = x.shape[-1]
    return x * c + pltpu.roll(x, n - 1, 1) * se + pltpu.roll(x, 1, 1) * so


def _dot(a, b):
    return jnp.dot(a, b, preferred_element_type=F32)


def _dot_nt(a, b):
    return lax.dot_general(a, b, (((1,), (1,)), ((), ())), preferred_element_type=F32)


def _dot_tn(a, b):
    return lax.dot_general(a, b, (((0,), (0,)), ((), ())), preferred_element_type=F32)


def _params(*sem):
    return pltpu.CompilerParams(dimension_semantics=sem, vmem_limit_bytes=VMEM_LIMIT)


def _full(shape):
    zeros = (0,) * len(shape)
    return pl.BlockSpec(shape, lambda *_: zeros, pipeline_mode=pl.Buffered(1))


def _rope_tables(n_tokens, rot_dim, lane_lo, period):
    rows = n_tokens // GRID_W
    row = np.repeat(np.arange(rows, dtype=np.float64), GRID_W)
    col = np.tile(np.arange(GRID_W, dtype=np.float64), rows)
    n_freq = rot_dim // 4
    inv = ROPE_BASE ** (-np.arange(n_freq, dtype=np.float64) / n_freq)
    ang = np.concatenate([row[:, None] * inv, col[:, None] * inv], axis=-1)
    cos, sin = np.cos(ang), np.sin(ang)
    c = np.ones((n_tokens, period))
    se = np.zeros((n_tokens, period))
    so = np.zeros((n_tokens, period))
    c[:, lane_lo:lane_lo + rot_dim] = np.repeat(cos, 2, axis=-1)
    se[:, lane_lo:lane_lo + rot_dim:2] = -sin
    so[:, lane_lo + 1:lane_lo + rot_dim:2] = sin
    reps = LANES // period
    return tuple(jnp.asarray(np.tile(t, (1, reps)), dtype=F32) for t in (c, se, so))


def _mod_kernel(c_ref, w_ref, b_ref, o_ref):
    a = _silu(c_ref[...]).astype(BF16)
    o_ref[0] = _dot(a, w_ref[0].astype(BF16)) + b_ref[0]


def _modulation(conds, w_mod, b_mod):
    depth, d, n = w_mod.shape
    rows = conds.shape[0]
    tn = 1536
    return pl.pallas_call(
        _mod_kernel,
        grid=(depth, n // tn),
        in_specs=[_full((rows, d)),
                  pl.BlockSpec((1, d, tn), lambda l, j: (l, 0, j)),
                  pl.BlockSpec((1, 1, tn), lambda l, j: (l, 0, j))],
        out_specs=pl.BlockSpec((1, rows, tn), lambda l, j: (l, 0, j)),
        out_shape=jax.ShapeDtypeStruct((depth, rows, n), F32),
        compiler_params=_params("arbitrary", "arbitrary"),
        name="modulation",
    )(conds, w_mod, b_mod.reshape(depth, 1, n))


def _ab_in_kernel(use_rope, *refs):
    (x_ref, mod_ref, g0_ref, win_ref, gq_ref, gkv_ref, wq_ref, wk_ref, wv_ref) = refs[:9]
    refs = refs[9:]
    if use_rope:
        cq, seq, soq, cm, sem, som = refs[:6]
        refs = refs[6:]
    (q_out, kcat_out, v_out, ckv_out, misc_out, gq_out, gk_out, gv_out, gout_out) = refs

    mod = mod_ref[0]
    shift, scale = mod[:, 0:D_MODEL], mod[:, D_MODEL:2 * D_MODEL]
    hb = (_rms(x_ref[...], g0_ref[...]) * (1.0 + scale) + shift).astype(BF16)
    project = lambda lo, hi: _dot(hb, win_ref[:, lo:hi])

    q_misc = project(_OFF_QLAT, _OFF_KVLAT)
    misc = q_misc[:, _OFF_MISC:]
    if use_rope:
        misc = _rope(misc, cm[...], sem[...], som[...])
    misc_out[...] = misc

    ckv = _rms(project(_OFF_KVLAT, _OFF_GQ), gkv_ref[...])
    ckv_out[...] = ckv
    ckv_b = ckv.astype(BF16)
    key_rope = pltpu.roll(misc, MLA_NOPE, 1)
    lane = lax.broadcasted_iota(jnp.int32, misc.shape, 1)
    rope_lanes = (lane >= MLA_NOPE) & (lane < MLA_NOPE + MLA_ROPE)
    pair = 2 * LANES
    for hp in range(MLA_HEADS // 2):
        k_nope = _dot(ckv_b, wk_ref[:, hp * pair:(hp + 1) * pair])
        for e in range(2):
            kh = jnp.where(rope_lanes, key_rope, k_nope[:, e * LANES:(e + 1) * LANES])
            kcat_out[0, 2 * hp + e] = kh.astype(BF16)
    v = _dot(ckv_b, wv_ref[...]).astype(BF16)
    for hp in range(MLA_HEADS // 2):
        v_out[0, hp] = v[:, hp * LANES:(hp + 1) * LANES]

    qn = _rms(q_misc[:, :_OFF_MISC], gq_ref[...]).astype(BF16)
    sm_scale = (MLA_NOPE + MLA_ROPE) ** -0.5 * LOG2E
    for hp in range(MLA_HEADS // 2):
        q2 = _dot(qn, wq_ref[:, hp * pair:(hp + 1) * pair])
        for e in range(2):
            qh = q2[:, e * LANES:(e + 1) * LANES]
            if use_rope:
                qh = _rope(qh, cq[...], seq[...], soq[...])
            q_out[0, 2 * hp + e] = (qh * sm_scale).astype(BF16)

    gq_out[...] = (project(_OFF_GQ, _OFF_GK) * (GLA_DK ** -0.5)).astype(BF16)
    gk_out[...] = project(_OFF_GK, _OFF_GV).astype(BF16)
    gv_out[...] = project(_OFF_GV, _OFF_GOUT).astype(BF16)
    gout_out[...] = project(_OFF_GOUT, _AB_COLS).astype(BF16)


def _ab_in(x, mod3, modrow, g0, w, rope, tm, seq_len):
    t = x.shape[0]
    nt = t // tm
    tok = lambda n: pl.BlockSpec((tm, n), lambda i: (i, 0))
    in_specs = [tok(D_MODEL),
                pl.BlockSpec((1, 1, 6 * D_MODEL), lambda i: (modrow(i), 0, 0)),
                _full((1, D_MODEL)), _full((D_MODEL, _AB_COLS)),
                _full((1, MLA_Q_LORA)), _full((1, MLA_KV_LORA)),
                _full((MLA_Q_LORA, MLA_HEADS * LANES)), _full((MLA_KV_LORA, MLA_HEADS * LANES)),
                _full((MLA_KV_LORA, MLA_HEADS * MLA_V))]
    args = [x, mod3, g0, w["win"], w["g_q"], w["g_kv"], w["wq"], w["wk"], w["wv"]]
    per = seq_len // tm
    if rope is not None:
        in_specs += [pl.BlockSpec((tm, LANES), lambda i: (i % per, 0))] * 6
        args += list(rope)
    nb = t // seq_len
    heads = lambda nh: pl.BlockSpec((1, nh, tm, LANES), lambda i: (i // per, 0, i % per, 0))
    head_shape = lambda nh: jax.ShapeDtypeStruct((nb, nh, seq_len, LANES), BF16)
    widths = [(MLA_KV_LORA, F32), (LANES, F32)] + [(GLA_HEADS * GLA_DK, BF16)] * 4
    return pl.pallas_call(
        functools.partial(_ab_in_kernel, rope is not None),
        grid=(nt,),
        in_specs=in_specs,
        out_specs=[heads(MLA_HEADS), heads(MLA_HEADS), heads(MLA_HEADS // 2)] + [tok(n) for n, _ in widths],
        out_shape=[head_shape(MLA_HEADS), head_shape(MLA_HEADS), head_shape(MLA_HEADS // 2)]
        + [jax.ShapeDtypeStruct((t, n), dt) for n, dt in widths],
        compiler_params=_params("arbitrary"),
        name="ab_in",
    )(*args)


def _ctx_kv_kernel(ckv_ref, kr_ref, wk_ref, wv_ref, e_ref, kcat_out, v_out):
    c = ckv_ref[...].astype(BF16)
    kr = kr_ref[...].astype(BF16)
    kcat = (_dot(c, wk_ref[...]) + _dot(kr, e_ref[...])).astype(BF16)
    v = _dot(c, wv_ref[...]).astype(BF16)
    for hd in range(MLA_HEADS):
        kcat_out[0, hd] = kcat[:, hd * LANES:(hd + 1) * LANES]
    for hp in range(MLA_HEADS // 2):
        v_out[0, hp] = v[:, hp * LANES:(hp + 1) * LANES]


def _ctx_kv(ckv, krope, w):
    b, n_ctx, _ = ckv.shape
    tok = lambda n: pl.BlockSpec((None, n_ctx, n), lambda i: (i, 0, 0))
    heads = lambda nh: pl.BlockSpec((1, nh, n_ctx, LANES), lambda i: (i, 0, 0, 0))
    return pl.pallas_call(
        _ctx_kv_kernel,
        grid=(b,),
        in_specs=[tok(MLA_KV_LORA), tok(MLA_ROPE),
                  _full((MLA_KV_LORA, MLA_HEADS * LANES)), _full((MLA_KV_LORA, MLA_HEADS * MLA_V)),
                  _full((MLA_ROPE, MLA_HEADS * LANES))],
        out_specs=[heads(MLA_HEADS), heads(MLA_HEADS // 2)],
        out_shape=[jax.ShapeDtypeStruct((b, MLA_HEADS, n_ctx, LANES), BF16),
                   jax.ShapeDtypeStruct((b, MLA_HEADS // 2, n_ctx, LANES), BF16)],
        compiler_params=_params("arbitrary"),
        name="mla_ctx_kv",
    )(ckv, krope, w["wk"], w["wv"], w["e"])


def _two_map_flash(q0, q1, sources):
    state = [None, None]
    for k0_ref, k1_ref, v_ref in sources:
        rows = v_ref.shape[0]
        tk = min(ATTN_K_TILE, rows)
        ones = jnp.ones((tk, LANES), BF16)
        for c in range(rows // tk):
            r = slice(c * tk, (c + 1) * tk)
            v = jnp.concatenate([v_ref[r, :], ones], axis=1)
            for j, (q, k_ref) in enumerate(((q0, k0_ref), (q1, k1_ref))):
                s = _dot_nt(q, k_ref[r, :])
                m_blk = jnp.max(s, axis=-1, keepdims=True)
                if state[j] is None:
                    p = jnp.exp2(s - m_blk)
                    state[j] = (m_blk, _dot(p.astype(BF16), v))
                else:
                    m, acc = state[j]
                    m_new = jnp.maximum(m, m_blk)
                    p = jnp.exp2(s - m_new)
                    state[j] = (m_new, jnp.exp2(m - m_new) * acc + _dot(p.astype(BF16), v))
    return tuple((acc[:, LANES:], acc[:, :LANES]) for _, acc in state)


def _mla_attn_kernel(has_ctx, *refs):
    if has_ctx:
        q_ref, kn, vn, kc, vc, o_ref = refs
    else:
        q_ref, kn, vn, o_ref = refs
    for i in range(vn.shape[0]):
        e, o = 2 * i, 2 * i + 1
        sources = [(kn.at[e], kn.at[o], vn.at[i])]
        if has_ctx:
            sources.insert(0, (kc.at[e], kc.at[o], vc.at[i]))
        (l0, acc0), (l1, acc1) = _two_map_flash(q_ref[e], q_ref[o], sources)
        lane = lax.broadcasted_iota(jnp.int32, acc0.shape, 1)
        o_ref[:, i * LANES:(i + 1) * LANES] = jnp.where(
            lane < MLA_V, acc0 * (1.0 / l0), acc1 * (1.0 / l1)).astype(BF16)


def _mla_attn(q, k_new, v_new, ctx, tq, pairs_per_step):
    b, nh, n, _ = q.shape
    pp = pairs_per_step
    slab = lambda heads, rows, f: pl.BlockSpec((None, heads, rows, LANES), f)
    whole = lambda bi, p, qi: (bi, p, 0, 0)
    in_specs = [slab(2 * pp, tq, lambda bi, p, qi: (bi, p, qi, 0)), slab(2 * pp, n, whole), slab(pp, n, whole)]
    args = [q, k_new, v_new]
    if ctx is not None:
        nc = ctx[0].shape[2]
        in_specs += [slab(2 * pp, nc, whole), slab(pp, nc, whole)]
        args += list(ctx)
    return pl.pallas_call(
        functools.partial(_mla_attn_kernel, ctx is not None),
        grid=(b, nh // (2 * pp), n // tq),
        in_specs=in_specs,
        out_specs=pl.BlockSpec((None, tq, pp * LANES), lambda bi, p, qi: (bi, qi, p)),
        out_shape=jax.ShapeDtypeStruct((b, n, (nh // 2) * LANES), BF16),
        compiler_params=_params("arbitrary", "arbitrary", "arbitrary"),
        name="mla_attn",
    )(*args)


def _gla_kernel(has_s0, lookahead, *refs):
    fwd_in, bwd_in = refs[0:4], refs[4:8]
    refs = refs[8:]
    if lookahead:
        next_misc = refs[:2]
        refs = refs[2:]
    wg_ref, bg_ref, tf_ref, tb_ref = refs[:4]
    refs = refs[4:]
    if has_s0:
        s0f_ref, s0b_ref = refs[:2]
        refs = refs[2:]
    of_ref, ob_ref, sf_ref, sb_ref = refs[:4]
    tb_rows = of_ref.shape[1]
    nch = tb_rows // GLA_CHUNK
    n_bb = of_ref.shape[0]
    t_refs = (tf_ref, tb_ref)

    def log_decay(m_ref, d, bb):
        pre = _dot(m_ref[bb].astype(BF16), wg_ref[d]) + bg_ref[d]
        g = _log_sigmoid(pre) * (LOG2E / GLA_GATE_NORM)
        g_hi = g.astype(BF16)
        g_lo = (g - g_hi.astype(F32)).astype(BF16)
        t = t_refs[d][...]
        return _dot(t, g_hi) + _dot(t, g_lo)

    @pl.when(pl.program_id(1) == 0)
    def _():
        if has_s0:
            sf_ref[...] = s0f_ref[...]
            sb_ref[...] = s0b_ref[...]
        else:
            sf_ref[...] = jnp.zeros(sf_ref.shape, F32)
            sb_ref[...] = jnp.zeros(sb_ref.shape, F32)
        if lookahead:
            cum_scr = refs[4]
            for d, m_ref in enumerate((fwd_in[3], bwd_in[3])):
                for bb in range(n_bb):
                    cum_scr[d, bb] = log_decay(m_ref, d, bb)

    scans = []
    for d, ((q_ref, k_ref, v_ref, m_ref), o_ref, s_ref) in enumerate(
            ((fwd_in, of_ref, sf_ref), (bwd_in, ob_ref, sb_ref))):
        row = lax.broadcasted_iota(jnp.int32, (tb_rows, tb_rows), 0)
        col = lax.broadcasted_iota(jnp.int32, (tb_rows, tb_rows), 1)
        chunk_lo = (row // GLA_CHUNK) * GLA_CHUNK
        if d == 0:
            tmask = (col <= row) & (col >= chunk_lo)
        else:
            tmask = (col >= row) & (col < chunk_lo + GLA_CHUNK)
        edges = [(n + 1) * GLA_CHUNK - 1 if d == 0 else n * GLA_CHUNK for n in range(nch)]
        for bb in range(n_bb):
            cum = refs[4][d, bb] if lookahead else log_decay(m_ref, d, bb)
            decay_rows = jnp.exp2(jnp.concatenate([cum[e:e + 1, :] for e in edges], axis=0))
            decay_cols = jnp.concatenate(
                [decay_rows, jnp.zeros((LANES - nch, decay_rows.shape[1]), F32)], axis=0).T
            for hd in range(GLA_HEADS):
                sl = slice(hd * LANES, (hd + 1) * LANES)
                scans.append(dict(d=d, bb=bb, hd=hd, sl=sl, cs=cum[:, sl], tmask=tmask,
                                  decay_row=decay_rows[:, sl], decay_col=decay_cols[sl, :],
                                  q_ref=q_ref, k_ref=k_ref, v_ref=v_ref, o_ref=o_ref, s_ref=s_ref))

    for sc in scans:
        cs, bb, sl = sc["cs"], sc["bb"], sc["sl"]
        k = sc["k_ref"][bb, :, sl].astype(F32)
        v = sc["v_ref"][bb, :, sl]
        sc["q_in"] = (sc["q_ref"][bb, :, sl].astype(F32) * jnp.exp2(cs)).astype(BF16)
        k_neg = k * jnp.exp2(-cs)
        a = jnp.where(sc["tmask"], _dot(sc["q_in"], k_neg.T.astype(BF16)), 0.0)
        sc["o_intra"] = _dot(a.astype(BF16), v)
        sc["u"] = []
        for n in range(nch):
            r = slice(n * GLA_CHUNK, (n + 1) * GLA_CHUNK)
            k_dec = (k_neg[r] * sc["decay_row"][n:n + 1, :]).astype(BF16)
            sc["u"].append(_dot_tn(k_dec, v[r]))
        sc["s"] = sc["s_ref"][bb, sc["hd"]]

    for step in range(nch):
        for sc in scans:
            n = step if sc["d"] == 0 else nch - 1 - step
            r = slice(n * GLA_CHUNK, (n + 1) * GLA_CHUNK)
            sc["o_ref"][sc["bb"], r, sc["sl"]] = sc["o_intra"][r] + _dot(sc["q_in"][r], sc["s"].astype(BF16))
            sc["s"] = sc["s"] * sc["decay_col"][:, n:n + 1] + sc["u"][n]

    for sc in scans:
        sc["s_ref"][sc["bb"], sc["hd"]] = sc["s"]

    if lookahead:
        for d, m_ref in enumerate(next_misc):
            for bb in range(n_bb):
                refs[4][d, bb] = log_decay(m_ref, d, bb)


def _gla_masks(tb_rows):
    idx = np.arange(tb_rows)
    same = (idx[:, None] // GLA_CHUNK) == (idx[None, :] // GLA_CHUNK)
    fwd = same & (idx[None, :] <= idx[:, None])
    bwd = same & (idx[None, :] >= idx[:, None])
    return jnp.asarray(fwd, dtype=BF16), jnp.asarray(bwd, dtype=BF16)


def _gla(gq, gk, gv, misc, w, s0, tb_rows):
    b, n, width = gq.shape
    nb = n // tb_rows
    gb = GLA_BATCH
    fw = lambda cols: pl.BlockSpec((gb, tb_rows, cols), lambda bi, i: (bi, i, 0))
    bw = lambda cols: pl.BlockSpec((gb, tb_rows, cols), lambda bi, i: (bi, nb - 1 - i, 0))
    st = pl.BlockSpec((gb, GLA_HEADS, GLA_DK, GLA_DV), lambda bi, i: (bi, 0, 0, 0))
    tf, tb = _gla_masks(tb_rows)
    lookahead = nb > 1
    in_specs = [fw(width)] * 3 + [fw(LANES)] + [bw(width)] * 3 + [bw(LANES)]
    args = [gq, gk, gv, misc, gq, gk, gv, misc]
    if lookahead:
        in_specs += [pl.BlockSpec((gb, tb_rows, LANES), lambda bi, i: (bi, jnp.minimum(i + 1, nb - 1), 0)),
                     pl.BlockSpec((gb, tb_rows, LANES), lambda bi, i: (bi, jnp.maximum(nb - 2 - i, 0), 0))]
        args += [misc, misc]
    in_specs += [_full((2, LANES, width)), _full((2, 1, width)),
                 _full((tb_rows, tb_rows)), _full((tb_rows, tb_rows))]
    args += [w["wg"], w["bg"], tf, tb]
    if s0 is not None:
        in_specs += [st, st]
        args += list(s0)
    scratch = [pltpu.VMEM((2, gb, tb_rows, width), F32)] if lookahead else []
    return pl.pallas_call(
        functools.partial(_gla_kernel, s0 is not None, lookahead),
        grid=(b // gb, nb),
        in_specs=in_specs,
        out_specs=[fw(width), bw(width), st, st],
        out_shape=[jax.ShapeDtypeStruct((b, n, width), F32)] * 2
        + [jax.ShapeDtypeStruct((b, GLA_HEADS, GLA_DK, GLA_DV), F32)] * 2,
        scratch_shapes=scratch,
        compiler_params=_params("arbitrary", "arbitrary"),
        name="gla",
    )(*args)


def _layer_tail_kernel(has_gla, *refs):
    if has_gla:
        mix_ref, of_ref, ob_ref, gout_ref, ggo_ref = refs[:5]
        refs = refs[5:]
    else:
        mix_ref = refs[0]
        refs = refs[1:]
    wmix_ref, x_ref, mod_ref, g_ref, win_ref, wo_ref, o_ref = refs
    mod = mod_ref[0]
    if has_gla:
        o = of_ref[...] + ob_ref[...]
        parts = []
        for hd in range(GLA_HEADS):
            sl = slice(hd * LANES, (hd + 1) * LANES)
            gate = gout_ref[:, sl].astype(F32)
            parts.append((_rms(o[:, sl], ggo_ref[...]) * _silu(gate)).astype(BF16))
        n_mla = mix_ref.shape[1]
        out = (_dot(mix_ref[...], wmix_ref[0:n_mla, :])
               + _dot(jnp.concatenate(parts, axis=-1), wmix_ref[n_mla:, :]))
    else:
        out = _dot(mix_ref[...], wmix_ref[...])
    gate1 = mod[:, 2 * D_MODEL:3 * D_MODEL]
    x1 = x_ref[...] + gate1 * _rms(out, g_ref[1:2, :])
    shift, scale = mod[:, 3 * D_MODEL:4 * D_MODEL], mod[:, 4 * D_MODEL:5 * D_MODEL]
    hb = (_rms(x1, g_ref[2:3, :]) * (1.0 + scale) + shift).astype(BF16)

    th = FFN_HIDDEN // FFN_SPLIT
    f = None
    for s in range(FFN_SPLIT):
        gate = _dot(hb, win_ref[:, s * th:(s + 1) * th])
        up = _dot(hb, win_ref[:, FFN_HIDDEN + s * th:FFN_HIDDEN + (s + 1) * th])
        part = _dot((_silu(gate) * up).astype(BF16), wo_ref[s * th:(s + 1) * th, :])
        f = part if f is None else f + part

    gate2 = mod[:, 5 * D_MODEL:6 * D_MODEL]
    o_ref[...] = x1 + gate2 * _rms(f, g_ref[3:4, :])


def _layer_tail(mix, gla, w_mix, x, mod3, modrow, g, w_in, w_out, tm):
    t = x.shape[0]
    tok = lambda n: pl.BlockSpec((tm, n), lambda i: (i, 0))
    resident = lambda shape: pl.BlockSpec(shape, lambda i: (0,) * len(shape), pipeline_mode=pl.Buffered(1))
    in_specs = [tok(mix.shape[1])]
    args = [mix]
    if gla is not None:
        o_f, o_b, gout, g_gla = gla
        in_specs += [tok(o_f.shape[1])] * 3 + [resident((1, GLA_DV))]
        args += [o_f, o_b, gout, g_gla]
    in_specs += [resident(w_mix.shape), tok(D_MODEL),
                 pl.BlockSpec((1, 1, 6 * D_MODEL), lambda i: (modrow(i), 0, 0)), resident(g.shape),
                 resident(w_in.shape), resident(w_out.shape)]
    args += [w_mix, x, mod3, g, w_in, w_out]
    return pl.pallas_call(
        functools.partial(_layer_tail_kernel, gla is not None),
        grid=(t // tm,),
        in_specs=in_specs,
        out_specs=tok(D_MODEL),
        out_shape=jax.ShapeDtypeStruct((t, D_MODEL), F32),
        compiler_params=_params("arbitrary"),
        name="layer_tail",
    )(*args)


def _c_in_kernel(use_rope, want_f32, *refs):
    x_ref, mod_ref, g0_ref, w_ref = refs[:4]
    refs = refs[4:]
    if use_rope:
        c_ref, se_ref, so_ref = refs[:3]
        refs = refs[3:]
    q_out, k_out, v_out = refs[:3]
    mod = mod_ref[0]
    shift, scale = mod[:, 0:D_MODEL], mod[:, D_MODEL:2 * D_MODEL]
    h = _rms(x_ref[...], g0_ref[...]) * (1.0 + scale) + shift
    proj = _dot(h.astype(BF16), w_ref[...])
    width = DIFF_HEADS * LANES
    sm_scale = DIFF_HEAD_DIM ** -0.5 * LOG2E
    for hd in range(DIFF_HEADS):
        sl = slice(hd * LANES, (hd + 1) * LANES)
        qh = proj[:, hd * LANES:(hd + 1) * LANES]
        kh = proj[:, width + hd * LANES:width + (hd + 1) * LANES]
        vh = proj[:, 2 * width + hd * LANES:2 * width + (hd + 1) * LANES]
        if use_rope:
            qh = _rope(qh, c_ref[...], se_ref[...], so_ref[...])
            kh = _rope(kh, c_ref[...], se_ref[...], so_ref[...])
        q_out[0, hd] = (qh * sm_scale).astype(BF16)
        k_out[0, hd] = kh.astype(BF16)
        v_out[0, hd] = vh.astype(BF16)
        if want_f32:
            refs[3][:, sl] = kh
            refs[4][:, sl] = vh


def _c_in(x, mod3, modrow, g0, w_qkv, rope, want_f32, tm, seq_len):
    t = x.shape[0]
    width = DIFF_HEADS * LANES
    tok = lambda n: pl.BlockSpec((tm, n), lambda i: (i, 0))
    in_specs = [tok(D_MODEL), pl.BlockSpec((1, 1, 6 * D_MODEL), lambda i: (modrow(i), 0, 0)),
                _full((1, D_MODEL)), _full((D_MODEL, 3 * width))]
    args = [x, mod3, g0, w_qkv]
    per = seq_len // tm
    nb = t // seq_len
    if rope is not None:
        in_specs += [pl.BlockSpec((tm, LANES), lambda i: (i % per, 0))] * 3
        args += list(rope)
    heads = pl.BlockSpec((1, DIFF_HEADS, tm, LANES), lambda i: (i // per, 0, i % per, 0))
    head_shape = jax.ShapeDtypeStruct((nb, DIFF_HEADS, seq_len, LANES), BF16)
    n_f32 = 2 if want_f32 else 0
    return pl.pallas_call(
        functools.partial(_c_in_kernel, rope is not None, want_f32),
        grid=(t // tm,),
        in_specs=in_specs,
        out_specs=[heads] * 3 + [tok(width)] * n_f32,
        out_shape=[head_shape] * 3 + [jax.ShapeDtypeStruct((t, width), F32)] * n_f32,
        compiler_params=_params("arbitrary"),
        name="c_in",
    )(*args)


def _diff_attn_kernel(has_ctx, lam_init, *refs):
    if has_ctx:
        q_ref, kn, vn, kc, vc, lam_ref, g_ref, o_ref = refs
    else:
        q_ref, kn, vn, lam_ref, g_ref, o_ref = refs
    lp = lam_ref[...]
    lam = (jnp.exp(jnp.sum(lp[0:1] * lp[1:2], axis=-1, keepdims=True))
           - jnp.exp(jnp.sum(lp[2:3] * lp[3:4], axis=-1, keepdims=True)) + lam_init)
    for i in range(q_ref.shape[0]):
        k_new, k_ctx = kn.at[i], (kc.at[i] if has_ctx else None)
        sources = [(k_new, k_new, vn.at[i])]
        if has_ctx:
            sources.insert(0, (k_ctx, k_ctx, vc.at[i]))
        qh = q_ref[i]
        lane = lax.broadcasted_iota(jnp.int32, qh.shape, 1)
        zero = jnp.zeros_like(qh)
        q0 = jnp.where(lane < DIFF_HEAD_DIM, qh, zero)
        q1 = jnp.where(lane < DIFF_HEAD_DIM, zero, qh)
        (l0, acc0), (l1, acc1) = _two_map_flash(q0, q1, sources)
        o = acc0 * (1.0 / l0) - lam * (acc1 * (1.0 / l1))
        o_ref[:, i * LANES:(i + 1) * LANES] = (_rms(o, g_ref[...]) * (1.0 - lam_init)).astype(BF16)


def _diff_attn(q, k_new, v_new, ctx, lam_p, g_out, lam_init, tq, heads_per_step):
    b, nh, n, _ = q.shape
    hps = heads_per_step
    slab = lambda rows, f: pl.BlockSpec((None, hps, rows, LANES), f)
    whole = lambda bi, h, qi: (bi, h, 0, 0)
    in_specs = [slab(tq, lambda bi, h, qi: (bi, h, qi, 0)), slab(n, whole), slab(n, whole)]
    args = [q, k_new, v_new]
    if ctx is not None:
        in_specs += [slab(ctx[0].shape[2], whole)] * 2
        args += list(ctx)
    in_specs += [_full(lam_p.shape), _full(g_out.shape)]
    args += [lam_p, g_out]
    return pl.pallas_call(
        functools.partial(_diff_attn_kernel, ctx is not None, lam_init),
        grid=(b, nh // hps, n // tq),
        in_specs=in_specs,
        out_specs=pl.BlockSpec((None, tq, hps * LANES), lambda bi, h, qi: (bi, qi, h)),
        out_shape=jax.ShapeDtypeStruct((b, n, nh * LANES), BF16),
        compiler_params=_params("arbitrary", "arbitrary", "arbitrary"),
        name="diff_attn",
    )(*args)


def _prep_ab_weights(ai, ab_w_in, mla_g_q, mla_g_kv, mla_w_uq, mla_w_ukv, gla_w_gate_up, gla_b_gate):
    w = ab_w_in[ai]
    d = w.shape[0]
    o = np.cumsum([0, MLA_Q_LORA, MLA_KV_LORA, MLA_ROPE, GLA_HEADS * GLA_DK, GLA_HEADS * GLA_DK,
                   GLA_HEADS * GLA_DV, 2 * GLA_GATE_RANK, GLA_HEADS * GLA_DV])
    q_lat, kv_lat, k_rope, gq, gk, gv, ggate, gout = (w[:, o[i]:o[i + 1]] for i in range(8))
    pad = jnp.zeros((d, LANES - MLA_ROPE - 2 * GLA_GATE_RANK), w.dtype)
    win = jnp.concatenate([q_lat, k_rope, ggate, pad, kv_lat, gq, gk, gv, gout], axis=1).astype(BF16)

    head_pad = LANES - MLA_NOPE - MLA_ROPE
    wq = mla_w_uq[ai].reshape(MLA_Q_LORA, MLA_HEADS, MLA_NOPE + MLA_ROPE)
    wq = jnp.pad(wq, ((0, 0), (0, 0), (0, head_pad))).reshape(MLA_Q_LORA, MLA_HEADS * LANES).astype(BF16)
    wkv = mla_w_ukv[ai].reshape(MLA_KV_LORA, MLA_HEADS, MLA_NOPE + MLA_V)
    wk = jnp.pad(wkv[:, :, :MLA_NOPE], ((0, 0), (0, 0), (0, LANES - MLA_NOPE)))
    wk = wk.reshape(MLA_KV_LORA, MLA_HEADS * LANES).astype(BF16)
    wv = wkv[:, :, MLA_NOPE:].reshape(MLA_KV_LORA, MLA_HEADS * MLA_V).astype(BF16)

    e = np.zeros((MLA_ROPE, MLA_HEADS * LANES), np.float32)
    for hd in range(MLA_HEADS):
        e[np.arange(MLA_ROPE), hd * LANES + MLA_NOPE + np.arange(MLA_ROPE)] = 1.0

    wg = jnp.zeros((2, LANES, GLA_HEADS * GLA_DK), F32)
    for dr in range(2):
        lo = _MISC_GATE + dr * GLA_GATE_RANK
        wg = wg.at[dr, lo:lo + GLA_GATE_RANK, :].set(gla_w_gate_up[ai, dr])
    return dict(win=win, g_q=mla_g_q[ai][None, :], g_kv=mla_g_kv[ai][None, :], wq=wq, wk=wk, wv=wv,
                e=jnp.asarray(e, dtype=BF16), wg=wg.astype(BF16),
                bg=gla_b_gate[ai].reshape(2, 1, GLA_HEADS * GLA_DK))


def _diff_lambda_init(layer):
    return 0.8 - 0.6 * math.exp(-0.3 * layer)


def kernel(x_prompt, x_sample, cache_mla_ckv, cache_mla_krope, state_gla_fwd, state_gla_bwd,
           cache_diff_k, cache_diff_v, c, c_ctx, w_mod, b_mod, g_norm, ab_w_in, mla_g_q, mla_g_kv,
           mla_w_uq, mla_w_ukv, gla_w_gate_up, gla_b_gate, gla_g_out, ab_w_out, c_w_qkv, diff_lambda,
           diff_g_out, c_w_out, w_ffn_in, w_ffn_out):
    depth = w_mod.shape[0]
    bp, lp, d = x_prompt.shape
    bs, ls, _ = x_sample.shape
    n_ctx = cache_mla_ckv.shape[2]

    n_cond = 1 + bs
    rows = -(-n_cond // 8) * 8
    conds = jnp.concatenate([c_ctx[None, :], c, jnp.zeros((rows - n_cond, d), F32)], axis=0)
    mod3 = _modulation(conds, w_mod, b_mod).reshape(depth * rows, 1, 6 * d)

    per_s = ls // min(TOKEN_TILE, ls)
    rope_q = _rope_tables(ls, MLA_ROPE, MLA_NOPE, LANES)
    rope_misc = _rope_tables(ls, MLA_ROPE, 0, LANES)
    rope_diff = _rope_tables(ls, DIFF_HEAD_DIM, 0, DIFF_HEAD_DIM)

    groups = {
        "prompt": dict(x=x_prompt.reshape(bp * lp, d), b=bp, n=lp, rope=False,
                       modrow=lambda li: (lambda i: li * rows)),
        "sample": dict(x=x_sample.reshape(bs * ls, d), b=bs, n=ls, rope=True,
                       modrow=lambda li: (lambda i: li * rows + 1 + i // per_s)),
    }
    ab_states, c_states = [], []

    ab_w = [_prep_ab_weights(ai, ab_w_in, mla_g_q, mla_g_kv, mla_w_uq, mla_w_ukv, gla_w_gate_up, gla_b_gate)
            for ai in range(ab_w_in.shape[0])]
    ab_w_out_b = ab_w_out.astype(BF16)
    c_w_qkv_b = c_w_qkv.astype(BF16)
    c_w_out_b = c_w_out.astype(BF16)
    w_ffn_in_b = w_ffn_in.astype(BF16)
    w_ffn_out_b = w_ffn_out.astype(BF16)

    for name, grp in groups.items():
        x, b, n = grp["x"], grp["b"], grp["n"]
        is_sample = grp["rope"]
        tm = min(TOKEN_TILE, n)
        for li in range(depth):
            modrow = grp["modrow"](li)
            g = g_norm[li]
            if li % 2 == 0:
                ai = li // 2
                w = ab_w[ai]
                rope = rope_q + rope_misc if is_sample else None
                q, kcat, v, ckv, misc, gq, gk, gv, gout = _ab_in(x, mod3, modrow, g[0:1], w, rope, tm, n)
                sh = lambda a: a.reshape(b, n, a.shape[-1])
                if is_sample:
                    ctx = _ctx_kv(cache_mla_ckv[:, ai], cache_mla_krope[:, ai], w)
                    s0 = (state_gla_fwd[:, ai], state_gla_bwd[:, ai])
                else:
                    ctx, s0 = None, None
                mla_o = _mla_attn(q, kcat, v, ctx, min(ATTN_Q_TILE, n),
                                  max(1, min(MLA_HEADS // 2, ATTN_STEP_ROWS // n)))
                o_f, o_b, s_f, s_b = _gla(sh(gq), sh(gk), sh(gv), sh(misc), w, s0, min(GLA_BLOCK, n))
                if not is_sample:
                    ab_states.append((ckv.reshape(b, n, MLA_KV_LORA), misc[:, :MLA_ROPE].reshape(b, n, MLA_ROPE),
                                      s_f, s_b))
                mix = mla_o.reshape(b * n, -1)
                gla = (o_f.reshape(b * n, -1), o_b.reshape(b * n, -1), gout, gla_g_out[ai][None, :])
                w_mix = ab_w_out_b[ai]
            else:
                ci = li // 2
                rope = rope_diff if is_sample else None
                outs = _c_in(x, mod3, modrow, g[0:1], c_w_qkv_b[ci], rope, not is_sample, tm, n)
                q, k, v = outs[:3]
                if is_sample:
                    head_major = lambda a: jnp.swapaxes(a.reshape(bs, n_ctx, DIFF_HEADS, LANES), 1, 2).astype(BF16)
                    ctx = (head_major(cache_diff_k[:, ci]), head_major(cache_diff_v[:, ci]))
                else:
                    ctx = None
                    c_states.append((outs[3].reshape(b, n, DIFF_HEADS, 2, DIFF_HEAD_DIM),
                                     outs[4].reshape(b, n, DIFF_HEADS, 2 * DIFF_HEAD_DIM)))
                o = _diff_attn(q, k, v, ctx, diff_lambda[ci], diff_g_out[ci][None, :],
                               _diff_lambda_init(li), min(ATTN_Q_TILE, n),
                               max(1, min(DIFF_HEADS, ATTN_STEP_ROWS // n)))
                mix, gla, w_mix = o.reshape(b * n, -1), None, c_w_out_b[ci]
            tm_tail = tm if is_sample else min(TOKEN_TILE, b * n)
            x = _layer_tail(mix, gla, w_mix, x, mod3, modrow, g, w_ffn_in_b[li], w_ffn_out_b[li], tm_tail)
        grp["y"] = x.reshape(b, n, d)

    stack = lambda states, i: jnp.stack([s[i] for s in states], axis=1)
    return (groups["prompt"]["y"], groups["sample"]["y"],
            stack(ab_states, 0), stack(ab_states, 1), stack(ab_states, 2), stack(ab_states, 3),
            stack(c_states, 0), stack(c_states, 1))
```
